```python
import math
import jax, jax.numpy as jnp
from jax import lax
import numpy as np

D_MODEL = 4096
BATCH = 1
SEQ = 8192
DEPTH = 1
DEC_BATCH = 128
DEC_SEQ = 8
PAST_LEN = 2048
PAGE_SIZE = 128

D_CONV = D_MODEL // 2
N_HEADS = 16
HEAD_DIM = 128
N_KV = 4
HPG = N_HEADS // N_KV
CMP_STRIDE = 16
CMP_LEN = 2 * CMP_STRIDE
CMP_HID = 256
SEL_BLOCK = 64
N_SEL = 16
WINDOW = 512
FORCE_SCORE = 1e4
REL_BUCKETS = 32
REL_EXACT = REL_BUCKETS // 2
REL_MAX_DIST = 128
CONV_W = 3
D_FF = 256 * ((8 * D_MODEL // 3 + 255) // 256)
Q_BLOCK = 128
ALPHA = (2.0 * DEPTH) ** 0.25
BETA = (8.0 * DEPTH) ** -0.25
LN_EPS = 1e-5
IN_SIZES = (D_CONV, D_CONV, D_CONV, N_HEADS * HEAD_DIM,
            N_KV * HEAD_DIM, N_KV * HEAD_DIM, N_KV * HEAD_DIM, N_KV * HEAD_DIM, N_KV * HEAD_DIM, N_KV * HEAD_DIM,
            3 * N_HEADS, 2 * D_MODEL)

kernel_name = 'nsa_shortconv_hybrid_step'


def layer_norm(x, g, b):
    xf = x.astype(jnp.float32)
    mu = xf.mean(-1, keepdims=True)
    var = jnp.square(xf - mu).mean(-1, keepdims=True)
    y = (xf - mu) * lax.rsqrt(var + LN_EPS) * g.astype(jnp.float32) + b.astype(jnp.float32)
    return y.astype(x.dtype)


def masked_softmax(s, mask):
    s = jnp.where(mask, s, -jnp.inf)
    m = jnp.max(s, axis=-1, keepdims=True)
    m = jnp.where(jnp.isfinite(m), m, 0.0)
    e = jnp.where(mask, jnp.exp(s - m), 0.0)
    return e / jnp.maximum(e.sum(-1, keepdims=True), 1e-30)


def rel_bucket(dist):
    n = jnp.maximum(dist, 0)
    nf = jnp.maximum(n, 1).astype(jnp.float32)
    large = REL_EXACT + (jnp.log(nf / REL_EXACT) / math.log(REL_MAX_DIST / REL_EXACT)
                         * (REL_BUCKETS - REL_EXACT)).astype(jnp.int32)
    return jnp.where(n < REL_EXACT, n, jnp.minimum(large, REL_BUCKETS - 1))


def causal_conv3(u, prev, w, b):
    full = jnp.concatenate([prev.astype(u.dtype), u], axis=1)
    L = u.shape[1]
    out = b
    for i in range(CONV_W):
        out = out + full[:, i:i + L] * w[i]
    return out, full[:, -(CONV_W - 1):]


def gather_pages(pool, page_table):
    g = pool[page_table]
    return g.reshape(page_table.shape[0], -1, pool.shape[2], pool.shape[3])


def compress(k, pe, w1, w2):
    B, T = k.shape[:2]
    n_chunk = T // CMP_STRIDE
    half = CMP_STRIDE * HEAD_DIM
    c = k[:, :n_chunk * CMP_STRIDE].reshape(B, n_chunk, CMP_STRIDE, N_KV, HEAD_DIM)
    c = c.transpose(0, 1, 3, 2, 4).reshape(B, n_chunk, N_KV, half)
    w1a, w1b = w1[:half], w1[half:]
    pe_flat = pe.reshape(2, half)
    first = c @ w1a + pe_flat[0] @ w1a
    second = c @ w1b + pe_flat[1] @ w1b
    hid = jax.nn.gelu(first[:, :-1] + second[:, 1:])
    return hid @ w2


def nsa_attention(q, gates, kc_all, vc_all, ks_all, vs_all, kw_ctx, vw_ctx, kw_start, pos0, qb, prm):
    B, L = q.shape[:2]
    T = kc_all.shape[1]
    f32 = jnp.float32
    k_cmp = compress(kc_all, prm['pe_cmp_k'], prm['w_cmp_k1'], prm['w_cmp_k2'])
    v_cmp = compress(vc_all, prm['pe_cmp_v'], prm['w_cmp_v1'], prm['w_cmp_v2'])
    n_cmp = k_cmp.shape[1]
    c_end = jnp.arange(n_cmp, dtype=jnp.int32) * CMP_STRIDE + (CMP_LEN - 1)
    n_sel = -(-T // SEL_BLOCK)
    top = min(N_SEL, n_sel)
    per_sel = SEL_BLOCK // CMP_STRIDE
    pad_t = n_sel * SEL_BLOCK - T

    def to_blocks(a):
        a = jnp.pad(a, ((0, 0), (0, pad_t), (0, 0), (0, 0)))
        return a.reshape(B, n_sel, SEL_BLOCK, N_KV, HEAD_DIM).transpose(0, 3, 1, 2, 4)

    ks_blk, vs_blk = to_blocks(ks_all), to_blocks(vs_all)
    zpad = jnp.zeros((B, WINDOW, N_KV, HEAD_DIM), kw_ctx.dtype)
    kw_pad = jnp.concatenate([zpad, kw_ctx], axis=1)
    vw_pad = jnp.concatenate([zpad, vw_ctx], axis=1)
    scale = HEAD_DIM ** -0.5
    rb = prm['rel_bias'].astype(f32)
    rb_g = rb.reshape(REL_BUCKETS, N_KV, HPG)
    b_ix = jnp.arange(B)[:, None, None, None]
    g_ix = jnp.arange(N_KV)[None, :, None, None]
    sel_off = jnp.arange(SEL_BLOCK, dtype=jnp.int32)
    j_sel = jnp.arange(n_sel, dtype=jnp.int32)
    w_off = jnp.arange(WINDOW + qb, dtype=jnp.int32)

    def head_bias(dist):
        return rb[rel_bucket(dist)].reshape(dist.shape + (N_KV, HPG)).transpose(2, 3, 0, 1)

    def block(args):
        q_blk, g_blk, t = args
        s_c = jnp.einsum('bqgnd,bcgd->bgnqc', q_blk, k_cmp, preferred_element_type=f32) * scale
        dist_c = t[:, None] - c_end[None, :]
        p_c = masked_softmax(s_c + head_bias(dist_c), dist_c >= 0)
        o_c = jnp.einsum('bgnqc,bcgd->bqgnd', p_c.astype(v_cmp.dtype), v_cmp)
        imp = p_c.sum(axis=2)
        imp = jnp.pad(imp, ((0, 0), (0, 0), (0, 0), (0, n_sel * per_sel - n_cmp)))
        imp = imp.reshape(B, N_KV, qb, n_sel, per_sel).sum(-1)
        cur = (t // SEL_BLOCK)[:, None]
        valid_j = j_sel[None] <= cur
        forced = (j_sel[None] == 0) | (j_sel[None] == cur) | (j_sel[None] == cur - 1)
        score = jnp.where(valid_j, jnp.where(forced, FORCE_SCORE, imp), -jnp.inf)
        _, idx = lax.top_k(score, top)
        k_g = ks_blk[b_ix, g_ix, idx].reshape(B, N_KV, qb, top * SEL_BLOCK, HEAD_DIM)
        v_g = vs_blk[b_ix, g_ix, idx].reshape(B, N_KV, qb, top * SEL_BLOCK, HEAD_DIM)
        pos_s = (idx[..., None] * SEL_BLOCK + sel_off).reshape(B, N_KV, qb, top * SEL_BLOCK)
        dist_s = t[None, None, :, None] - pos_s
        bias_s = jnp.moveaxis(rb_g[rel_bucket(dist_s), g_ix], -1, 2)
        s_s = jnp.einsum('bqgnd,bgqkd->bgnqk', q_blk, k_g, preferred_element_type=f32) * scale + bias_s
        p_s = masked_softmax(s_s, (dist_s >= 0)[:, :, None])
        o_s = jnp.einsum('bgnqk,bgqkd->bqgnd', p_s.astype(v_g.dtype), v_g)
        start = t[0] - kw_start
        k_w = lax.dynamic_slice_in_dim(kw_pad, start, WINDOW + qb, axis=1)
        v_w = lax.dynamic_slice_in_dim(vw_pad, start, WINDOW + qb, axis=1)
        pos_w = t[0] - WINDOW + w_off
        dist_w = t[:, None] - pos_w[None, :]
        valid_w = (dist_w >= 0) & (dist_w < WINDOW) & (pos_w[None, :] >= 0)
        s_w = jnp.einsum('bqgnd,bkgd->bgnqk', q_blk, k_w, preferred_element_type=f32) * scale + head_bias(dist_w)
        p_w = masked_softmax(s_w, valid_w)
        o_w = jnp.einsum('bgnqk,bkgd->bqgnd', p_w.astype(v_w.dtype), v_w)
        o = (g_blk[:, :, 0, :, :, None] * o_c + g_blk[:, :, 1, :, :, None] * o_s
             + g_blk[:, :, 2, :, :, None] * o_w)
        return o.reshape(B, qb, N_HEADS * HEAD_DIM)

    nb = L // qb
    q_b = q.reshape(B, nb, qb, N_KV, HPG, HEAD_DIM).swapaxes(0, 1)
    g_b = gates.reshape(B, nb, qb, 3, N_KV, HPG).swapaxes(0, 1)
    pos_b = (pos0 + jnp.arange(L, dtype=jnp.int32)).reshape(nb, qb)
    out = lax.map(block, (q_b, g_b, pos_b))
    return out.swapaxes(0, 1).reshape(B, L, N_HEADS * HEAD_DIM)


def hybrid_layer(x, pos0, qb, conv_prev, ffn_prev, past, win_prev, prm):
    B, L, _ = x.shape
    splits = np.cumsum(IN_SIZES)[:-1].tolist()
    cb, cc, cv, q, kc, vc, ks, vs, kw, vw, g_nsa, g_mix = jnp.split(x @ prm['w_in'], splits, axis=-1)
    conv_out, conv_new = causal_conv3(cc * cv, conv_prev, prm['conv_w'], prm['conv_b'])
    a = (cb * conv_out) @ prm['w_br_conv']
    new_rows = [t.reshape(B, L, N_KV, HEAD_DIM) for t in (kc, vc, ks, vs)]
    if past is None:
        ctx = new_rows
    else:
        ctx = [jnp.concatenate([p_.astype(n_.dtype), n_], axis=1) for p_, n_ in zip(past, new_rows)]
    kw_new = kw.reshape(B, L, N_KV, HEAD_DIM)
    vw_new = vw.reshape(B, L, N_KV, HEAD_DIM)
    if win_prev is None:
        kw_ctx, vw_ctx, kw_start = kw_new, vw_new, pos0
    else:
        kw_ctx = jnp.concatenate([win_prev[0].astype(kw_new.dtype), kw_new], axis=1)
        vw_ctx = jnp.concatenate([win_prev[1].astype(vw_new.dtype), vw_new], axis=1)
        kw_start = pos0 - win_prev[0].shape[1]
    gates = jax.nn.sigmoid(g_nsa).reshape(B, L, 3, N_KV, HPG)
    o = nsa_attention(q.reshape(B, L, N_KV, HPG, HEAD_DIM), gates, ctx[0], ctx[1], ctx[2], ctx[3],
                      kw_ctx, vw_ctx, kw_start, pos0, qb, prm)
    b = o @ prm['w_br_nsa']
    gm = jax.nn.sigmoid(g_mix).reshape(B, L, 2, D_MODEL)
    mix = (gm[:, :, 0] * a + gm[:, :, 1] * b) @ prm['w_out']
    h = layer_norm(ALPHA * x + mix, prm['ln1_g'], prm['ln1_b'])
    gate_pre, val = jnp.split(h @ prm['w_ffn_in'], 2, axis=-1)
    gconv, ffn_new = causal_conv3(gate_pre, ffn_prev, prm['ffn_conv_w'], prm['ffn_conv_b'])
    f = (jax.nn.silu(gconv) * val) @ prm['w_ffn_out']
    y = layer_norm(ALPHA * h + f, prm['ln2_g'], prm['ln2_b'])
    n_keep = min(WINDOW, kw_ctx.shape[1])
    return y, (new_rows[0], new_rows[1], new_rows[2], new_rows[3],
               kw_ctx[:, -n_keep:], vw_ctx[:, -n_keep:], conv_new, ffn_new)


def setup_inputs(seed: int = 0) -> dict:
    key = jax.random.key(seed)
    keys = iter(jax.random.split(key, 64))

    def nrm(shape, scale):
        return jax.random.normal(next(keys), shape, jnp.float32) * scale

    n_pages = PAST_LEN // PAGE_SIZE
    n_phys = (5 * DEC_BATCH * n_pages) // 4
    win_buf = min(WINDOW, PAST_LEN)
    kv_page = (DEPTH, n_phys, PAGE_SIZE, N_KV, HEAD_DIM)
    win_shape = (DEPTH, DEC_BATCH, win_buf, N_KV, HEAD_DIM)
    in_scale = (1.0, 1.0, BETA, 1.0, 1.0, BETA, 1.0, BETA, 1.0, BETA, 1.0, 1.0)
    page_table = jax.random.permutation(next(keys), n_phys)[:DEC_BATCH * n_pages]
    page_table = page_table.reshape(DEC_BATCH, n_pages).astype(jnp.int32)
    w_in = jnp.concatenate([nrm((DEPTH, D_MODEL, sz), sc * D_MODEL ** -0.5)
                            for sz, sc in zip(IN_SIZES, in_scale)], axis=-1)
    flat = CMP_LEN * HEAD_DIM
    return {
        'x_prompt': nrm((BATCH, SEQ, D_MODEL), 1.0),
        'x_sample': nrm((DEC_BATCH, DEC_SEQ, D_MODEL), 1.0),
        'cache_k_cmp': nrm(kv_page, 1.0),
        'cache_v_cmp': nrm(kv_page, 1.0),
        'cache_k_sel': nrm(kv_page, 1.0),
        'cache_v_sel': nrm(kv_page, 1.0),
        'state_k_win': nrm(win_shape, 1.0),
        'state_v_win': nrm(win_shape, 1.0),
        'state_conv': nrm((DEPTH, DEC_BATCH, CONV_W - 1, D_CONV), 0.6),
        'state_ffn_conv': nrm((DEPTH, DEC_BATCH, CONV_W - 1, D_FF), 0.6),
        'page_table': page_table,
        'rel_bias': nrm((REL_BUCKETS, N_HEADS), 0.5),
        'w_in': w_in,
        'conv_w': nrm((DEPTH, CONV_W, D_CONV), CONV_W ** -0.5),
        'conv_b': nrm((DEPTH, D_CONV), 0.01),
        'w_br_conv': nrm((DEPTH, D_CONV, D_MODEL), BETA * D_CONV ** -0.5),
        'w_br_nsa': nrm((DEPTH, N_HEADS * HEAD_DIM, D_MODEL), BETA * (N_HEADS * HEAD_DIM) ** -0.5),
        'w_out': nrm((DEPTH, D_MODEL, D_MODEL), BETA * D_MODEL ** -0.5),
        'pe_cmp_k': nrm((DEPTH, CMP_LEN, HEAD_DIM), 0.5),
        'w_cmp_k1': nrm((DEPTH, flat, CMP_HID), flat ** -0.5),
        'w_cmp_k2': nrm((DEPTH, CMP_HID, HEAD_DIM), 2.0 * CMP_HID ** -0.5),
        'pe_cmp_v': nrm((DEPTH, CMP_LEN, HEAD_DIM), 0.5),
        'w_cmp_v1': nrm((DEPTH, flat, CMP_HID), flat ** -0.5),
        'w_cmp_v2': nrm((DEPTH, CMP_HID, HEAD_DIM), 2.0 * CMP_HID ** -0.5),
        'ln1_g': 1.0 + nrm((DEPTH, D_MODEL), 0.02),
        'ln1_b': nrm((DEPTH, D_MODEL), 0.02),
        'w_ffn_in': nrm((DEPTH, D_MODEL, 2 * D_FF), BETA * D_MODEL ** -0.5),
        'ffn_conv_w': nrm((DEPTH, CONV_W, D_FF), CONV_W ** -0.5),
        'ffn_conv_b': nrm((DEPTH, D_FF), 0.01),
        'w_ffn_out': nrm((DEPTH, D_FF, D_MODEL), BETA * D_FF ** -0.5),
        'ln2_g': 1.0 + nrm((DEPTH, D_MODEL), 0.02),
        'ln2_b': nrm((DEPTH, D_MODEL), 0.02),
    }


def reference(x_prompt, x_sample, cache_k_cmp, cache_v_cmp, cache_k_sel, cache_v_sel, state_k_win, state_v_win,
              state_conv, state_ffn_conv, page_table, rel_bias, w_in, conv_w, conv_b, w_br_conv, w_br_nsa, w_out,
              pe_cmp_k, w_cmp_k1, w_cmp_k2, pe_cmp_v, w_cmp_v1, w_cmp_v2, ln1_g, ln1_b, w_ffn_in, ffn_conv_w,
              ffn_conv_b, w_ffn_out, ln2_g, ln2_b):
    outs_p, outs_s = [], []
    h_p, h_s = x_prompt, x_sample
    for l in range(DEPTH):
        prm = {'w_in': w_in[l], 'conv_w': conv_w[l], 'conv_b': conv_b[l], 'w_br_conv': w_br_conv[l],
               'w_br_nsa': w_br_nsa[l], 'w_out': w_out[l], 'pe_cmp_k': pe_cmp_k[l], 'w_cmp_k1': w_cmp_k1[l],
               'w_cmp_k2': w_cmp_k2[l], 'pe_cmp_v': pe_cmp_v[l], 'w_cmp_v1': w_cmp_v1[l], 'w_cmp_v2': w_cmp_v2[l],
               'ln1_g': ln1_g[l], 'ln1_b': ln1_b[l], 'w_ffn_in': w_ffn_in[l], 'ffn_conv_w': ffn_conv_w[l],
               'ffn_conv_b': ffn_conv_b[l], 'w_ffn_out': w_ffn_out[l], 'ln2_g': ln2_g[l], 'ln2_b': ln2_b[l],
               'rel_bias': rel_bias}
        zc = jnp.zeros((h_p.shape[0], CONV_W - 1, D_CONV), h_p.dtype)
        zf = jnp.zeros((h_p.shape[0], CONV_W - 1, D_FF), h_p.dtype)
        h_p, st_p = hybrid_layer(h_p, 0, min(Q_BLOCK, h_p.shape[1]), zc, zf, None, None, prm)
        past = (gather_pages(cache_k_cmp[l], page_table), gather_pages(cache_v_cmp[l], page_table),
                gather_pages(cache_k_sel[l], page_table), gather_pages(cache_v_sel[l], page_table))
        h_s, st_s = hybrid_layer(h_s, PAST_LEN, 1, state_conv[l], state_ffn_conv[l], past,
                                 (state_k_win[l], state_v_win[l]), prm)
        outs_p.append(st_p)
        outs_s.append(st_s)
    kcp, vcp, ksp, vsp, kwp, vwp, cvp, ffp = [jnp.stack([st[i] for st in outs_p]) for i in range(8)]
    kcs, vcs, kss, vss, kws, vws, cvs, ffs = [jnp.stack([st[i] for st in outs_s]) for i in range(8)]
    return (h_p, h_s, kcp, kcs, vcp, vcs, ksp, kss, vsp, vss, kwp, kws, vwp, vws, cvp, cvs, ffp, ffs)
```

```python
import functools
import math

import jax
import jax.numpy as jnp
from jax import lax
from jax.experimental import pallas as pl
from jax.experimental.pallas import tpu as pltpu

F32 = jnp.float32
BF16 = jnp.bfloat16

D_MODEL = 4096
D_CONV = 2048
N_HEADS = 16
HEAD_DIM = 128
N_KV = 4
HPG = 4
CMP_STRIDE = 16
CMP_HID = 256
SEL_BLOCK = 64
N_SEL = 16
WINDOW = 512
FORCE_SCORE = 1e4
REL_BUCKETS = 32
REL_EXACT = 16
REL_MAX_DIST = 128
D_FF = 11008
PAGE = 128
ALPHA = 2.0 ** 0.25
LN_EPS = 1e-5
SCALE = HEAD_DIM ** -0.5

KEY_TILE = 128
NEG = -1e30
MIB = 1024 * 1024


def _params(sem, vmem_mib):
    return pltpu.CompilerParams(dimension_semantics=sem, vmem_limit_bytes=vmem_mib * MIB)


def _dot(a, b):
    return jnp.dot(a, b, preferred_element_type=F32)


def _proj_kernel(x_ref, w_ref, of_ref, ob_ref):
    acc = _dot(x_ref[...], w_ref[...])
    of_ref[...] = acc
    ob_ref[...] = acc.astype(BF16)


def _proj(x, w, col0, ncols, tm, tn):
    m_rows, k = x.shape
    c0 = col0 // tn
    return pl.pallas_call(
        _proj_kernel,
        grid=(m_rows // tm, ncols // tn),
        in_specs=[pl.BlockSpec((tm, k), lambda m, n: (m, 0)),
                  pl.BlockSpec((k, tn), lambda m, n: (0, c0 + n))],
        out_specs=[pl.BlockSpec((tm, tn), lambda m, n: (m, n)),
                   pl.BlockSpec((tm, tn), lambda m, n: (m, n))],
        out_shape=[jax.ShapeDtypeStruct((m_rows, ncols), F32),
                   jax.ShapeDtypeStruct((m_rows, ncols), BF16)],
        compiler_params=_params(("arbitrary", "arbitrary"), 48),
        name="proj_qkv",
    )(x, w)


def _gate_kernel(x_ref, w_ref, o_ref):
    o_ref[...] = jax.nn.sigmoid(_dot(x_ref[...], w_ref[...]))


def _gate_proj(x, w, tm):
    m_rows, k = x.shape
    n = w.shape[1]
    return pl.pallas_call(
        _gate_kernel,
        grid=(m_rows // tm,),
        in_specs=[pl.BlockSpec((tm, k), lambda m: (m, 0)),
                  pl.BlockSpec((k, n), lambda m: (0, 0))],
        out_specs=pl.BlockSpec((tm, n), lambda m: (m, 0)),
        out_shape=jax.ShapeDtypeStruct((m_rows, n), F32),
        compiler_params=_params(("arbitrary",), 40),
        name="proj_gates",
    )(x, w)


def _gconv_kernel(*refs, three, n_prompt_tiles):
    if three:
        x_ref, w0_ref, w1_ref, w2_ref, cw_ref, cb_ref, s1_ref, s2_ref, z_ref, u_ref, carry_ref = refs
    else:
        x_ref, w0_ref, w1_ref, cw_ref, cb_ref, s1_ref, s2_ref, z_ref, u_ref, carry_ref = refs
    m = pl.program_id(0)
    n = pl.program_id(1)
    x = x_ref[...]
    p0 = _dot(x, w0_ref[...])
    u = _dot(x, w1_ref[...])
    if three:
        u = u * _dot(x, w2_ref[...])
    u_ref[...] = u
    tm = u.shape[0]
    row = lax.broadcasted_iota(jnp.int32, u.shape, 0)
    sh1 = pltpu.roll(u, 1, 0)
    sh2 = pltpu.roll(u, 2, 0)
    cw = cw_ref[...]
    cb = cb_ref[...]

    def finish(prev1, prev2):
        conv = cb + prev2 * cw[0:1] + prev1 * cw[1:2] + u * cw[2:3]
        if three:
            z = p0 * conv
        else:
            z = jax.nn.silu(conv) * p0
        z_ref[...] = z.astype(z_ref.dtype)

    @pl.when(m < n_prompt_tiles)
    def _():
        c = carry_ref[n]
        c = jnp.where(m == 0, jnp.zeros_like(c), c)
        c0 = c[6:7]
        c1 = c[7:8]
        prev1 = jnp.where(row == 0, c1, sh1)
        prev2 = jnp.where(row == 0, c0, jnp.where(row == 1, c1, sh2))
        carry_ref[n] = u[tm - 8:tm]
        finish(prev1, prev2)

    @pl.when(m >= n_prompt_tiles)
    def _():
        r8 = row & 7
        prev1 = jnp.where(r8 == 0, s1_ref[...], sh1)
        prev2 = jnp.where(r8 < 2, s2_ref[...], sh2)
        finish(prev1, prev2)


def _gconv(x, w, col_blocks, cw, cb, s1, s2, n_prompt_rows, tm, tn, vmem_mib, name):
    m_rows, k = x.shape
    ncols = cw.shape[1]
    three = len(col_blocks) == 3
    assert m_rows - n_prompt_rows == tm and n_prompt_rows % tm == 0
    nt = ncols // tn
    w_specs = [pl.BlockSpec((k, tn), functools.partial(lambda m, n, c: (0, c + n), c=c)) for c in col_blocks]
    col_spec = pl.BlockSpec((tm, tn), lambda m, n: (m, n))
    return pl.pallas_call(
        functools.partial(_gconv_kernel, three=three, n_prompt_tiles=n_prompt_rows // tm),
        grid=(m_rows // tm, nt),
        in_specs=[pl.BlockSpec((tm, k), lambda m, n: (m, 0))] + w_specs + [
            pl.BlockSpec((3, tn), lambda m, n: (0, n)),
            pl.BlockSpec((1, tn), lambda m, n: (0, n)),
            pl.BlockSpec((tm, tn), lambda m, n: (0, n)),
            pl.BlockSpec((tm, tn), lambda m, n: (0, n))],
        out_specs=[col_spec, col_spec],
        out_shape=[jax.ShapeDtypeStruct((m_rows, ncols), BF16),
                   jax.ShapeDtypeStruct((m_rows, ncols), F32)],
        scratch_shapes=[pltpu.VMEM((nt, 8, tn), F32)],
        compiler_params=_params(("arbitrary", "arbitrary"), vmem_mib),
        name=name,
    )(x, *([w] * len(col_blocks)), cw, cb, s1, s2)


def _merge_kernel(x_ref, z_ref, o_ref, wc_ref, wn_ref, wga_ref, wgb_ref, out_ref):
    x = x_ref[...]
    a = _dot(z_ref[...], wc_ref[...])
    b = _dot(o_ref[...], wn_ref[...])
    ga = jax.nn.sigmoid(_dot(x, wga_ref[...]))
    gb = jax.nn.sigmoid(_dot(x, wgb_ref[...]))
    out_ref[...] = (ga * a + gb * b).astype(BF16)


def _merge(x, z, o, wc, wn, wg, tm, tn):
    m_rows, k = x.shape
    kc = z.shape[1]
    nt = D_MODEL // tn
    return pl.pallas_call(
        _merge_kernel,
        grid=(m_rows // tm, nt),
        in_specs=[pl.BlockSpec((tm, k), lambda m, n: (m, 0)),
                  pl.BlockSpec((tm, kc), lambda m, n: (m, 0)),
                  pl.BlockSpec((tm, kc), lambda m, n: (m, 0)),
                  pl.BlockSpec((kc, tn), lambda m, n: (0, n)),
                  pl.BlockSpec((kc, tn), lambda m, n: (0, n)),
                  pl.BlockSpec((k, tn), lambda m, n: (0, n)),
                  pl.BlockSpec((k, tn), lambda m, n: (0, nt + n))],
        out_specs=pl.BlockSpec((tm, tn), lambda m, n: (m, n)),
        out_shape=jax.ShapeDtypeStruct((m_rows, D_MODEL), BF16),
        compiler_params=_params(("arbitrary", "arbitrary"), 56),
        name="merge_branches",
    )(x, z, o, wc, wn, wg, wg)


def _resid_kernel(l_ref, w_ref, r_ref, o_ref):
    o_ref[...] = ALPHA * r_ref[...] + _dot(l_ref[...], w_ref[...])


def _resid_mm(lhs, w, res, tm, tn, vmem_mib, name):
    m_rows, k = lhs.shape
    n_cols = w.shape[1]
    return pl.pallas_call(
        _resid_kernel,
        grid=(m_rows // tm, n_cols // tn),
        in_specs=[pl.BlockSpec((tm, k), lambda m, n: (m, 0)),
                  pl.BlockSpec((k, tn), lambda m, n: (0, n)),
                  pl.BlockSpec((tm, tn), lambda m, n: (m, n))],
        out_specs=pl.BlockSpec((tm, tn), lambda m, n: (m, n)),
        out_shape=jax.ShapeDtypeStruct((m_rows, n_cols), F32),
        compiler_params=_params(("arbitrary", "arbitrary"), vmem_mib),
        name=name,
    )(lhs, w, res)


def _ln_kernel(x_ref, g_ref, b_ref, of_ref, ob_ref):
    x = x_ref[...]
    mu = jnp.mean(x, axis=-1, keepdims=True)
    xc = x - mu
    var = jnp.mean(xc * xc, axis=-1, keepdims=True)
    y = xc * lax.rsqrt(var + LN_EPS) * g_ref[...] + b_ref[...]
    of_ref[...] = y
    ob_ref[...] = y.astype(BF16)


def _layer_norm(x, g, b, tr, name):
    m_rows, d = x.shape
    row_spec = pl.BlockSpec((tr, d), lambda m: (m, 0))
    vec_spec = pl.BlockSpec((1, d), lambda m: (0, 0))
    return pl.pallas_call(
        _ln_kernel,
        grid=(m_rows // tr,),
        in_specs=[row_spec, vec_spec, vec_spec],
        out_specs=[row_spec, row_spec],
        out_shape=[jax.ShapeDtypeStruct((m_rows, d), F32), jax.ShapeDtypeStruct((m_rows, d), BF16)],
        compiler_params=_params(("arbitrary",), 48),
        name=name,
    )(x, g, b)


N_CACHES = 4


def _gather_kernel(pt_ref, c0, c1, c2, c3, o0, o1, o2, o3, sem):
    b = pl.program_id(0)
    n_pages = pt_ref.shape[1]
    copies = []
    for ci, (c, o) in enumerate(((c0, o0), (c1, o1), (c2, o2), (c3, o3))):
        for j in range(n_pages):
            cp = pltpu.make_async_copy(c.at[pt_ref[b, j]], o.at[b, pl.ds(j * PAGE, PAGE)], sem.at[ci])
            cp.start()
            copies.append(cp)
    for cp in copies:
        cp.wait()


def _gather_pages(page_table, caches):
    n_seq, n_pages = page_table.shape
    width = caches[0].shape[-1]
    any_spec = pl.BlockSpec(memory_space=pl.ANY)
    return pl.pallas_call(
        _gather_kernel,
        grid_spec=pltpu.PrefetchScalarGridSpec(
            num_scalar_prefetch=1, grid=(n_seq,),
            in_specs=[any_spec] * N_CACHES, out_specs=[any_spec] * N_CACHES,
            scratch_shapes=[pltpu.SemaphoreType.DMA((N_CACHES,))]),
        out_shape=[jax.ShapeDtypeStruct((n_seq, n_pages * PAGE, width), F32)] * N_CACHES,
        compiler_params=pltpu.CompilerParams(dimension_semantics=("arbitrary",)),
        name="gather_pages",
    )(page_table, *caches)


def _compress_kernel(x_ref, pe_ref, w1_ref, w2_ref, o_ref):
    tr = x_ref.shape[0]
    w1 = w1_ref[...]
    w2 = w2_ref[...]
    pe = pe_ref[...]
    half = CMP_STRIDE * HEAD_DIM
    pa = _dot(jnp.broadcast_to(pe[0:1], (8, half)).astype(BF16), w1[:, :CMP_HID])[0:1]
    pb = _dot(jnp.broadcast_to(pe[1:2], (8, half)).astype(BF16), w1[:, CMP_HID:])[0:1]
    row_w = N_KV * HEAD_DIM
    for g in range(N_KV):
        xg = jnp.concatenate(
            [x_ref[:, r * row_w + g * HEAD_DIM: r * row_w + (g + 1) * HEAD_DIM] for r in range(CMP_STRIDE)],
            axis=1).astype(BF16)
        fs = _dot(xg, w1)
        first = fs[:, :CMP_HID] + pa
        second = fs[:, CMP_HID:] + pb
        hid = jax.nn.gelu(first + pltpu.roll(second, tr - 1, 0))
        o_ref[:, g * HEAD_DIM:(g + 1) * HEAD_DIM] = _dot(hid.astype(BF16), w2).astype(BF16)


def _compress(x2, pe2, w1cat, w2, tr, name):
    rows, width = x2.shape
    return pl.pallas_call(
        _compress_kernel,
        grid=(rows // tr,),
        in_specs=[pl.BlockSpec((tr, width), lambda t: (t, 0)),
                  pl.BlockSpec(pe2.shape, lambda t: (0, 0)),
                  pl.BlockSpec(w1cat.shape, lambda t: (0, 0)),
                  pl.BlockSpec(w2.shape, lambda t: (0, 0))],
        out_specs=pl.BlockSpec((tr, N_KV * HEAD_DIM), lambda t: (t, 0)),
        out_shape=jax.ShapeDtypeStruct((rows, N_KV * HEAD_DIM), BF16),
        compiler_params=_params(("arbitrary",), 48),
        name=name,
    )(x2, pe2, w1cat, w2)


def _bias_kernel(rb_ref, img_ref, ct_ref, *, qbp, off2):
    g = pl.program_id(0)
    r_l = HPG * qbp
    lane = lax.broadcasted_iota(jnp.int32, (1, r_l), 1)
    n_l = lane // qbp
    tbl = []
    for k in range(REL_BUCKETS):
        row = jnp.zeros((1, r_l), F32)
        for n in range(HPG):
            row = jnp.where(n_l == n, rb_ref[k, g * HPG + n], row)
        tbl.append(row)

    def bias_of(dist):
        nn = jnp.maximum(dist, 0)
        nf = jnp.maximum(nn, 1).astype(F32)
        large = REL_EXACT + (jnp.log(nf / REL_EXACT) / math.log(REL_MAX_DIST / REL_EXACT)
                             * (REL_BUCKETS - REL_EXACT)).astype(jnp.int32)
        bucket = jnp.where(nn < REL_EXACT, nn, jnp.minimum(large, REL_BUCKETS - 1))
        val = jnp.zeros(dist.shape, F32)
        for k in range(REL_BUCKETS):
            val = jnp.where(bucket == k, tbl[k], val)
        return val

    shape = (KEY_TILE, r_l)
    kj = lax.broadcasted_iota(jnp.int32, shape, 0)
    qi = lax.broadcasted_iota(jnp.int32, shape, 1) % qbp
    d0 = qi - kj
    far = jnp.broadcast_to(tbl[REL_BUCKETS - 1], shape)
    img_ref[0] = jnp.where(d0 >= 0, bias_of(d0), NEG)
    img_ref[1] = bias_of(d0 + KEY_TILE)
    img_ref[2] = far
    img_ref[3] = jnp.where(kj > qi, far, NEG)
    for r0 in range(0, ct_ref.shape[0], KEY_TILE):
        dist = qi + CMP_STRIDE * (off2 - r0 - kj) - (2 * CMP_STRIDE - 1)
        ct_ref[r0:r0 + KEY_TILE, :] = jnp.where(dist >= 0, bias_of(dist), NEG)


def _bias_images(rel_bias, qbp, nct, off2, name):
    r_l = HPG * qbp
    return pl.pallas_call(
        functools.partial(_bias_kernel, qbp=qbp, off2=off2),
        grid=(N_KV,),
        in_specs=[pl.BlockSpec(memory_space=pltpu.SMEM)],
        out_specs=[pl.BlockSpec((None, 4, KEY_TILE, r_l), lambda g: (g, 0, 0, 0)),
                   pl.BlockSpec((None, nct, r_l), lambda g: (g, 0, 0))],
        out_shape=[jax.ShapeDtypeStruct((N_KV, 4, KEY_TILE, r_l), F32),
                   jax.ShapeDtypeStruct((N_KV, nct, r_l), F32)],
        compiler_params=_params(("arbitrary",), 32),
        name=name,
    )(rel_bias)


def _attn_kernel(*refs, qbp, n_out, nc, nb, off2, sample):
    if sample:
        (q_ref, gt_ref, kc_ref, vc_ref, ks_ref, vs_ref, kw_ref, vw_ref, kst_ref, vst_ref, kwt_ref, vwt_ref,
         img_ref, ct_ref, o_ref, selm_ref, ps_ref) = refs
        i_tile = (ks_ref.shape[0]) // KEY_TILE
    else:
        (q_ref, gt_ref, kc_ref, vc_ref, ks_ref, vs_ref, kw_ref, vw_ref,
         img_ref, ct_ref, o_ref, selm_ref, ps_ref) = refs
        i_tile = pl.program_id(1)
    r_l = HPG * qbp
    t0 = i_tile * KEY_TILE

    q = q_ref[...]
    qg = jnp.concatenate([q[:, n * HEAD_DIM:(n + 1) * HEAD_DIM] for n in range(HPG)], axis=0)
    q_t = qg.T.astype(BF16)

    def pv(v_tile, p):
        return lax.dot_general(v_tile, p.astype(BF16), (((0,), (0,)), ((), ())), preferred_element_type=F32)

    if sample:
        cbias = ct_ref[...]
    else:
        start = pl.multiple_of(off2 - i_tile * (KEY_TILE // CMP_STRIDE), 8)
        cbias = ct_ref[pl.ds(start, nc), :]
    s = _dot(kc_ref[...], q_t) * SCALE + cbias
    valid = cbias > 0.5 * NEG
    mx = jnp.max(s, axis=0, keepdims=True)
    mx = jnp.where(mx > 0.5 * NEG, mx, 0.0)
    e = jnp.where(valid, jnp.exp(s - mx), 0.0)
    p_c = e / jnp.maximum(jnp.sum(e, axis=0, keepdims=True), 1e-30)
    o_c = pv(vc_ref[...], p_c)

    if qbp == KEY_TILE:
        psum = p_c[:, 0:qbp]
        for n in range(1, HPG):
            psum = psum + p_c[:, n * qbp:(n + 1) * qbp]
    else:
        psum = p_c
        for n in range(1, HPG):
            psum = psum + pltpu.roll(p_c, n * qbp, 1)
    ps_ref[...] = psum
    per_sel = SEL_BLOCK // CMP_STRIDE
    nbi = nc // per_sel
    imp = ps_ref[pl.ds(0, nbi, stride=per_sel), :]
    for c in range(1, per_sel):
        imp = imp + ps_ref[pl.ds(c, nbi, stride=per_sel), :]
    if nb > nbi:
        imp = jnp.concatenate([imp, jnp.zeros((nb - nbi, KEY_TILE), F32)], axis=0)
    jb = lax.broadcasted_iota(jnp.int32, (nb, KEY_TILE), 0)
    tq = t0 + lax.broadcasted_iota(jnp.int32, (nb, KEY_TILE), 1) % qbp
    cur = tq // SEL_BLOCK
    forced = (jb == 0) | (jb == cur) | (jb == cur - 1)
    score = jnp.where(jb <= cur, jnp.where(forced, FORCE_SCORE, imp), NEG)
    jbf = jb.astype(F32)
    selm = jnp.full((nb, KEY_TILE), NEG, F32)
    for _ in range(N_SEL):
        best = jnp.max(score, axis=0, keepdims=True)
        first = jnp.min(jnp.where(score == best, jbf, 1e9), axis=0, keepdims=True)
        pick = jbf == first
        selm = jnp.where(pick, 0.0, selm)
        score = jnp.where(pick, 3.0 * NEG, score)
    if r_l > KEY_TILE:
        selm = jnp.concatenate([selm] * (r_l // KEY_TILE), axis=1)
    for j in range(nb):
        selm_ref[j] = jnp.broadcast_to(selm[j:j + 1, :], (8, r_l))

    def tile_step(carry, k_tile, v_tile, add):
        m_run, l_run, acc = carry
        st = _dot(k_tile.astype(BF16), q_t) * SCALE + add
        m_new = jnp.maximum(m_run, jnp.max(st, axis=0, keepdims=True))
        alpha = jnp.exp(m_run - m_new)
        p = jnp.exp(st - m_new)
        l_new = alpha * l_run + jnp.sum(p, axis=0, keepdims=True)
        acc_new = alpha * acc + pv(v_tile.astype(BF16), p)
        return m_new, l_new, acc_new

    init = (jnp.full((1, r_l), NEG, F32), jnp.zeros((1, r_l), F32), jnp.zeros((HEAD_DIM, r_l), F32))

    def sel_mask(j):
        half = (KEY_TILE // SEL_BLOCK) * 4
        ma = jnp.broadcast_to(selm_ref[2 * j][None], (half, 8, r_l))
        mb = jnp.broadcast_to(selm_ref[2 * j + 1][None], (half, 8, r_l))
        return jnp.concatenate([ma, mb], axis=0).reshape(KEY_TILE, r_l)

    def rows(j):
        return pl.ds(pl.multiple_of(j * KEY_TILE, KEY_TILE), KEY_TILE)

    def sel_body(j, carry):
        add = img_ref[jnp.minimum(i_tile - j, 2)] + sel_mask(j)
        return tile_step(carry, ks_ref[rows(j), :], vs_ref[rows(j), :], add)

    def win_img(d):
        return img_ref[jnp.where(d == WINDOW // KEY_TILE, 3, jnp.minimum(d, 2))]

    if sample:
        carry = lax.fori_loop(0, i_tile, sel_body, init)
        _, l_s, acc_s = tile_step(carry, kst_ref[...], vst_ref[...], img_ref[0] + sel_mask(i_tile))
        carry = init
        n_win = WINDOW // KEY_TILE
        for jw in range(n_win):
            carry = tile_step(carry, kw_ref[jw * KEY_TILE:(jw + 1) * KEY_TILE, :],
                              vw_ref[jw * KEY_TILE:(jw + 1) * KEY_TILE, :], img_ref[(3, 2, 2, 1)[jw]])
        _, l_w, acc_w = tile_step(carry, kwt_ref[...], vwt_ref[...], img_ref[0])
    else:
        _, l_s, acc_s = lax.fori_loop(0, i_tile + 1, sel_body, init)

        def win_body(j, carry):
            return tile_step(carry, kw_ref[rows(j), :], vw_ref[rows(j), :], win_img(i_tile - j))

        _, l_w, acc_w = lax.fori_loop(jnp.maximum(i_tile - WINDOW // KEY_TILE, 0), i_tile + 1, win_body, init)

    gt = gt_ref[...]
    o_t = gt[0:1] * o_c + gt[1:2] * (acc_s / l_s) + gt[2:3] * (acc_w / l_w)
    o_r = o_t.T
    for n in range(HPG):
        o_ref[:, n * HEAD_DIM:(n + 1) * HEAD_DIM] = o_r[n * qbp:n * qbp + n_out].astype(o_ref.dtype)


def _attn_prompt(qkv_f, qkv_b, gates_t, kcmp, vcmp, img, ct, seq, off2):
    qbp = KEY_TILE
    r_l = HPG * qbp
    n_tiles = seq // qbp
    nc = seq // CMP_STRIDE
    nb = seq // SEL_BLOCK
    q_cols = N_HEADS * HEAD_DIM // HEAD_DIM

    def kv_spec(which):
        return pl.BlockSpec((seq, HEAD_DIM), functools.partial(lambda g, i, c: (0, c + g), c=q_cols + which * N_KV))

    cmp_spec = pl.BlockSpec((nc, HEAD_DIM), lambda g, i: (0, g))
    return pl.pallas_call(
        functools.partial(_attn_kernel, qbp=qbp, n_out=qbp, nc=nc, nb=nb, off2=off2, sample=False),
        grid=(N_KV, n_tiles),
        in_specs=[pl.BlockSpec((qbp, HPG * HEAD_DIM), lambda g, i: (i, g)),
                  pl.BlockSpec((None, None, 3, r_l), lambda g, i: (i, g, 0, 0)),
                  cmp_spec, cmp_spec,
                  kv_spec(2), kv_spec(3), kv_spec(4), kv_spec(5),
                  pl.BlockSpec((None, 4, KEY_TILE, r_l), lambda g, i: (g, 0, 0, 0)),
                  pl.BlockSpec((None, ct.shape[1], r_l), lambda g, i: (g, 0, 0))],
        out_specs=pl.BlockSpec((qbp, HPG * HEAD_DIM), lambda g, i: (i, g)),
        out_shape=jax.ShapeDtypeStruct((seq, N_HEADS * HEAD_DIM), BF16),
        scratch_shapes=[pltpu.VMEM((nb, 8, r_l), F32), pltpu.VMEM((nc, KEY_TILE), F32)],
        compiler_params=_params(("arbitrary", "arbitrary"), 48),
        name="nsa_prompt",
    )(qkv_f, gates_t, kcmp, vcmp, qkv_b, qkv_b, qkv_b, qkv_b, img, ct)


def _attn_sample(q_pad, gates_t, kcmp, vcmp, ks, vs, kw, vw, ks_t, vs_t, kw_t, vw_t, img, ct, qbp, n_out):
    n_seq, past = ks.shape[0], ks.shape[1]
    r_l = HPG * qbp
    nc = past // CMP_STRIDE
    nb = 8 * (-(-(past // SEL_BLOCK + 2) // 8))

    def seq_spec(rows_):
        return pl.BlockSpec((None, rows_, HEAD_DIM), lambda b, g: (b, 0, g))

    cmp_spec = pl.BlockSpec((nc, HEAD_DIM), lambda b, g: (b, g))
    return pl.pallas_call(
        functools.partial(_attn_kernel, qbp=qbp, n_out=qbp, nc=nc, nb=nb, off2=0, sample=True),
        grid=(n_seq, N_KV),
        in_specs=[pl.BlockSpec((qbp, HPG * HEAD_DIM), lambda b, g: (b, g)),
                  pl.BlockSpec((None, None, 3, r_l), lambda b, g: (b, g, 0, 0)),
                  cmp_spec, cmp_spec,
                  seq_spec(past), seq_spec(past), seq_spec(WINDOW), seq_spec(WINDOW),
                  seq_spec(KEY_TILE), seq_spec(KEY_TILE), seq_spec(KEY_TILE), seq_spec(KEY_TILE),
                  pl.BlockSpec((None, 4, KEY_TILE, r_l), lambda b, g: (g, 0, 0, 0)),
                  pl.BlockSpec((None, ct.shape[1], r_l), lambda b, g: (g, 0, 0))],
        out_specs=pl.BlockSpec((qbp, HPG * HEAD_DIM), lambda b, g: (b, g)),
        out_shape=jax.ShapeDtypeStruct((n_seq * qbp, N_HEADS * HEAD_DIM), BF16),
        scratch_shapes=[pltpu.VMEM((nb, 8, r_l), F32), pltpu.VMEM((nc, KEY_TILE), F32)],
        compiler_params=_params(("arbitrary", "arbitrary"), 48),
        name="nsa_sample",
    )(q_pad, gates_t, kcmp, vcmp, ks, vs, kw, vw, ks_t, vs_t, kw_t, vw_t, img, ct)


def _gates_to_lanes(g_sig, n_tiles, q_rows, qbp):
    g5 = g_sig.reshape(n_tiles, q_rows, 3, N_KV, HPG)
    if qbp > q_rows:
        g5 = jnp.pad(g5, ((0, 0), (0, qbp - q_rows), (0, 0), (0, 0), (0, 0)))
    return g5.transpose(0, 3, 2, 4, 1).reshape(n_tiles, N_KV, 3, HPG * qbp)


def _history_rows(state, dec_seq):
    n_seq, _, c = state.shape
    z = jnp.zeros((n_seq, dec_seq - 2, c), state.dtype)
    s1 = jnp.concatenate([state[:, 1:2], jnp.zeros((n_seq, 1, c), state.dtype), z], axis=1)
    s2 = jnp.concatenate([state, z], axis=1)
    return s1.reshape(n_seq * dec_seq, c), s2.reshape(n_seq * dec_seq, c)


def _cmp_weights(pe, w1, w2):
    half = CMP_STRIDE * HEAD_DIM
    w1cat = jnp.concatenate([w1[:half], w1[half:]], axis=1).astype(BF16)
    return pe.reshape(2, half), w1cat, w2.astype(BF16)


def kernel(x_prompt, x_sample, cache_k_cmp, cache_v_cmp, cache_k_sel, cache_v_sel, state_k_win, state_v_win,
           state_conv, state_ffn_conv, page_table, rel_bias, w_in, conv_w, conv_b, w_br_conv, w_br_nsa, w_out,
           pe_cmp_k, w_cmp_k1, w_cmp_k2, pe_cmp_v, w_cmp_v1, w_cmp_v2, ln1_g, ln1_b, w_ffn_in, ffn_conv_w,
           ffn_conv_b, w_ffn_out, ln2_g, ln2_b):
    seq = x_prompt.shape[1]
    n_seq, dec_seq = x_sample.shape[0], x_sample.shape[1]
    n_s = n_seq * dec_seq
    past = page_table.shape[1] * PAGE
    kvw = N_KV * HEAD_DIM
    qw = N_HEADS * HEAD_DIM
    tm = 1024

    x_all = jnp.concatenate([x_prompt[0], x_sample.reshape(n_s, D_MODEL)], axis=0)
    x_bf = x_all.astype(BF16)
    c_qkv = 3 * D_CONV
    c_gate = c_qkv + qw + 6 * kvw
    c_mix = c_gate + 3 * N_HEADS
    w_main = w_in[0, :, :c_gate].astype(BF16)
    w_gate = jnp.pad(w_in[0, :, c_gate:c_mix], ((0, 0), (0, 128 - 3 * N_HEADS))).astype(BF16)
    w_mix = w_in[0, :, c_mix:].astype(BF16)

    s1, s2 = _history_rows(state_conv[0], dec_seq)
    tn_c = 256
    z_conv, u_conv = _gconv(x_bf, w_main, (0, D_CONV // tn_c, 2 * D_CONV // tn_c), conv_w[0], conv_b, s1, s2,
                            seq, tm, tn_c, 48, "proj_conv")
    qkv_f, qkv_b = _proj(x_bf, w_main, c_qkv, qw + 6 * kvw, tm, 512)
    g_sig = _gate_proj(x_bf, w_gate, tm)[:, :3 * N_HEADS]

    pe_k, w1_k, w2_k = _cmp_weights(pe_cmp_k[0], w_cmp_k1[0], w_cmp_k2[0])
    pe_v, w1_v, w2_v = _cmp_weights(pe_cmp_v[0], w_cmp_v1[0], w_cmp_v2[0])
    chunk_w = CMP_STRIDE * kvw
    kc_p = qkv_b[:seq, qw:qw + kvw].reshape(seq // CMP_STRIDE, chunk_w)
    vc_p = qkv_b[:seq, qw + kvw:qw + 2 * kvw].reshape(seq // CMP_STRIDE, chunk_w)
    kcmp_p = _compress(kc_p, pe_k, w1_k, w2_k, seq // CMP_STRIDE, "compress_k_prompt")
    vcmp_p = _compress(vc_p, pe_v, w1_v, w2_v, seq // CMP_STRIDE, "compress_v_prompt")

    caches = [c[0].reshape(c.shape[1], PAGE, kvw) for c in (cache_k_cmp, cache_v_cmp, cache_k_sel, cache_v_sel)]
    kc_past, vc_past, ks_past, vs_past = _gather_pages(page_table, caches)
    chunks_s = past // CMP_STRIDE
    kcmp_s = _compress(kc_past.reshape(n_seq * chunks_s, chunk_w), pe_k, w1_k, w2_k, 2 * chunks_s, "compress_k_sample")
    vcmp_s = _compress(vc_past.reshape(n_seq * chunks_s, chunk_w), pe_v, w1_v, w2_v, 2 * chunks_s, "compress_v_sample")

    off2_p = (seq - KEY_TILE) // CMP_STRIDE
    nct_p = KEY_TILE * (-(-(off2_p + seq // CMP_STRIDE) // KEY_TILE))
    img_p, ct_p = _bias_images(rel_bias, KEY_TILE, nct_p, off2_p, "bias_prompt")
    gates_p = _gates_to_lanes(g_sig[:seq], seq // KEY_TILE, KEY_TILE, KEY_TILE)
    o_p = _attn_prompt(qkv_f, qkv_b, gates_p, kcmp_p, vcmp_p, img_p, ct_p, seq, off2_p)

    qbp_s = 32
    img_s, ct_s = _bias_images(rel_bias, qbp_s, chunks_s, chunks_s, "bias_sample")
    gates_s = _gates_to_lanes(g_sig[seq:], n_seq, dec_seq, qbp_s)
    new_f = qkv_f[seq:].reshape(n_seq, dec_seq, qw + 6 * kvw)
    q_pad = jnp.pad(new_f[:, :, :qw], ((0, 0), (0, qbp_s - dec_seq), (0, 0))).reshape(n_seq * qbp_s, qw)

    def tail(col):
        return jnp.pad(new_f[:, :, col:col + kvw], ((0, 0), (0, KEY_TILE - dec_seq), (0, 0)))

    kw_state = state_k_win[0].reshape(n_seq, -1, kvw)
    vw_state = state_v_win[0].reshape(n_seq, -1, kvw)
    o_s = _attn_sample(q_pad, gates_s, kcmp_s, vcmp_s, ks_past, vs_past, kw_state, vw_state,
                       tail(qw + 2 * kvw), tail(qw + 3 * kvw), tail(qw + 4 * kvw), tail(qw + 5 * kvw),
                       img_s, ct_s, qbp_s, dec_seq)
    o_s = o_s.reshape(n_seq, qbp_s, qw)[:, :dec_seq].reshape(n_s, qw)
    o_all = jnp.concatenate([o_p, o_s], axis=0)

    mix = _merge(x_bf, z_conv, o_all, w_br_conv[0].astype(BF16), w_br_nsa[0].astype(BF16), w_mix, 512, 256)
    r1 = _resid_mm(mix, w_out[0].astype(BF16), x_all, tm, 512, 48, "out_proj")
    h_f, h_b = _layer_norm(r1, ln1_g, ln1_b, 256, "ln1")
    f1, f2 = _history_rows(state_ffn_conv[0], dec_seq)
    tn_f = 256
    act, gp = _gconv(h_b, w_ffn_in[0].astype(BF16), (D_FF // tn_f, 0), ffn_conv_w[0], ffn_conv_b, f1, f2,
                     seq, tm, tn_f, 48, "ffn_in")
    r2 = _resid_mm(act, w_ffn_out[0].astype(BF16), h_f, 512, 256, 56, "ffn_out")
    y, _ = _layer_norm(r2, ln2_g, ln2_b, 256, "ln2")

    def rows_p(col):
        return qkv_f[:seq, col:col + kvw].reshape(1, 1, seq, N_KV, HEAD_DIM)

    def rows_s(col):
        return new_f[:, :, col:col + kvw].reshape(1, n_seq, dec_seq, N_KV, HEAD_DIM)

    def win_p(col):
        return qkv_f[seq - WINDOW:seq, col:col + kvw].reshape(1, 1, WINDOW, N_KV, HEAD_DIM)

    def win_s(state, col):
        new = new_f[:, :, col:col + kvw].reshape(n_seq, dec_seq, N_KV, HEAD_DIM)
        return jnp.concatenate([state[0][:, dec_seq:], new], axis=1)[None]

    def last2(u):
        c = u.shape[1]
        return (u[seq - 2:seq].reshape(1, 1, 2, c),
                u[seq:].reshape(n_seq, dec_seq, c)[:, dec_seq - 2:].reshape(1, n_seq, 2, c))

    conv_p, conv_s = last2(u_conv)
    ffn_p, ffn_s = last2(gp)
    c = qw
    return (y[:seq].reshape(1, seq, D_MODEL), y[seq:].reshape(n_seq, dec_seq, D_MODEL),
            rows_p(c), rows_s(c), rows_p(c + kvw), rows_s(c + kvw),
            rows_p(c + 2 * kvw), rows_s(c + 2 * kvw), rows_p(c + 3 * kvw), rows_s(c + 3 * kvw),
            win_p(c + 4 * kvw), win_s(state_k_win, c + 4 * kvw), win_p(c + 5 * kvw), win_s(state_v_win, c + 5 * kvw),
            conv_p, conv_s, ffn_p, ffn_s)
```

```python
import functools
import math

import jax
import jax.numpy as jnp
from jax import lax
from jax.experimental import pallas as pl
from jax.experimental.pallas import tpu as pltpu

F32 = jnp.float32
BF16 = jnp.bfloat16

D_MODEL = 4096
D_CONV = 2048
N_HEADS = 16
HEAD_DIM = 128
N_KV = 4
HPG = 4
CMP_STRIDE = 16
CMP_HID = 256
SEL_BLOCK = 64
N_SEL = 16
WINDOW = 512
FORCE_SCORE = 1e4
REL_BUCKETS = 32
REL_EXACT = 16
REL_MAX_DIST = 128
D_FF = 11008
PAGE = 128
ALPHA = 2.0 ** 0.25
LN_EPS = 1e-5
LOG2E = 1.0 / math.log(2.0)
Q_SCALE = HEAD_DIM ** -0.5 * LOG2E

KEY_TILE = 128
SEL_GROUP = 4
PAGE_ROWS = PAGE * N_KV
NEG = -1e30
MIB = 1024 * 1024

IMG_DIAG, IMG_PREV, IMG_ZERO, IMG_EDGE, IMG_NEG = 0, 1, 2, 3, 4
N_IMG = 5


def _params(sem, vmem_mib):
    return pltpu.CompilerParams(dimension_semantics=sem, vmem_limit_bytes=vmem_mib * MIB)


def _dot(a, b):
    return jnp.dot(a, b, preferred_element_type=F32)


def _proj_kernel(x_ref, w_ref, of_ref, ob_ref):
    acc = _dot(x_ref[...], w_ref[...])
    of_ref[...] = acc
    ob_ref[...] = acc.astype(BF16)


def _proj(x, w, col0, ncols, tm, tn):
    m_rows, k = x.shape
    c0 = col0 // tn
    return pl.pallas_call(
        _proj_kernel,
        grid=(m_rows // tm, ncols // tn),
        in_specs=[pl.BlockSpec((tm, k), lambda m, n: (m, 0)),
                  pl.BlockSpec((k, tn), lambda m, n: (0, c0 + n))],
        out_specs=[pl.BlockSpec((tm, tn), lambda m, n: (m, n)),
                   pl.BlockSpec((tm, tn), lambda m, n: (m, n))],
        out_shape=[jax.ShapeDtypeStruct((m_rows, ncols), F32),
                   jax.ShapeDtypeStruct((m_rows, ncols), BF16)],
        compiler_params=_params(("arbitrary", "arbitrary"), 48),
        name="proj_qkv",
    )(x, w)


def _gate_kernel(x_ref, w_ref, o_ref):
    o_ref[...] = jax.nn.sigmoid(_dot(x_ref[...], w_ref[...]))


def _gate_proj(x, w, tm):
    m_rows, k = x.shape
    n = w.shape[1]
    return pl.pallas_call(
        _gate_kernel,
        grid=(m_rows // tm,),
        in_specs=[pl.BlockSpec((tm, k), lambda m: (m, 0)),
                  pl.BlockSpec((k, n), lambda m: (0, 0))],
        out_specs=pl.BlockSpec((tm, n), lambda m: (m, 0)),
        out_shape=jax.ShapeDtypeStruct((m_rows, n), F32),
        compiler_params=_params(("arbitrary",), 40),
        name="proj_gates",
    )(x, w)


def _gconv_kernel(*refs, three, n_prompt_tiles):
    if three:
        x_ref, w0_ref, w1_ref, w2_ref, cw_ref, cb_ref, s1_ref, s2_ref, z_ref, u_ref, carry_ref = refs
    else:
        x_ref, w0_ref, w1_ref, cw_ref, cb_ref, s1_ref, s2_ref, z_ref, u_ref, carry_ref = refs
    m = pl.program_id(0)
    n = pl.program_id(1)
    x = x_ref[...]
    p0 = _dot(x, w0_ref[...])
    u = _dot(x, w1_ref[...])
    if three:
        u = u * _dot(x, w2_ref[...])
    u_ref[...] = u
    tm = u.shape[0]
    row = lax.broadcasted_iota(jnp.int32, u.shape, 0)
    sh1 = pltpu.roll(u, 1, 0)
    sh2 = pltpu.roll(u, 2, 0)
    cw = cw_ref[...]
    cb = cb_ref[...]

    def finish(prev1, prev2):
        conv = cb + prev2 * cw[0:1] + prev1 * cw[1:2] + u * cw[2:3]
        if three:
            z = p0 * conv
        else:
            z = jax.nn.silu(conv) * p0
        z_ref[...] = z.astype(z_ref.dtype)

    @pl.when(m < n_prompt_tiles)
    def _():
        c = carry_ref[n]
        c = jnp.where(m == 0, jnp.zeros_like(c), c)
        c0 = c[6:7]
        c1 = c[7:8]
        prev1 = jnp.where(row == 0, c1, sh1)
        prev2 = jnp.where(row == 0, c0, jnp.where(row == 1, c1, sh2))
        carry_ref[n] = u[tm - 8:tm]
        finish(prev1, prev2)

    @pl.when(m >= n_prompt_tiles)
    def _():
        r8 = row & 7
        prev1 = jnp.where(r8 == 0, s1_ref[...], sh1)
        prev2 = jnp.where(r8 < 2, s2_ref[...], sh2)
        finish(prev1, prev2)


def _gconv(x, w, col_blocks, cw, cb, s1, s2, n_prompt_rows, tm, tn, vmem_mib, name):
    m_rows, k = x.shape
    ncols = cw.shape[1]
    three = len(col_blocks) == 3
    assert m_rows - n_prompt_rows == tm and n_prompt_rows % tm == 0
    nt = ncols // tn
    w_specs = [pl.BlockSpec((k, tn), functools.partial(lambda m, n, c: (0, c + n), c=c)) for c in col_blocks]
    col_spec = pl.BlockSpec((tm, tn), lambda m, n: (m, n))
    return pl.pallas_call(
        functools.partial(_gconv_kernel, three=three, n_prompt_tiles=n_prompt_rows // tm),
        grid=(m_rows // tm, nt),
        in_specs=[pl.BlockSpec((tm, k), lambda m, n: (m, 0))] + w_specs + [
            pl.BlockSpec((3, tn), lambda m, n: (0, n)),
            pl.BlockSpec((1, tn), lambda m, n: (0, n)),
            pl.BlockSpec((tm, tn), lambda m, n: (0, n)),
            pl.BlockSpec((tm, tn), lambda m, n: (0, n))],
        out_specs=[col_spec, col_spec],
        out_shape=[jax.ShapeDtypeStruct((m_rows, ncols), BF16),
                   jax.ShapeDtypeStruct((m_rows, ncols), F32)],
        scratch_shapes=[pltpu.VMEM((nt, 8, tn), F32)],
        compiler_params=_params(("arbitrary", "arbitrary"), vmem_mib),
        name=name,
    )(x, *([w] * len(col_blocks)), cw, cb, s1, s2)


def _merge_kernel(x_ref, z_ref, o_ref, wc_ref, wn_ref, wga_ref, wgb_ref, out_ref):
    x = x_ref[...]
    a = _dot(z_ref[...], wc_ref[...])
    b = _dot(o_ref[...], wn_ref[...])
    ga = jax.nn.sigmoid(_dot(x, wga_ref[...]))
    gb = jax.nn.sigmoid(_dot(x, wgb_ref[...]))
    out_ref[...] = (ga * a + gb * b).astype(BF16)


def _merge(x, z, o, wc, wn, wg, tm, tn):
    m_rows, k = x.shape
    kc = z.shape[1]
    nt = D_MODEL // tn
    return pl.pallas_call(
        _merge_kernel,
        grid=(m_rows // tm, nt),
        in_specs=[pl.BlockSpec((tm, k), lambda m, n: (m, 0)),
                  pl.BlockSpec((tm, kc), lambda m, n: (m, 0)),
                  pl.BlockSpec((tm, kc), lambda m, n: (m, 0)),
                  pl.BlockSpec((kc, tn), lambda m, n: (0, n)),
                  pl.BlockSpec((kc, tn), lambda m, n: (0, n)),
                  pl.BlockSpec((k, tn), lambda m, n: (0, n)),
                  pl.BlockSpec((k, tn), lambda m, n: (0, nt + n))],
        out_specs=pl.BlockSpec((tm, tn), lambda m, n: (m, n)),
        out_shape=jax.ShapeDtypeStruct((m_rows, D_MODEL), BF16),
        compiler_params=_params(("arbitrary", "arbitrary"), 56),
        name="merge_branches",
    )(x, z, o, wc, wn, wg, wg)


def _resid_kernel(l_ref, w_ref, r_ref, o_ref):
    o_ref[...] = ALPHA * r_ref[...] + _dot(l_ref[...], w_ref[...])


def _resid_mm(lhs, w, res, tm, tn, vmem_mib, name):
    m_rows, k = lhs.shape
    n_cols = w.shape[1]
    return pl.pallas_call(
        _resid_kernel,
        grid=(m_rows // tm, n_cols // tn),
        in_specs=[pl.BlockSpec((tm, k), lambda m, n: (m, 0)),
                  pl.BlockSpec((k, tn), lambda m, n: (0, n)),
                  pl.BlockSpec((tm, tn), lambda m, n: (m, n))],
        out_specs=pl.BlockSpec((tm, tn), lambda m, n: (m, n)),
        out_shape=jax.ShapeDtypeStruct((m_rows, n_cols), F32),
        compiler_params=_params(("arbitrary", "arbitrary"), vmem_mib),
        name=name,
    )(lhs, w, res)


def _ln_kernel(x_ref, g_ref, b_ref, of_ref, ob_ref):
    x = x_ref[...]
    mu = jnp.mean(x, axis=-1, keepdims=True)
    xc = x - mu
    var = jnp.mean(xc * xc, axis=-1, keepdims=True)
    y = xc * lax.rsqrt(var + LN_EPS) * g_ref[...] + b_ref[...]
    of_ref[...] = y
    ob_ref[...] = y.astype(BF16)


def _layer_norm(x, g, b, tr, name):
    m_rows, d = x.shape
    row_spec = pl.BlockSpec((tr, d), lambda m: (m, 0))
    vec_spec = pl.BlockSpec((1, d), lambda m: (0, 0))
    return pl.pallas_call(
        _ln_kernel,
        grid=(m_rows // tr,),
        in_specs=[row_spec, vec_spec, vec_spec],
        out_specs=[row_spec, row_spec],
        out_shape=[jax.ShapeDtypeStruct((m_rows, d), F32), jax.ShapeDtypeStruct((m_rows, d), BF16)],
        compiler_params=_params(("arbitrary",), 48),
        name=name,
    )(x, g, b)


def _page_copies(pt_ref, seq, cache_ref, buf, slot, sem):
    return [pltpu.make_async_copy(
        cache_ref.at[pl.ds(pl.multiple_of(pt_ref[seq, j] * PAGE_ROWS, PAGE_ROWS), PAGE_ROWS)],
        buf.at[slot, pl.ds(j * PAGE_ROWS, PAGE_ROWS)], sem)
        for j in range(pt_ref.shape[1])]


def _prefetch_pages(pt_ref, b, n_seq, streams):
    slot = b % 2

    @pl.when(b == 0)
    def _():
        for cache_ref, buf, sem in streams:
            for cp in _page_copies(pt_ref, 0, cache_ref, buf, 0, sem.at[0]):
                cp.start()

    @pl.when(b + 1 < n_seq)
    def _():
        for cache_ref, buf, sem in streams:
            for cp in _page_copies(pt_ref, b + 1, cache_ref, buf, 1 - slot, sem.at[1 - slot]):
                cp.start()

    for cache_ref, buf, sem in streams:
        for cp in _page_copies(pt_ref, b, cache_ref, buf, slot, sem.at[slot]):
            cp.wait()
    return slot


def _compress_rows(xg, pe, w1, w2):
    rows = xg.shape[0]
    half = CMP_STRIDE * HEAD_DIM
    pa = _dot(jnp.broadcast_to(pe[0:1], (8, half)).astype(BF16), w1[:, :CMP_HID])[0:1]
    pb = _dot(jnp.broadcast_to(pe[1:2], (8, half)).astype(BF16), w1[:, CMP_HID:])[0:1]
    fs = _dot(xg, w1)
    first = fs[:, :CMP_HID] + pa
    second = fs[:, CMP_HID:] + pb
    hid = jax.nn.gelu(first + pltpu.roll(second, rows - 1, 0))
    return _dot(hid.astype(BF16), w2)


def _compress_kernel(x_ref, pe_ref, w1_ref, w2_ref, o_ref):
    row_w = N_KV * HEAD_DIM
    for g in range(N_KV):
        xg = jnp.concatenate(
            [x_ref[:, r * row_w + g * HEAD_DIM: r * row_w + (g + 1) * HEAD_DIM] for r in range(CMP_STRIDE)],
            axis=1).astype(BF16)
        o_ref[:, g * HEAD_DIM:(g + 1) * HEAD_DIM] = _compress_rows(
            xg, pe_ref[...], w1_ref[...], w2_ref[...]).astype(BF16)


def _compress(x2, pe2, w1cat, w2, name):
    rows, width = x2.shape
    return pl.pallas_call(
        _compress_kernel,
        grid=(1,),
        in_specs=[pl.BlockSpec((rows, width), lambda t: (0, 0)),
                  pl.BlockSpec(pe2.shape, lambda t: (0, 0)),
                  pl.BlockSpec(w1cat.shape, lambda t: (0, 0)),
                  pl.BlockSpec(w2.shape, lambda t: (0, 0))],
        out_specs=pl.BlockSpec((rows, N_KV * HEAD_DIM), lambda t: (0, 0)),
        out_shape=jax.ShapeDtypeStruct((rows, N_KV * HEAD_DIM), BF16),
        compiler_params=_params(("arbitrary",), 48),
        name=name,
    )(x2, pe2, w1cat, w2)


def _compress_paged_kernel(pt_ref, cache_ref, pe_ref, w1_ref, w2_ref, o_ref, buf, sem, *, chunks):
    b = pl.program_id(0)
    slot = _prefetch_pages(pt_ref, b, pl.num_programs(0), [(cache_ref, buf, sem)])
    stride = CMP_STRIDE * N_KV
    xs = []
    for g in range(N_KV):
        xs.append(jnp.concatenate(
            [buf[slot, pl.ds(r * N_KV + g, chunks, stride=stride), :] for r in range(CMP_STRIDE)], axis=1))
    xg = jnp.concatenate(xs, axis=0).astype(BF16)
    out = _compress_rows(xg, pe_ref[...], w1_ref[...], w2_ref[...])
    for g in range(N_KV):
        o_ref[:, g * HEAD_DIM:(g + 1) * HEAD_DIM] = out[g * chunks:(g + 1) * chunks].astype(BF16)


def _compress_paged(page_table, cache2, pe2, w1cat, w2, name):
    n_seq, n_pages = page_table.shape
    chunks = n_pages * PAGE // CMP_STRIDE
    return pl.pallas_call(
        functools.partial(_compress_paged_kernel, chunks=chunks),
        grid_spec=pltpu.PrefetchScalarGridSpec(
            num_scalar_prefetch=1, grid=(n_seq,),
            in_specs=[pl.BlockSpec(memory_space=pl.ANY),
                      pl.BlockSpec(pe2.shape, lambda b, pt: (0, 0)),
                      pl.BlockSpec(w1cat.shape, lambda b, pt: (0, 0)),
                      pl.BlockSpec(w2.shape, lambda b, pt: (0, 0))],
            out_specs=pl.BlockSpec((chunks, N_KV * HEAD_DIM), lambda b, pt: (b, 0)),
            scratch_shapes=[pltpu.VMEM((2, n_pages * PAGE_ROWS, HEAD_DIM), F32),
                            pltpu.SemaphoreType.DMA((2,))]),
        out_shape=jax.ShapeDtypeStruct((n_seq * chunks, N_KV * HEAD_DIM), BF16),
        compiler_params=_params(("arbitrary",), 40),
        name=name,
    )(page_table, cache2, pe2, w1cat, w2)


def _bias_kernel(rb_ref, img_ref, ct_ref, *, qbp, off2):
    g = pl.program_id(0)
    r_l = HPG * qbp
    lane = lax.broadcasted_iota(jnp.int32, (1, r_l), 1)
    n_l = lane // qbp
    tbl = []
    for k in range(REL_BUCKETS):
        row = jnp.zeros((1, r_l), F32)
        for n in range(HPG):
            row = jnp.where(n_l == n, rb_ref[k, g * HPG + n], row)
        tbl.append(row)

    def bias_of(dist):
        nn = jnp.maximum(dist, 0)
        nf = jnp.maximum(nn, 1).astype(F32)
        large = REL_EXACT + (jnp.log(nf / REL_EXACT) / math.log(REL_MAX_DIST / REL_EXACT)
                             * (REL_BUCKETS - REL_EXACT)).astype(jnp.int32)
        bucket = jnp.where(nn < REL_EXACT, nn, jnp.minimum(large, REL_BUCKETS - 1))
        val = jnp.zeros(dist.shape, F32)
        for k in range(REL_BUCKETS):
            val = jnp.where(bucket == k, tbl[k], val)
        return val

    shape = (KEY_TILE, r_l)
    kj = lax.broadcasted_iota(jnp.int32, shape, 0)
    qi = lax.broadcasted_iota(jnp.int32, shape, 1) % qbp
    d0 = qi - kj
    far = tbl[REL_BUCKETS - 1]
    img_ref[IMG_DIAG] = jnp.where(d0 >= 0, (bias_of(d0) - far) * LOG2E, NEG)
    img_ref[IMG_PREV] = (bias_of(d0 + KEY_TILE) - far) * LOG2E
    img_ref[IMG_ZERO] = jnp.zeros(shape, F32)
    img_ref[IMG_EDGE] = jnp.where(kj > qi, 0.0, NEG)
    img_ref[IMG_NEG] = jnp.full(shape, NEG, F32)
    for r0 in range(0, ct_ref.shape[0], KEY_TILE):
        dist = qi + CMP_STRIDE * (off2 - r0 - kj) - (2 * CMP_STRIDE - 1)
        ct_ref[r0:r0 + KEY_TILE, :] = jnp.where(dist >= 0, bias_of(dist) * LOG2E, NEG)


def _bias_images(rel_bias, qbp, nct, off2, name):
    r_l = HPG * qbp
    return pl.pallas_call(
        functools.partial(_bias_kernel, qbp=qbp, off2=off2),
        grid=(N_KV,),
        in_specs=[pl.BlockSpec(memory_space=pltpu.SMEM)],
        out_specs=[pl.BlockSpec((None, N_IMG, KEY_TILE, r_l), lambda g: (g, 0, 0, 0)),
                   pl.BlockSpec((None, nct, r_l), lambda g: (g, 0, 0))],
        out_shape=[jax.ShapeDtypeStruct((N_KV, N_IMG, KEY_TILE, r_l), F32),
                   jax.ShapeDtypeStruct((N_KV, nct, r_l), F32)],
        compiler_params=_params(("arbitrary",), 32),
        name=name,
    )(rel_bias)


def _pv(v, p):
    return lax.dot_general(v, p.astype(BF16), (((0,), (0,)), ((), ())), preferred_element_type=F32)


def _softmax_block(carry, q_t, segs):
    m_run, l_run, acc = carry
    scores = []
    for k, _, add, mask in segs:
        st = _dot(k, q_t)
        if add is not None:
            st = st + add
        if mask is not None:
            nblk, r_l = mask.shape[0], st.shape[1]
            st = (st.reshape(nblk, SEL_BLOCK // 8, 8, r_l) + mask[:, None]).reshape(nblk * SEL_BLOCK, r_l)
        scores.append(st)
    m_new = m_run
    for st in scores:
        m_new = jnp.maximum(m_new, jnp.max(st, axis=0, keepdims=True))
    alpha = jnp.exp2(m_run - m_new)
    l_new = alpha * l_run
    acc_new = alpha * acc
    for st, seg in zip(scores, segs):
        p = jnp.exp2(st - m_new)
        l_new = l_new + jnp.sum(p, axis=0, keepdims=True)
        acc_new = acc_new + _pv(seg[1], p)
    return m_new, l_new, acc_new


def _attn_kernel(*refs, qbp, nc, nb, off2, sample):
    if sample:
        (pt_ref, q_ref, gt_ref, kc_ref, vc_ref, kw_ref, vw_ref, kst_ref, vst_ref, kwt_ref, vwt_ref,
         img_ref, ct_ref, ks_hbm, vs_hbm, o_ref, selm_ref, ps_ref, kbuf, vbuf, sem) = refs
        past = kbuf.shape[1] // N_KV
        i_tile = past // KEY_TILE
        grp = pl.program_id(1)
    else:
        (q_ref, gt_ref, kc_ref, vc_ref, ks_ref, vs_ref, kw_ref, vw_ref,
         img_ref, ct_ref, o_ref, selm_ref, ps_ref) = refs
        i_tile = pl.program_id(1)
    r_l = HPG * qbp
    t0 = i_tile * KEY_TILE

    q = q_ref[...] * Q_SCALE
    qg = jnp.concatenate([q[:, n * HEAD_DIM:(n + 1) * HEAD_DIM] for n in range(HPG)], axis=0)
    q_t = qg.T.astype(BF16)

    if sample:
        cbias = ct_ref[...]
    else:
        start = pl.multiple_of(off2 - i_tile * (KEY_TILE // CMP_STRIDE), 8)
        cbias = ct_ref[pl.ds(start, nc), :]
    s = _dot(kc_ref[...], q_t) + cbias
    valid = cbias > 0.5 * NEG
    mx = jnp.max(s, axis=0, keepdims=True)
    mx = jnp.where(mx > 0.5 * NEG, mx, 0.0)
    e = jnp.where(valid, jnp.exp2(s - mx), 0.0)
    p_c = e / jnp.maximum(jnp.sum(e, axis=0, keepdims=True), 1e-30)
    o_c = _pv(vc_ref[...], p_c)

    if qbp == KEY_TILE:
        psum = p_c[:, 0:qbp]
        for n in range(1, HPG):
            psum = psum + p_c[:, n * qbp:(n + 1) * qbp]
    else:
        psum = p_c
        for n in range(1, HPG):
            psum = psum + pltpu.roll(p_c, n * qbp, 1)
    ps_ref[...] = psum
    per_sel = SEL_BLOCK // CMP_STRIDE
    nbi = nc // per_sel
    imp = ps_ref[pl.ds(0, nbi, stride=per_sel), :]
    for c in range(1, per_sel):
        imp = imp + ps_ref[pl.ds(c, nbi, stride=per_sel), :]
    if nb > nbi:
        imp = jnp.concatenate([imp, jnp.zeros((nb - nbi, KEY_TILE), F32)], axis=0)
    jb = lax.broadcasted_iota(jnp.int32, (nb, KEY_TILE), 0)
    tq = t0 + lax.broadcasted_iota(jnp.int32, (nb, KEY_TILE), 1) % qbp
    cur = tq // SEL_BLOCK
    forced = (jb == 0) | (jb == cur) | (jb == cur - 1)
    score = jnp.where(jb <= cur, jnp.where(forced, FORCE_SCORE, imp), NEG)
    jbf = jb.astype(F32)
    selm = jnp.full((nb, KEY_TILE), NEG, F32)
    for _ in range(N_SEL):
        best = jnp.max(score, axis=0, keepdims=True)
        first = jnp.min(jnp.where(score == best, jbf, 1e9), axis=0, keepdims=True)
        pick = jbf == first
        selm = jnp.where(pick, 0.0, selm)
        score = jnp.where(pick, 3.0 * NEG, score)
    if r_l > KEY_TILE:
        selm = jnp.concatenate([selm] * (r_l // KEY_TILE), axis=1)
    for j in range(nb):
        selm_ref[j] = jnp.broadcast_to(selm[j:j + 1, :], (8, r_l))

    init = (jnp.full((1, r_l), NEG, F32), jnp.zeros((1, r_l), F32), jnp.zeros((HEAD_DIM, r_l), F32))
    blocks_per_tile = KEY_TILE // SEL_BLOCK
    n_win = WINDOW // KEY_TILE

    if sample:
        slot = _prefetch_on_first_group(pt_ref, ks_hbm, vs_hbm, kbuf, vbuf, sem)
        ks = kbuf[slot, pl.ds(grp, past, stride=N_KV), :].astype(BF16)
        vs = vbuf[slot, pl.ds(grp, past, stride=N_KV), :].astype(BF16)
        far = past - KEY_TILE
        nfb = far // SEL_BLOCK
        segs = [(ks[:far], vs[:far], None, selm_ref[0:nfb]),
                (ks[far:], vs[far:], img_ref[IMG_PREV], selm_ref[nfb:nfb + blocks_per_tile]),
                (kst_ref[...].astype(BF16), vst_ref[...].astype(BF16), img_ref[IMG_DIAG],
                 selm_ref[nfb + blocks_per_tile:nfb + 2 * blocks_per_tile])]
        _, l_s, acc_s = _softmax_block(init, q_t, segs)
        kw = kw_ref[pl.ds(grp, WINDOW, stride=N_KV), :].astype(BF16)
        vw = vw_ref[pl.ds(grp, WINDOW, stride=N_KV), :].astype(BF16)
        a, b = KEY_TILE, WINDOW - KEY_TILE
        segs = [(kw[:a], vw[:a], img_ref[IMG_EDGE], None),
                (kw[a:b], vw[a:b], None, None),
                (kw[b:], vw[b:], img_ref[IMG_PREV], None),
                (kwt_ref[...].astype(BF16), vwt_ref[...].astype(BF16), img_ref[IMG_DIAG], None)]
        _, l_w, acc_w = _softmax_block(init, q_t, segs)
    else:
        def img_of(d, edge):
            idx = jnp.minimum(d, IMG_ZERO)
            if edge:
                idx = jnp.where(d == n_win, IMG_EDGE, idx)
            return img_ref[jnp.where(d < 0, IMG_NEG, idx)]

        def sel_body(it, carry):
            base = it * SEL_GROUP
            rows = pl.ds(pl.multiple_of(base * KEY_TILE, SEL_GROUP * KEY_TILE), SEL_GROUP * KEY_TILE)
            add = jnp.concatenate([img_of(i_tile - (base + t), False) for t in range(SEL_GROUP)], axis=0)
            mask = selm_ref[pl.ds(base * blocks_per_tile, SEL_GROUP * blocks_per_tile)]
            return _softmax_block(carry, q_t, [(ks_ref[rows, :], vs_ref[rows, :], add, mask)])

        _, l_s, acc_s = lax.fori_loop(0, (i_tile + SEL_GROUP) // SEL_GROUP, sel_body, init)

        first_tile = jnp.maximum(i_tile - n_win, 0)
        rows = pl.ds(pl.multiple_of(first_tile * KEY_TILE, KEY_TILE), (n_win + 1) * KEY_TILE)
        add = jnp.concatenate([img_of(i_tile - (first_tile + t), True) for t in range(n_win + 1)], axis=0)
        _, l_w, acc_w = _softmax_block(init, q_t, [(kw_ref[rows, :], vw_ref[rows, :], add, None)])

    gt = gt_ref[...]
    o_t = gt[0:1] * o_c + gt[1:2] * (acc_s / l_s) + gt[2:3] * (acc_w / l_w)
    o_r = o_t.T
    for n in range(HPG):
        o_ref[:, n * HEAD_DIM:(n + 1) * HEAD_DIM] = o_r[n * qbp:(n + 1) * qbp].astype(o_ref.dtype)


def _prefetch_on_first_group(pt_ref, ks_hbm, vs_hbm, kbuf, vbuf, sem):
    b = pl.program_id(0)

    @pl.when(pl.program_id(1) == 0)
    def _():
        _prefetch_pages(pt_ref, b, pl.num_programs(0), [(ks_hbm, kbuf, sem.at[0]), (vs_hbm, vbuf, sem.at[1])])

    return b % 2


def _attn_prompt(qkv_f, qkv_b, gates_t, kcmp, vcmp, img, ct, seq, off2):
    qbp = KEY_TILE
    r_l = HPG * qbp
    n_tiles = seq // qbp
    nc = seq // CMP_STRIDE
    nb = seq // SEL_BLOCK
    q_cols = N_HEADS * HEAD_DIM // HEAD_DIM

    def kv_spec(which):
        return pl.BlockSpec((seq, HEAD_DIM), functools.partial(lambda g, i, c: (0, c + g), c=q_cols + which * N_KV))

    cmp_spec = pl.BlockSpec((nc, HEAD_DIM), lambda g, i: (0, g))
    return pl.pallas_call(
        functools.partial(_attn_kernel, qbp=qbp, nc=nc, nb=nb, off2=off2, sample=False),
        grid=(N_KV, n_tiles),
        in_specs=[pl.BlockSpec((qbp, HPG * HEAD_DIM), lambda g, i: (i, g)),
                  pl.BlockSpec((None, None, 3, r_l), lambda g, i: (i, g, 0, 0)),
                  cmp_spec, cmp_spec,
                  kv_spec(2), kv_spec(3), kv_spec(4), kv_spec(5),
                  pl.BlockSpec((None, N_IMG, KEY_TILE, r_l), lambda g, i: (g, 0, 0, 0)),
                  pl.BlockSpec((None, ct.shape[1], r_l), lambda g, i: (g, 0, 0))],
        out_specs=pl.BlockSpec((qbp, HPG * HEAD_DIM), lambda g, i: (i, g)),
        out_shape=jax.ShapeDtypeStruct((seq, N_HEADS * HEAD_DIM), BF16),
        scratch_shapes=[pltpu.VMEM((nb, 8, r_l), F32), pltpu.VMEM((nc, KEY_TILE), F32)],
        compiler_params=_params(("arbitrary", "arbitrary"), 48),
        name="nsa_prompt",
    )(qkv_f, gates_t, kcmp, vcmp, qkv_b, qkv_b, qkv_b, qkv_b, img, ct)


def _attn_sample(page_table, q_pad, gates_t, kcmp, vcmp, kw, vw, ks_t, vs_t, kw_t, vw_t, img, ct,
                 ks_cache, vs_cache, qbp):
    n_seq, n_pages = page_table.shape
    past = n_pages * PAGE
    r_l = HPG * qbp
    nc = past // CMP_STRIDE
    nb = 8 * (-(-(past // SEL_BLOCK + 2) // 8))

    def tail_spec():
        return pl.BlockSpec((None, KEY_TILE, HEAD_DIM), lambda b, g, pt: (b, 0, g))

    win_spec = pl.BlockSpec((WINDOW * N_KV, HEAD_DIM), lambda b, g, pt: (b, 0))
    cmp_spec = pl.BlockSpec((nc, HEAD_DIM), lambda b, g, pt: (b, g))
    any_spec = pl.BlockSpec(memory_space=pl.ANY)
    return pl.pallas_call(
        functools.partial(_attn_kernel, qbp=qbp, nc=nc, nb=nb, off2=0, sample=True),
        grid_spec=pltpu.PrefetchScalarGridSpec(
            num_scalar_prefetch=1, grid=(n_seq, N_KV),
            in_specs=[pl.BlockSpec((qbp, HPG * HEAD_DIM), lambda b, g, pt: (b, g)),
                      pl.BlockSpec((None, None, 3, r_l), lambda b, g, pt: (b, g, 0, 0)),
                      cmp_spec, cmp_spec, win_spec, win_spec,
                      tail_spec(), tail_spec(), tail_spec(), tail_spec(),
                      pl.BlockSpec((None, N_IMG, KEY_TILE, r_l), lambda b, g, pt: (g, 0, 0, 0)),
                      pl.BlockSpec((None, ct.shape[1], r_l), lambda b, g, pt: (g, 0, 0)),
                      any_spec, any_spec],
            out_specs=pl.BlockSpec((qbp, HPG * HEAD_DIM), lambda b, g, pt: (b, g)),
            scratch_shapes=[pltpu.VMEM((nb, 8, r_l), F32), pltpu.VMEM((nc, KEY_TILE), F32),
                            pltpu.VMEM((2, n_pages * PAGE_ROWS, HEAD_DIM), F32),
                            pltpu.VMEM((2, n_pages * PAGE_ROWS, HEAD_DIM), F32),
                            pltpu.SemaphoreType.DMA((2, 2))]),
        out_shape=jax.ShapeDtypeStruct((n_seq * qbp, N_HEADS * HEAD_DIM), BF16),
        compiler_params=_params(("arbitrary", "arbitrary"), 48),
        name="nsa_sample",
    )(page_table, q_pad, gates_t, kcmp, vcmp, kw, vw, ks_t, vs_t, kw_t, vw_t, img, ct, ks_cache, vs_cache)


def _gates_to_lanes(g_sig, n_tiles, q_rows, qbp):
    g5 = g_sig.reshape(n_tiles, q_rows, 3, N_KV, HPG)
    if qbp > q_rows:
        g5 = jnp.pad(g5, ((0, 0), (0, qbp - q_rows), (0, 0), (0, 0), (0, 0)))
    return g5.transpose(0, 3, 2, 4, 1).reshape(n_tiles, N_KV, 3, HPG * qbp)


def _history_rows(state, dec_seq):
    n_seq, _, c = state.shape
    z = jnp.zeros((n_seq, dec_seq - 2, c), state.dtype)
    s1 = jnp.concatenate([state[:, 1:2], jnp.zeros((n_seq, 1, c), state.dtype), z], axis=1)
    s2 = jnp.concatenate([state, z], axis=1)
    return s1.reshape(n_seq * dec_seq, c), s2.reshape(n_seq * dec_seq, c)


def _cmp_weights(pe, w1, w2):
    half = CMP_STRIDE * HEAD_DIM
    w1cat = jnp.concatenate([w1[:half], w1[half:]], axis=1).astype(BF16)
    return pe.reshape(2, half), w1cat, w2.astype(BF16)


def _token_group_rows(x):
    return x.reshape(-1, HEAD_DIM)


def kernel(x_prompt, x_sample, cache_k_cmp, cache_v_cmp, cache_k_sel, cache_v_sel, state_k_win, state_v_win,
           state_conv, state_ffn_conv, page_table, rel_bias, w_in, conv_w, conv_b, w_br_conv, w_br_nsa, w_out,
           pe_cmp_k, w_cmp_k1, w_cmp_k2, pe_cmp_v, w_cmp_v1, w_cmp_v2, ln1_g, ln1_b, w_ffn_in, ffn_conv_w,
           ffn_conv_b, w_ffn_out, ln2_g, ln2_b):
    seq = x_prompt.shape[1]
    n_seq, dec_seq = x_sample.shape[0], x_sample.shape[1]
    n_s = n_seq * dec_seq
    past = page_table.shape[1] * PAGE
    kvw = N_KV * HEAD_DIM
    qw = N_HEADS * HEAD_DIM
    tm = 1024

    x_all = jnp.concatenate([x_prompt[0], x_sample.reshape(n_s, D_MODEL)], axis=0)
    x_bf = x_all.astype(BF16)
    c_qkv = 3 * D_CONV
    c_gate = c_qkv + qw + 6 * kvw
    c_mix = c_gate + 3 * N_HEADS
    w_main = w_in[0, :, :c_gate].astype(BF16)
    w_gate = jnp.pad(w_in[0, :, c_gate:c_mix], ((0, 0), (0, 128 - 3 * N_HEADS))).astype(BF16)
    w_mix = w_in[0, :, c_mix:].astype(BF16)

    s1, s2 = _history_rows(state_conv[0], dec_seq)
    tn_c = 256
    z_conv, u_conv = _gconv(x_bf, w_main, (0, D_CONV // tn_c, 2 * D_CONV // tn_c), conv_w[0], conv_b, s1, s2,
                            seq, tm, tn_c, 48, "proj_conv")
    qkv_f, qkv_b = _proj(x_bf, w_main, c_qkv, qw + 6 * kvw, tm, 512)
    g_sig = _gate_proj(x_bf, w_gate, tm)[:, :3 * N_HEADS]

    pe_k, w1_k, w2_k = _cmp_weights(pe_cmp_k[0], w_cmp_k1[0], w_cmp_k2[0])
    pe_v, w1_v, w2_v = _cmp_weights(pe_cmp_v[0], w_cmp_v1[0], w_cmp_v2[0])
    chunk_w = CMP_STRIDE * kvw
    kc_p = qkv_b[:seq, qw:qw + kvw].reshape(seq // CMP_STRIDE, chunk_w)
    vc_p = qkv_b[:seq, qw + kvw:qw + 2 * kvw].reshape(seq // CMP_STRIDE, chunk_w)
    kcmp_p = _compress(kc_p, pe_k, w1_k, w2_k, "compress_k_prompt")
    vcmp_p = _compress(vc_p, pe_v, w1_v, w2_v, "compress_v_prompt")
    kcmp_s = _compress_paged(page_table, _token_group_rows(cache_k_cmp), pe_k, w1_k, w2_k, "compress_k_sample")
    vcmp_s = _compress_paged(page_table, _token_group_rows(cache_v_cmp), pe_v, w1_v, w2_v, "compress_v_sample")

    off2_p = (seq - KEY_TILE) // CMP_STRIDE
    nct_p = KEY_TILE * (-(-(off2_p + seq // CMP_STRIDE) // KEY_TILE))
    img_p, ct_p = _bias_images(rel_bias, KEY_TILE, nct_p, off2_p, "bias_prompt")
    gates_p = _gates_to_lanes(g_sig[:seq], seq // KEY_TILE, KEY_TILE, KEY_TILE)
    o_p = _attn_prompt(qkv_f, qkv_b, gates_p, kcmp_p, vcmp_p, img_p, ct_p, seq, off2_p)

    qbp_s = 32
    chunks_s = past // CMP_STRIDE
    img_s, ct_s = _bias_images(rel_bias, qbp_s, chunks_s, chunks_s, "bias_sample")
    gates_s = _gates_to_lanes(g_sig[seq:], n_seq, dec_seq, qbp_s)
    new_f = qkv_f[seq:].reshape(n_seq, dec_seq, qw + 6 * kvw)
    q_pad = jnp.pad(new_f[:, :, :qw], ((0, 0), (0, qbp_s - dec_seq), (0, 0))).reshape(n_seq * qbp_s, qw)

    def tail(col):
        return jnp.pad(new_f[:, :, col:col + kvw], ((0, 0), (0, KEY_TILE - dec_seq), (0, 0)))

    o_s = _attn_sample(page_table, q_pad, gates_s, kcmp_s, vcmp_s,
                       _token_group_rows(state_k_win), _token_group_rows(state_v_win),
                       tail(qw + 2 * kvw), tail(qw + 3 * kvw), tail(qw + 4 * kvw), tail(qw + 5 * kvw),
                       img_s, ct_s, _token_group_rows(cache_k_sel), _token_group_rows(cache_v_sel), qbp_s)
    o_s = o_s.reshape(n_seq, qbp_s, qw)[:, :dec_seq].reshape(n_s, qw)
    o_all = jnp.concatenate([o_p, o_s], axis=0)

    mix = _merge(x_bf, z_conv, o_all, w_br_conv[0].astype(BF16), w_br_nsa[0].astype(BF16), w_mix, 512, 256)
    r1 = _resid_mm(mix, w_out[0].astype(BF16), x_all, tm, 512, 48, "out_proj")
    h_f, h_b = _layer_norm(r1, ln1_g, ln1_b, 256, "ln1")
    f1, f2 = _history_rows(state_ffn_conv[0], dec_seq)
    tn_f = 256
    act, gp = _gconv(h_b, w_ffn_in[0].astype(BF16), (D_FF // tn_f, 0), ffn_conv_w[0], ffn_conv_b, f1, f2,
                     seq, tm, tn_f, 48, "ffn_in")
    r2 = _resid_mm(act, w_ffn_out[0].astype(BF16), h_f, 512, 256, 56, "ffn_out")
    y, _ = _layer_norm(r2, ln2_g, ln2_b, 256, "ln2")

    def rows_p(col):
        return qkv_f[:seq, col:col + kvw].reshape(1, 1, seq, N_KV, HEAD_DIM)

    def rows_s(col):
        return new_f[:, :, col:col + kvw].reshape(1, n_seq, dec_seq, N_KV, HEAD_DIM)

    def win_p(col):
        return qkv_f[seq - WINDOW:seq, col:col + kvw].reshape(1, 1, WINDOW, N_KV, HEAD_DIM)

    def win_s(state, col):
        new = new_f[:, :, col:col + kvw].reshape(n_seq, dec_seq, N_KV, HEAD_DIM)
        return jnp.concatenate([state[0][:, dec_seq:], new], axis=1)[None]

    def last2(u):
        c = u.shape[1]
        return (u[seq - 2:seq].reshape(1, 1, 2, c),
                u[seq:].reshape(n_seq, dec_seq, c)[:, dec_seq - 2:].reshape(1, n_seq, 2, c))

    conv_p, conv_s = last2(u_conv)
    ffn_p, ffn_s = last2(gp)
    c = qw
    return (y[:seq].reshape(1, seq, D_MODEL), y[seq:].reshape(n_seq, dec_seq, D_MODEL),
            rows_p(c), rows_s(c), rows_p(c + kvw), rows_s(c + kvw),
            rows_p(c + 2 * kvw), rows_s(c + 2 * kvw), rows_p(c + 3 * kvw), rows_s(c + 3 * kvw),
            win_p(c + 4 * kvw), win_s(state_k_win, c + 4 * kvw), win_p(c + 5 * kvw), win_s(state_v_win, c + 5 * kvw),
            conv_p, conv_s, ffn_p, ffn_s)
```

```python
import functools
import math

import jax
import jax.numpy as jnp
from jax import lax
from jax.experimental import pallas as pl
from jax.experimental.pallas import tpu as pltpu

F32 = jnp.float32
BF16 = jnp.bfloat16

D_MODEL = 4096
D_CONV = 2048
N_HEADS = 16
HEAD_DIM = 128
N_KV = 4
HPG = 4
CMP_STRIDE = 16
CMP_HID = 256
SEL_BLOCK = 64
N_SEL = 16
WINDOW = 512
FORCE_SCORE = 1e4
REL_BUCKETS = 32
REL_EXACT = 16
REL_MAX_DIST = 128
D_FF = 11008
PAGE = 128
ALPHA = 2.0 ** 0.25
LN_EPS = 1e-5
LOG2E = 1.0 / math.log(2.0)
Q_SCALE = HEAD_DIM ** -0.5 * LOG2E

KEY_TILE = 128
SEL_GROUP = 4
GCONV_PARTS = 2
SEL_STATES = 3
PAGE_ROWS = PAGE * N_KV
NEG = -1e30
MIB = 1024 * 1024

IMG_DIAG, IMG_PREV, IMG_ZERO, IMG_EDGE, IMG_NEG = 0, 1, 2, 3, 4
N_IMG = 5


def _params(sem, vmem_mib):
    return pltpu.CompilerParams(dimension_semantics=sem, vmem_limit_bytes=vmem_mib * MIB)


def _dot(a, b):
    return jnp.dot(a, b, preferred_element_type=F32)


def _proj_kernel(x_ref, w_ref, of_ref, ob_ref):
    acc = _dot(x_ref[...], w_ref[...])
    of_ref[...] = acc
    ob_ref[...] = acc.astype(BF16)


def _proj(x, w, col0, ncols, tm, tn):
    m_rows, k = x.shape
    c0 = col0 // tn
    return pl.pallas_call(
        _proj_kernel,
        grid=(m_rows // tm, ncols // tn),
        in_specs=[pl.BlockSpec((tm, k), lambda m, n: (m, 0)),
                  pl.BlockSpec((k, tn), lambda m, n: (0, c0 + n))],
        out_specs=[pl.BlockSpec((tm, tn), lambda m, n: (m, n)),
                   pl.BlockSpec((tm, tn), lambda m, n: (m, n))],
        out_shape=[jax.ShapeDtypeStruct((m_rows, ncols), F32),
                   jax.ShapeDtypeStruct((m_rows, ncols), BF16)],
        compiler_params=_params(("arbitrary", "arbitrary"), 48),
        name="proj_qkv",
    )(x, w)


def _gate_kernel(x_ref, w_ref, o_ref):
    o_ref[...] = jax.nn.sigmoid(_dot(x_ref[...], w_ref[...]))


def _gate_proj(x, w, tm):
    m_rows, k = x.shape
    n = w.shape[1]
    return pl.pallas_call(
        _gate_kernel,
        grid=(m_rows // tm,),
        in_specs=[pl.BlockSpec((tm, k), lambda m: (m, 0)),
                  pl.BlockSpec((k, n), lambda m: (0, 0))],
        out_specs=pl.BlockSpec((tm, n), lambda m: (m, 0)),
        out_shape=jax.ShapeDtypeStruct((m_rows, n), F32),
        compiler_params=_params(("arbitrary",), 40),
        name="proj_gates",
    )(x, w)


def _gconv_kernel(*refs, three, n_prompt_tiles):
    if three:
        x_ref, w0_ref, w1_ref, w2_ref, cw_ref, cb_ref, s1_ref, s2_ref, z_ref, u_ref, carry_ref = refs
    else:
        x_ref, w0_ref, w1_ref, cw_ref, cb_ref, s1_ref, s2_ref, z_ref, u_ref, carry_ref = refs
    m = pl.program_id(0)
    n = pl.program_id(1)
    is_sample = m >= n_prompt_tiles
    tm, tn = u_ref.shape
    hm = tm // GCONV_PARTS
    cw = cw_ref[...]
    cb = cb_ref[...]
    c = carry_ref[n]
    c = jnp.where(m == 0, jnp.zeros_like(c), c)
    prev_a, prev_b = c[6:7], c[7:8]
    row = lax.broadcasted_iota(jnp.int32, (hm, tn), 0)
    rmask = row & jnp.where(is_sample, 7, hm - 1)
    w0 = w0_ref[...].astype(BF16)
    w1 = w1_ref[...].astype(BF16)
    w2 = w2_ref[...].astype(BF16) if three else None
    for part in range(GCONV_PARTS):
        rows = slice(part * hm, (part + 1) * hm)
        x = x_ref[rows, :]
        p0 = _dot(x, w0)
        u = _dot(x, w1)
        if three:
            u = u * _dot(x, w2)
        u_ref[rows, :] = u
        fill1 = jnp.where(is_sample, s1_ref[rows, :], prev_b)
        fill2 = jnp.where(is_sample, s2_ref[rows, :], jnp.where(row == 0, prev_a, prev_b))
        prev1 = jnp.where(rmask == 0, fill1, pltpu.roll(u, 1, 0))
        prev2 = jnp.where(rmask < 2, fill2, pltpu.roll(u, 2, 0))
        conv = cb + prev2 * cw[0:1] + prev1 * cw[1:2] + u * cw[2:3]
        if three:
            z = p0 * conv
        else:
            z = jax.nn.silu(conv) * p0
        z_ref[rows, :] = z.astype(z_ref.dtype)
        prev_a, prev_b = u[hm - 2:hm - 1], u[hm - 1:hm]
        if part == GCONV_PARTS - 1:
            carry_ref[n] = u[hm - 8:hm]


def _gconv(x, w, col_blocks, cw, cb, s1, s2, n_prompt_rows, tm, tn, vmem_mib, name):
    m_rows, k = x.shape
    ncols = cw.shape[1]
    three = len(col_blocks) == 3
    assert m_rows - n_prompt_rows == tm and n_prompt_rows % tm == 0
    nt = ncols // tn
    w_specs = [pl.BlockSpec((k, tn), functools.partial(lambda m, n, c: (0, c + n), c=c)) for c in col_blocks]
    col_spec = pl.BlockSpec((tm, tn), lambda m, n: (m, n))
    return pl.pallas_call(
        functools.partial(_gconv_kernel, three=three, n_prompt_tiles=n_prompt_rows // tm),
        grid=(m_rows // tm, nt),
        in_specs=[pl.BlockSpec((tm, k), lambda m, n: (m, 0))] + w_specs + [
            pl.BlockSpec((3, tn), lambda m, n: (0, n)),
            pl.BlockSpec((1, tn), lambda m, n: (0, n)),
            pl.BlockSpec((tm, tn), lambda m, n: (0, n)),
            pl.BlockSpec((tm, tn), lambda m, n: (0, n))],
        out_specs=[col_spec, col_spec],
        out_shape=[jax.ShapeDtypeStruct((m_rows, ncols), BF16),
                   jax.ShapeDtypeStruct((m_rows, ncols), F32)],
        scratch_shapes=[pltpu.VMEM((nt, 8, tn), F32)],
        compiler_params=_params(("arbitrary", "arbitrary"), vmem_mib),
        name=name,
    )(x, *([w] * len(col_blocks)), cw, cb, s1, s2)


def _merge_kernel(x_ref, z_ref, o_ref, wc_ref, wn_ref, wga_ref, wgb_ref, out_ref):
    x = x_ref[...]
    a = _dot(z_ref[...], wc_ref[...])
    b = _dot(o_ref[...], wn_ref[...])
    ga = jax.nn.sigmoid(_dot(x, wga_ref[...]))
    gb = jax.nn.sigmoid(_dot(x, wgb_ref[...]))
    out_ref[...] = (ga * a + gb * b).astype(BF16)


def _merge(x, z, o, wc, wn, wg, tm, tn):
    m_rows, k = x.shape
    kc = z.shape[1]
    nt = D_MODEL // tn
    return pl.pallas_call(
        _merge_kernel,
        grid=(m_rows // tm, nt),
        in_specs=[pl.BlockSpec((tm, k), lambda m, n: (m, 0)),
                  pl.BlockSpec((tm, kc), lambda m, n: (m, 0)),
                  pl.BlockSpec((tm, kc), lambda m, n: (m, 0)),
                  pl.BlockSpec((kc, tn), lambda m, n: (0, n)),
                  pl.BlockSpec((kc, tn), lambda m, n: (0, n)),
                  pl.BlockSpec((k, tn), lambda m, n: (0, n)),
                  pl.BlockSpec((k, tn), lambda m, n: (0, nt + n))],
        out_specs=pl.BlockSpec((tm, tn), lambda m, n: (m, n)),
        out_shape=jax.ShapeDtypeStruct((m_rows, D_MODEL), BF16),
        compiler_params=_params(("arbitrary", "arbitrary"), 56),
        name="merge_branches",
    )(x, z, o, wc, wn, wg, wg)


def _resid_kernel(l_ref, w_ref, r_ref, o_ref):
    o_ref[...] = ALPHA * r_ref[...] + _dot(l_ref[...], w_ref[...])


def _resid2_kernel(l_ref, w_ref, rp_ref, rs_ref, o_ref, *, n_prompt_tiles):
    res = jnp.where(pl.program_id(0) < n_prompt_tiles, rp_ref[...], rs_ref[...])
    o_ref[...] = ALPHA * res + _dot(l_ref[...], w_ref[...])


def _resid_mm_split(lhs, w, res_p, res_s, tm, tn, vmem_mib, name):
    m_rows, k = lhs.shape
    n_cols = w.shape[1]
    npt = res_p.shape[0] // tm
    assert res_s.shape[0] == tm and npt * tm + tm == m_rows
    return pl.pallas_call(
        functools.partial(_resid2_kernel, n_prompt_tiles=npt),
        grid=(m_rows // tm, n_cols // tn),
        in_specs=[pl.BlockSpec((tm, k), lambda m, n: (m, 0)),
                  pl.BlockSpec((k, tn), lambda m, n: (0, n)),
                  pl.BlockSpec((tm, tn), lambda m, n: (jnp.minimum(m, npt - 1), n)),
                  pl.BlockSpec((tm, tn), lambda m, n: (0, n))],
        out_specs=pl.BlockSpec((tm, tn), lambda m, n: (m, n)),
        out_shape=jax.ShapeDtypeStruct((m_rows, n_cols), F32),
        compiler_params=_params(("arbitrary", "arbitrary"), vmem_mib),
        name=name,
    )(lhs, w, res_p, res_s)


def _resid_mm(lhs, w, res, tm, tn, vmem_mib, name):
    m_rows, k = lhs.shape
    n_cols = w.shape[1]
    return pl.pallas_call(
        _resid_kernel,
        grid=(m_rows // tm, n_cols // tn),
        in_specs=[pl.BlockSpec((tm, k), lambda m, n: (m, 0)),
                  pl.BlockSpec((k, tn), lambda m, n: (0, n)),
                  pl.BlockSpec((tm, tn), lambda m, n: (m, n))],
        out_specs=pl.BlockSpec((tm, tn), lambda m, n: (m, n)),
        out_shape=jax.ShapeDtypeStruct((m_rows, n_cols), F32),
        compiler_params=_params(("arbitrary", "arbitrary"), vmem_mib),
        name=name,
    )(lhs, w, res)


def _ln_kernel(x_ref, g_ref, b_ref, of_ref, ob_ref):
    x = x_ref[...]
    mu = jnp.mean(x, axis=-1, keepdims=True)
    xc = x - mu
    var = jnp.mean(xc * xc, axis=-1, keepdims=True)
    y = xc * lax.rsqrt(var + LN_EPS) * g_ref[...] + b_ref[...]
    of_ref[...] = y
    ob_ref[...] = y.astype(BF16)


def _layer_norm(x, g, b, tr, name):
    m_rows, d = x.shape
    row_spec = pl.BlockSpec((tr, d), lambda m: (m, 0))
    vec_spec = pl.BlockSpec((1, d), lambda m: (0, 0))
    return pl.pallas_call(
        _ln_kernel,
        grid=(m_rows // tr,),
        in_specs=[row_spec, vec_spec, vec_spec],
        out_specs=[row_spec, row_spec],
        out_shape=[jax.ShapeDtypeStruct((m_rows, d), F32), jax.ShapeDtypeStruct((m_rows, d), BF16)],
        compiler_params=_params(("arbitrary",), 48),
        name=name,
    )(x, g, b)


def _ln_split_kernel(x_ref, g_ref, b_ref, yp_ref, ys_ref, *, n_prompt_tiles):
    x = x_ref[...]
    mu = jnp.mean(x, axis=-1, keepdims=True)
    xc = x - mu
    var = jnp.mean(xc * xc, axis=-1, keepdims=True)
    y = xc * lax.rsqrt(var + LN_EPS) * g_ref[...] + b_ref[...]
    m = pl.program_id(0)

    @pl.when(m < n_prompt_tiles)
    def _():
        yp_ref[...] = y

    @pl.when(m >= n_prompt_tiles)
    def _():
        ys_ref[...] = y


def _layer_norm_split(x, g, b, n_prompt_rows, tr, name):
    m_rows, d = x.shape
    npt = n_prompt_rows // tr
    vec_spec = pl.BlockSpec((1, d), lambda m: (0, 0))
    return pl.pallas_call(
        functools.partial(_ln_split_kernel, n_prompt_tiles=npt),
        grid=(m_rows // tr,),
        in_specs=[pl.BlockSpec((tr, d), lambda m: (m, 0)), vec_spec, vec_spec],
        out_specs=[pl.BlockSpec((tr, d), lambda m: (jnp.minimum(m, npt - 1), 0)),
                   pl.BlockSpec((tr, d), lambda m: (jnp.maximum(m - npt, 0), 0))],
        out_shape=[jax.ShapeDtypeStruct((n_prompt_rows, d), F32),
                   jax.ShapeDtypeStruct((m_rows - n_prompt_rows, d), F32)],
        compiler_params=_params(("arbitrary",), 48),
        name=name,
    )(x, g, b)


def _page_copies(pt_ref, seq, cache_ref, buf, slot, sem):
    return [pltpu.make_async_copy(
        cache_ref.at[pl.ds(pl.multiple_of(pt_ref[seq, j] * PAGE_ROWS, PAGE_ROWS), PAGE_ROWS)],
        buf.at[slot, pl.ds(j * PAGE_ROWS, PAGE_ROWS)], sem)
        for j in range(pt_ref.shape[1])]


def _prefetch_pages(pt_ref, b, n_seq, streams):
    slot = b % 2

    @pl.when(b == 0)
    def _():
        for cache_ref, buf, sem in streams:
            for cp in _page_copies(pt_ref, 0, cache_ref, buf, 0, sem.at[0]):
                cp.start()

    @pl.when(b + 1 < n_seq)
    def _():
        for cache_ref, buf, sem in streams:
            for cp in _page_copies(pt_ref, b + 1, cache_ref, buf, 1 - slot, sem.at[1 - slot]):
                cp.start()

    for cache_ref, buf, sem in streams:
        for cp in _page_copies(pt_ref, b, cache_ref, buf, slot, sem.at[slot]):
            cp.wait()
    return slot


def _compress_rows(xg, pe, w1, w2):
    rows = xg.shape[0]
    half = CMP_STRIDE * HEAD_DIM
    pa = _dot(jnp.broadcast_to(pe[0:1], (8, half)).astype(BF16), w1[:, :CMP_HID])[0:1]
    pb = _dot(jnp.broadcast_to(pe[1:2], (8, half)).astype(BF16), w1[:, CMP_HID:])[0:1]
    fs = _dot(xg, w1)
    first = fs[:, :CMP_HID] + pa
    second = fs[:, CMP_HID:] + pb
    hid = jax.nn.gelu(first + pltpu.roll(second, rows - 1, 0))
    return _dot(hid.astype(BF16), w2)


def _compress_kernel(x_ref, pe_ref, w1_ref, w2_ref, o_ref):
    row_w = N_KV * HEAD_DIM
    for g in range(N_KV):
        xg = jnp.concatenate(
            [x_ref[:, r * row_w + g * HEAD_DIM: r * row_w + (g + 1) * HEAD_DIM] for r in range(CMP_STRIDE)],
            axis=1).astype(BF16)
        o_ref[:, g * HEAD_DIM:(g + 1) * HEAD_DIM] = _compress_rows(
            xg, pe_ref[...], w1_ref[...], w2_ref[...]).astype(BF16)


def _compress(x2, pe2, w1cat, w2, name):
    rows, width = x2.shape
    return pl.pallas_call(
        _compress_kernel,
        grid=(1,),
        in_specs=[pl.BlockSpec((rows, width), lambda t: (0, 0)),
                  pl.BlockSpec(pe2.shape, lambda t: (0, 0)),
                  pl.BlockSpec(w1cat.shape, lambda t: (0, 0)),
                  pl.BlockSpec(w2.shape, lambda t: (0, 0))],
        out_specs=pl.BlockSpec((rows, N_KV * HEAD_DIM), lambda t: (0, 0)),
        out_shape=jax.ShapeDtypeStruct((rows, N_KV * HEAD_DIM), BF16),
        compiler_params=_params(("arbitrary",), 48),
        name=name,
    )(x2, pe2, w1cat, w2)


def _compress_paged_kernel(pt_ref, cache_ref, pe_ref, w1_ref, w2_ref, o_ref, buf, sem, *, chunks):
    b = pl.program_id(0)
    slot = _prefetch_pages(pt_ref, b, pl.num_programs(0), [(cache_ref, buf, sem)])
    stride = CMP_STRIDE * N_KV
    xs = []
    for g in range(N_KV):
        xs.append(jnp.concatenate(
            [buf[slot, pl.ds(r * N_KV + g, chunks, stride=stride), :] for r in range(CMP_STRIDE)], axis=1))
    xg = jnp.concatenate(xs, axis=0).astype(BF16)
    out = _compress_rows(xg, pe_ref[...], w1_ref[...], w2_ref[...])
    for g in range(N_KV):
        o_ref[:, g * HEAD_DIM:(g + 1) * HEAD_DIM] = out[g * chunks:(g + 1) * chunks].astype(BF16)


def _compress_paged(page_table, cache2, pe2, w1cat, w2, name):
    n_seq, n_pages = page_table.shape
    chunks = n_pages * PAGE // CMP_STRIDE
    return pl.pallas_call(
        functools.partial(_compress_paged_kernel, chunks=chunks),
        grid_spec=pltpu.PrefetchScalarGridSpec(
            num_scalar_prefetch=1, grid=(n_seq,),
            in_specs=[pl.BlockSpec(memory_space=pl.ANY),
                      pl.BlockSpec(pe2.shape, lambda b, pt: (0, 0)),
                      pl.BlockSpec(w1cat.shape, lambda b, pt: (0, 0)),
                      pl.BlockSpec(w2.shape, lambda b, pt: (0, 0))],
            out_specs=pl.BlockSpec((chunks, N_KV * HEAD_DIM), lambda b, pt: (b, 0)),
            scratch_shapes=[pltpu.VMEM((2, n_pages * PAGE_ROWS, HEAD_DIM), F32),
                            pltpu.SemaphoreType.DMA((2,))]),
        out_shape=jax.ShapeDtypeStruct((n_seq * chunks, N_KV * HEAD_DIM), BF16),
        compiler_params=_params(("arbitrary",), 40),
        name=name,
    )(page_table, cache2, pe2, w1cat, w2)


def _bias_kernel(rb_ref, img_ref, ct_ref, *, heads, qbp, rpk, off2):
    g = pl.program_id(0)
    r_l = heads * qbp
    lane = lax.broadcasted_iota(jnp.int32, (1, r_l), 1)
    n_l = lane // qbp
    tbl = []
    for k in range(REL_BUCKETS):
        row = jnp.zeros((1, r_l), F32)
        for n in range(heads):
            row = jnp.where(n_l == n, rb_ref[k, g * heads + n], row)
        tbl.append(row)

    def bias_of(dist):
        nn = jnp.maximum(dist, 0)
        nf = jnp.maximum(nn, 1).astype(F32)
        large = REL_EXACT + (jnp.log(nf / REL_EXACT) / math.log(REL_MAX_DIST / REL_EXACT)
                             * (REL_BUCKETS - REL_EXACT)).astype(jnp.int32)
        bucket = jnp.where(nn < REL_EXACT, nn, jnp.minimum(large, REL_BUCKETS - 1))
        val = jnp.zeros(dist.shape, F32)
        for k in range(REL_BUCKETS):
            val = jnp.where(bucket == k, tbl[k], val)
        return val

    shape = (KEY_TILE, r_l)
    kj = lax.broadcasted_iota(jnp.int32, shape, 0) // rpk
    qi = lax.broadcasted_iota(jnp.int32, shape, 1) % qbp
    far = tbl[REL_BUCKETS - 1]
    for r0 in range(0, rpk * KEY_TILE, KEY_TILE):
        rows = slice(r0, r0 + KEY_TILE)
        d0 = qi - (kj + r0 // rpk)
        img_ref[IMG_DIAG, rows] = jnp.where(d0 >= 0, (bias_of(d0) - far) * LOG2E, NEG)
        img_ref[IMG_PREV, rows] = (bias_of(d0 + KEY_TILE) - far) * LOG2E
        img_ref[IMG_ZERO, rows] = jnp.zeros(shape, F32)
        img_ref[IMG_EDGE, rows] = jnp.where(d0 < 0, 0.0, NEG)
        img_ref[IMG_NEG, rows] = jnp.full(shape, NEG, F32)
    for r0 in range(0, ct_ref.shape[0], KEY_TILE):
        dist = qi + CMP_STRIDE * (off2 - r0 // rpk - kj) - (2 * CMP_STRIDE - 1)
        ct_ref[r0:r0 + KEY_TILE, :] = jnp.where(dist >= 0, bias_of(dist) * LOG2E, NEG)


def _bias_images(rel_bias, heads, qbp, rpk, nct, off2, name):
    r_l = heads * qbp
    groups = N_HEADS // heads
    return pl.pallas_call(
        functools.partial(_bias_kernel, heads=heads, qbp=qbp, rpk=rpk, off2=off2),
        grid=(groups,),
        in_specs=[pl.BlockSpec(memory_space=pltpu.SMEM)],
        out_specs=[pl.BlockSpec((None, N_IMG, rpk * KEY_TILE, r_l), lambda g: (g, 0, 0, 0)),
                   pl.BlockSpec((None, rpk * nct, r_l), lambda g: (g, 0, 0))],
        out_shape=[jax.ShapeDtypeStruct((groups, N_IMG, rpk * KEY_TILE, r_l), F32),
                   jax.ShapeDtypeStruct((groups, rpk * nct, r_l), F32)],
        compiler_params=_params(("arbitrary",), 32),
        name=name,
    )(rel_bias)


def _pv(v, p):
    return lax.dot_general(v, p.astype(BF16), (((0,), (0,)), ((), ())), preferred_element_type=F32)


def _online_update(carry, st, v):
    m_run, l_run, acc = carry
    m_new = jnp.maximum(m_run, jnp.max(st, axis=0, keepdims=True))
    alpha = jnp.exp2(m_run - m_new)
    p = jnp.exp2(st - m_new)
    l_new = alpha * l_run + jnp.sum(p, axis=0, keepdims=True)
    return m_new, l_new, alpha * acc + _pv(v, p)


def _softmax_init(r_l):
    return jnp.full((1, r_l), NEG, F32), jnp.zeros((1, r_l), F32), jnp.zeros((HEAD_DIM, r_l), F32)


def _block_onehot(n_keys):
    kj = lax.broadcasted_iota(jnp.int32, (n_keys, KEY_TILE), 0)
    col = lax.broadcasted_iota(jnp.int32, (n_keys, KEY_TILE), 1)
    return jnp.where(col == kj // SEL_BLOCK, 1.0, 0.0).astype(BF16)


def _query_lanes(q):
    q = q * Q_SCALE
    qg = jnp.concatenate([q[:, n * HEAD_DIM:(n + 1) * HEAD_DIM] for n in range(HPG)], axis=0)
    return qg.T.astype(BF16)


def _cmp_and_select(q_t, kc, vc, cbias, t0, ps_ref, *, qbp, nc, nb):
    r_l = HPG * qbp
    s = _dot(kc, q_t) + cbias
    valid = cbias > 0.5 * NEG
    mx = jnp.max(s, axis=0, keepdims=True)
    mx = jnp.where(mx > 0.5 * NEG, mx, 0.0)
    e = jnp.where(valid, jnp.exp2(s - mx), 0.0)
    p_c = e / jnp.maximum(jnp.sum(e, axis=0, keepdims=True), 1e-30)
    o_c = _pv(vc, p_c)

    if qbp == KEY_TILE:
        psum = p_c[:, 0:qbp]
        for n in range(1, HPG):
            psum = psum + p_c[:, n * qbp:(n + 1) * qbp]
    else:
        psum = p_c
        for n in range(1, HPG):
            psum = psum + pltpu.roll(p_c, n * qbp, 1)
    ps_ref[...] = psum
    per_sel = SEL_BLOCK // CMP_STRIDE
    nbi = nc // per_sel
    imp = ps_ref[pl.ds(0, nbi, stride=per_sel), :]
    for c in range(1, per_sel):
        imp = imp + ps_ref[pl.ds(c, nbi, stride=per_sel), :]
    if nb > nbi:
        imp = jnp.concatenate([imp, jnp.zeros((nb - nbi, KEY_TILE), F32)], axis=0)
    jb = lax.broadcasted_iota(jnp.int32, (nb, KEY_TILE), 0)
    tq = t0 + lax.broadcasted_iota(jnp.int32, (nb, KEY_TILE), 1) % qbp
    cur = tq // SEL_BLOCK
    forced = (jb == 0) | (jb == cur) | (jb == cur - 1)
    score = jnp.where(jb <= cur, jnp.where(forced, FORCE_SCORE, imp), NEG)
    jbf = jb.astype(F32)
    selm = jnp.full((nb, KEY_TILE), NEG, F32)
    for _ in range(N_SEL):
        best = jnp.max(score, axis=0, keepdims=True)
        first = jnp.min(jnp.where(score == best, jbf, 1e9), axis=0, keepdims=True)
        pick = jbf == first
        selm = jnp.where(pick, 0.0, selm)
        score = jnp.where(pick, 3.0 * NEG, score)
    if r_l > KEY_TILE:
        selm = jnp.concatenate([selm] * (r_l // KEY_TILE), axis=1)
    return o_c, selm


def _combine(gt, o_c, sel, win, qbp):
    (_, l_s, acc_s), (_, l_w, acc_w) = sel, win
    o_t = gt[0:1] * o_c + gt[1:2] * (acc_s / l_s) + gt[2:3] * (acc_w / l_w)
    return o_t.T


def _attn_prompt_kernel(q_ref, gt_ref, kc_ref, vc_ref, ks_ref, vs_ref, kw_ref, vw_ref, img_ref, ct_ref,
                        o_ref, selg_ref, ps_ref, *, nc, nb, off2):
    qbp = KEY_TILE
    r_l = HPG * qbp
    i_tile = pl.program_id(1)
    t0 = i_tile * KEY_TILE
    q_t = _query_lanes(q_ref[...])
    start = pl.multiple_of(off2 - i_tile * (KEY_TILE // CMP_STRIDE), 8)
    o_c, selm = _cmp_and_select(q_t, kc_ref[...], vc_ref[...], ct_ref[pl.ds(start, nc), :], t0, ps_ref,
                                qbp=qbp, nc=nc, nb=nb)

    grp_keys = SEL_GROUP * KEY_TILE
    grp_blocks = grp_keys // SEL_BLOCK
    n_grp, pad_rows = selg_ref.shape[0], selg_ref.shape[1] - grp_blocks
    selg_ref[...] = jnp.concatenate(
        [selm.reshape(n_grp, grp_blocks, r_l), jnp.zeros((n_grp, pad_rows, r_l), F32)], axis=1).astype(BF16)
    onehot = _block_onehot(grp_keys)
    zpad = jnp.zeros((KEY_TILE - selg_ref.shape[1], r_l), BF16)
    n_win = WINDOW // KEY_TILE

    def img_of(d, edge):
        idx = jnp.minimum(d, IMG_ZERO)
        if edge:
            idx = jnp.where(d == n_win, IMG_EDGE, idx)
        return img_ref[jnp.where(d < 0, IMG_NEG, idx)]

    def grp_rows(it):
        return pl.ds(pl.multiple_of(it * grp_keys, grp_keys), grp_keys)

    n_it = (i_tile + SEL_GROUP) // SEL_GROUP
    n_far = jnp.maximum(i_tile - 1, 0) // SEL_GROUP


    def scores(it):
        k_aug = jnp.concatenate([ks_ref[grp_rows(it), :], onehot], axis=1)
        q_aug = jnp.concatenate([q_t, selg_ref[it], zpad], axis=0)
        return _dot(k_aug, q_aug)

    def multi_body(itn, states):
        sts = [scores(SEL_STATES * itn + k) for k in range(SEL_STATES)]
        return tuple(_online_update(states[k], sts[k], vs_ref[grp_rows(SEL_STATES * itn + k), :])
                     for k in range(SEL_STATES))

    def single_body(it, state):
        st = scores(it) + jnp.concatenate(
            [img_of(i_tile - (it * SEL_GROUP + t), False) for t in range(SEL_GROUP)], axis=0)
        return _online_update(state, st, vs_ref[grp_rows(it), :])

    n_multi = n_far // SEL_STATES
    states = lax.fori_loop(0, n_multi, multi_body, tuple(_softmax_init(r_l) for _ in range(SEL_STATES)))
    states = (lax.fori_loop(SEL_STATES * n_multi, n_it, single_body, states[0]),) + states[1:]
    m_all = states[0][0]
    for st_k in states[1:]:
        m_all = jnp.maximum(m_all, st_k[0])
    l_all, acc_all = None, None
    for m_k, l_k, acc_k in states:
        f_k = jnp.exp2(m_k - m_all)
        l_all = f_k * l_k if l_all is None else l_all + f_k * l_k
        acc_all = f_k * acc_k if acc_all is None else acc_all + f_k * acc_k
    sel = (m_all, l_all, acc_all)

    first_tile = jnp.maximum(i_tile - n_win, 0)
    rows = pl.ds(pl.multiple_of(first_tile * KEY_TILE, KEY_TILE), (n_win + 1) * KEY_TILE)
    add = jnp.concatenate([img_of(i_tile - (first_tile + t), True) for t in range(n_win + 1)], axis=0)
    win = _online_update(_softmax_init(r_l), _dot(kw_ref[rows, :], q_t) + add, vw_ref[rows, :])

    o_r = _combine(gt_ref[...], o_c, sel, win, qbp)
    for n in range(HPG):
        o_ref[:, n * HEAD_DIM:(n + 1) * HEAD_DIM] = o_r[n * qbp:(n + 1) * qbp].astype(o_ref.dtype)


SAMPLE_GROUP_COL = 64
SAMPLE_FAR_PARTS = 4


def _softmax_segments(segs):
    m = jnp.max(segs[0][0], axis=0, keepdims=True)
    for st, _ in segs[1:]:
        m = jnp.maximum(m, jnp.max(st, axis=0, keepdims=True))
    l_sum, acc = None, None
    for st, v in segs:
        p = jnp.exp2(st - m)
        l_part, a_part = jnp.sum(p, axis=0, keepdims=True), _pv(v, p)
        l_sum = l_part if l_sum is None else l_sum + l_part
        acc = a_part if acc is None else acc + a_part
    return m, l_sum, acc


def _attn_sample_kernel(pt_ref, q_ref, gt_ref, kc_ref, vc_ref, kw_ref, vw_ref, kst_ref, vst_ref, kwt_ref, vwt_ref,
                        img_ref, ct_ref, feat_ref, ks_hbm, vs_hbm, o_ref, kbuf, vbuf, sem, *, qb, nb):
    b = pl.program_id(0)
    slot = _prefetch_pages(pt_ref, b, pl.num_programs(0), [(ks_hbm, kbuf, sem.at[0]), (vs_hbm, vbuf, sem.at[1])])
    lanes = N_HEADS * qb
    past_rows = kbuf.shape[1]
    past = past_rows // N_KV
    tile_rows = KEY_TILE * N_KV
    tail_rows = kst_ref.shape[0]
    win_rows = kw_ref.shape[0]
    cmp_rows = kc_ref.shape[0]

    q = q_ref[...] * Q_SCALE
    q_t = jnp.concatenate([q[:, h * HEAD_DIM:(h + 1) * HEAD_DIM] for h in range(N_HEADS)], axis=0).T.astype(BF16)
    lane = lax.broadcasted_iota(jnp.int32, (N_KV, lanes), 1)
    grp_rows = jnp.where(lane // (HPG * qb) == lax.broadcasted_iota(jnp.int32, (N_KV, lanes), 0), 0.0, NEG)

    def q_aug(selm):
        top = jnp.zeros((SAMPLE_GROUP_COL, lanes), F32) if selm is None else jnp.concatenate(
            [selm, jnp.zeros((SAMPLE_GROUP_COL - nb, lanes), F32)], axis=0)
        rest = jnp.zeros((KEY_TILE - SAMPLE_GROUP_COL - N_KV, lanes), F32)
        return jnp.concatenate([q_t, jnp.concatenate([top, grp_rows, rest], axis=0).astype(BF16)], axis=0)

    def scores(k_rows, feat, qa):
        return _dot(jnp.concatenate([k_rows.astype(BF16), feat], axis=1), qa)

    qa_plain = q_aug(None)

    s = scores(kc_ref[...], feat_ref[0:cmp_rows, :], qa_plain) + ct_ref[...]
    mx = jnp.max(s, axis=0, keepdims=True)
    mx = jnp.where(mx > 0.5 * NEG, mx, 0.0)
    e = jnp.where(s > 0.5 * NEG, jnp.exp2(s - mx), 0.0)
    p_c = e / jnp.maximum(jnp.sum(e, axis=0, keepdims=True), 1e-30)
    o_c = _pv(vc_ref[...], p_c)

    l32 = lax.broadcasted_iota(jnp.int32, p_c.shape, 1) % (HPG * qb)
    psum = p_c
    for n in range(1, HPG):
        sh = n * qb
        psum = psum + jnp.where(l32 >= sh, pltpu.roll(p_c, sh, 1), pltpu.roll(p_c, sh + lanes - HPG * qb, 1))
    rows_per_block = SEL_BLOCK // CMP_STRIDE * N_KV
    nbi = cmp_rows // rows_per_block
    imp = jnp.sum(psum.reshape(nbi, rows_per_block, lanes), axis=1)
    imp = jnp.concatenate([imp, jnp.zeros((nb - nbi, lanes), F32)], axis=0)
    jb = lax.broadcasted_iota(jnp.int32, (nb, lanes), 0)
    cur = (past + lax.broadcasted_iota(jnp.int32, (nb, lanes), 1) % qb) // SEL_BLOCK
    forced = (jb == 0) | (jb == cur) | (jb == cur - 1)
    score = jnp.where(jb <= cur, jnp.where(forced, FORCE_SCORE, imp), NEG)
    jbf = jb.astype(F32)
    selm = jnp.full((nb, lanes), NEG, F32)
    for _ in range(N_SEL):
        best = jnp.max(score, axis=0, keepdims=True)
        first = jnp.min(jnp.where(score == best, jbf, 1e9), axis=0, keepdims=True)
        pick = jbf == first
        selm = jnp.where(pick, 0.0, selm)
        score = jnp.where(pick, 3.0 * NEG, score)

    qa_sel = q_aug(selm)
    far = past_rows - tile_rows
    segs = []
    for r0 in range(0, far, far // SAMPLE_FAR_PARTS):
        rows = slice(r0, r0 + far // SAMPLE_FAR_PARTS)
        segs.append((scores(kbuf[slot, rows, :], feat_ref[rows, :], qa_sel), vbuf[slot, rows, :].astype(BF16)))
    st_prev = scores(kbuf[slot, far:past_rows, :], feat_ref[far:past_rows, :], qa_sel) + img_ref[IMG_PREV]
    st_new = scores(kst_ref[...], feat_ref[past_rows:past_rows + tail_rows, :], qa_sel) + img_ref[IMG_DIAG, 0:tail_rows]
    segs += [(st_prev, vbuf[slot, far:past_rows, :].astype(BF16)), (st_new, vst_ref[...].astype(BF16))]
    sel = _softmax_segments(segs)

    a, c = tile_rows, win_rows - tile_rows
    st_w = scores(kw_ref[...], feat_ref[0:win_rows, :], qa_plain)
    st_w = jnp.concatenate([st_w[:a] + img_ref[IMG_EDGE], st_w[a:c], st_w[c:] + img_ref[IMG_PREV]], axis=0)
    st_wn = scores(kwt_ref[...], feat_ref[0:tail_rows, :], qa_plain) + img_ref[IMG_DIAG, 0:tail_rows]
    win = _softmax_segments([(st_w, vw_ref[...].astype(BF16)), (st_wn, vwt_ref[...].astype(BF16))])

    o_r = _combine(gt_ref[...], o_c, sel, win, qb)
    for h in range(N_HEADS):
        o_ref[:, h * HEAD_DIM:(h + 1) * HEAD_DIM] = o_r[h * qb:(h + 1) * qb]


def _attn_prompt(qkv_f, qkv_b, gates_t, kcmp, vcmp, img, ct, seq, off2):
    qbp = KEY_TILE
    r_l = HPG * qbp
    n_tiles = seq // qbp
    nc = seq // CMP_STRIDE
    nb = seq // SEL_BLOCK
    q_cols = N_HEADS * HEAD_DIM // HEAD_DIM

    def kv_spec(which):
        return pl.BlockSpec((seq, HEAD_DIM), functools.partial(lambda g, i, c: (0, c + g), c=q_cols + which * N_KV))

    cmp_spec = pl.BlockSpec((nc, HEAD_DIM), lambda g, i: (0, g))
    return pl.pallas_call(
        functools.partial(_attn_prompt_kernel, nc=nc, nb=nb, off2=off2),
        grid=(N_KV, n_tiles),
        in_specs=[pl.BlockSpec((qbp, HPG * HEAD_DIM), lambda g, i: (i, g)),
                  pl.BlockSpec((None, None, 3, r_l), lambda g, i: (i, g, 0, 0)),
                  cmp_spec, cmp_spec,
                  kv_spec(2), kv_spec(3), kv_spec(4), kv_spec(5),
                  pl.BlockSpec((None, N_IMG, KEY_TILE, r_l), lambda g, i: (g, 0, 0, 0)),
                  pl.BlockSpec((None, ct.shape[1], r_l), lambda g, i: (g, 0, 0))],
        out_specs=pl.BlockSpec((qbp, HPG * HEAD_DIM), lambda g, i: (i, g)),
        out_shape=jax.ShapeDtypeStruct((seq, N_HEADS * HEAD_DIM), BF16),
        scratch_shapes=[pltpu.VMEM((nb * SEL_BLOCK // (SEL_GROUP * KEY_TILE), 16, r_l), BF16),
                        pltpu.VMEM((nc, KEY_TILE), F32)],
        compiler_params=_params(("arbitrary", "arbitrary"), 48),
        name="nsa_prompt",
    )(qkv_f, gates_t, kcmp, vcmp, qkv_b, qkv_b, qkv_b, qkv_b, img, ct)


def _key_features(n_rows):
    r = jnp.arange(n_rows, dtype=jnp.int32)[:, None]
    c = jnp.arange(KEY_TILE, dtype=jnp.int32)[None, :]
    hit = (c == r // (SEL_BLOCK * N_KV)) | (c == SAMPLE_GROUP_COL + r % N_KV)
    return hit.astype(BF16)


def _attn_sample(page_table, q_new, gates_t, kcmp, vcmp, kw, vw, ks_t, vs_t, kw_t, vw_t, img, ct,
                 ks_cache, vs_cache, qb):
    n_seq, n_pages = page_table.shape
    past = n_pages * PAGE
    lanes = N_HEADS * qb
    nb = 8 * (-(-(past // SEL_BLOCK + 2) // 8))
    assert nb <= SAMPLE_GROUP_COL and lanes == KEY_TILE
    qw = N_HEADS * HEAD_DIM
    tail_rows = ks_t.shape[1]
    cmp_rows = past // CMP_STRIDE * N_KV
    feat = _key_features(past * N_KV + tail_rows)
    tail_spec = pl.BlockSpec((None, tail_rows, HEAD_DIM), lambda b, pt: (b, 0, 0))
    win_spec = pl.BlockSpec((WINDOW * N_KV, HEAD_DIM), lambda b, pt: (b, 0))
    cmp_spec = pl.BlockSpec((cmp_rows, HEAD_DIM), lambda b, pt: (b, 0))
    any_spec = pl.BlockSpec(memory_space=pl.ANY)
    return pl.pallas_call(
        functools.partial(_attn_sample_kernel, qb=qb, nb=nb),
        grid_spec=pltpu.PrefetchScalarGridSpec(
            num_scalar_prefetch=1, grid=(n_seq,),
            in_specs=[pl.BlockSpec((qb, qw), lambda b, pt: (b, 0)),
                      pl.BlockSpec((None, 3, lanes), lambda b, pt: (b, 0, 0)),
                      cmp_spec, cmp_spec, win_spec, win_spec,
                      tail_spec, tail_spec, tail_spec, tail_spec,
                      pl.BlockSpec((None,) + img.shape[1:], lambda b, pt: (0, 0, 0, 0)),
                      pl.BlockSpec((None,) + ct.shape[1:], lambda b, pt: (0, 0, 0)),
                      pl.BlockSpec(feat.shape, lambda b, pt: (0, 0)),
                      any_spec, any_spec],
            out_specs=pl.BlockSpec((qb, qw), lambda b, pt: (b, 0)),
            scratch_shapes=[pltpu.VMEM((2, n_pages * PAGE_ROWS, HEAD_DIM), F32),
                            pltpu.VMEM((2, n_pages * PAGE_ROWS, HEAD_DIM), F32),
                            pltpu.SemaphoreType.DMA((2, 2))]),
        out_shape=jax.ShapeDtypeStruct((n_seq * qb, qw), F32),
        compiler_params=_params(("arbitrary",), 56),
        name="nsa_sample",
    )(page_table, q_new, gates_t, kcmp, vcmp, kw, vw, ks_t, vs_t, kw_t, vw_t, img, ct, feat, ks_cache, vs_cache)


def _gates_to_lanes(g_sig, n_tiles, q_rows, qbp):
    g5 = g_sig.reshape(n_tiles, q_rows, 3, N_KV, HPG)
    if qbp > q_rows:
        g5 = jnp.pad(g5, ((0, 0), (0, qbp - q_rows), (0, 0), (0, 0), (0, 0)))
    return g5.transpose(0, 3, 2, 4, 1).reshape(n_tiles, N_KV, 3, HPG * qbp)


def _history_rows(state, dec_seq):
    n_seq, _, c = state.shape
    z = jnp.zeros((n_seq, dec_seq - 2, c), state.dtype)
    s1 = jnp.concatenate([state[:, 1:2], jnp.zeros((n_seq, 1, c), state.dtype), z], axis=1)
    s2 = jnp.concatenate([state, z], axis=1)
    return s1.reshape(n_seq * dec_seq, c), s2.reshape(n_seq * dec_seq, c)


def _cmp_weights(pe, w1, w2):
    half = CMP_STRIDE * HEAD_DIM
    w1cat = jnp.concatenate([w1[:half], w1[half:]], axis=1).astype(BF16)
    return pe.reshape(2, half), w1cat, w2.astype(BF16)


def _token_group_rows(x):
    return x.reshape(-1, HEAD_DIM)


def kernel(x_prompt, x_sample, cache_k_cmp, cache_v_cmp, cache_k_sel, cache_v_sel, state_k_win, state_v_win,
           state_conv, state_ffn_conv, page_table, rel_bias, w_in, conv_w, conv_b, w_br_conv, w_br_nsa, w_out,
           pe_cmp_k, w_cmp_k1, w_cmp_k2, pe_cmp_v, w_cmp_v1, w_cmp_v2, ln1_g, ln1_b, w_ffn_in, ffn_conv_w,
           ffn_conv_b, w_ffn_out, ln2_g, ln2_b):
    seq = x_prompt.shape[1]
    n_seq, dec_seq = x_sample.shape[0], x_sample.shape[1]
    n_s = n_seq * dec_seq
    past = page_table.shape[1] * PAGE
    kvw = N_KV * HEAD_DIM
    qw = N_HEADS * HEAD_DIM
    tm = 1024

    x_s2 = x_sample.reshape(n_s, D_MODEL)
    x_bf = jnp.concatenate([x_prompt[0].astype(BF16), x_s2.astype(BF16)], axis=0)
    c_qkv = 3 * D_CONV
    c_gate = c_qkv + qw + 6 * kvw
    c_mix = c_gate + 3 * N_HEADS
    w_main = w_in[0, :, :c_gate].astype(BF16)
    w_gate = jnp.pad(w_in[0, :, c_gate:c_mix], ((0, 0), (0, 128 - 3 * N_HEADS))).astype(BF16)
    w_mix = w_in[0, :, c_mix:].astype(BF16)

    s1, s2 = _history_rows(state_conv[0], dec_seq)
    tn_c = 256
    z_conv, u_conv = _gconv(x_bf, w_main, (0, D_CONV // tn_c, 2 * D_CONV // tn_c), conv_w[0], conv_b, s1, s2,
                            seq, tm, tn_c, 48, "proj_conv")
    qkv_f, qkv_b = _proj(x_bf, w_main, c_qkv, qw + 6 * kvw, tm, 512)
    g_sig = _gate_proj(x_bf, w_gate, tm)[:, :3 * N_HEADS]

    pe_k, w1_k, w2_k = _cmp_weights(pe_cmp_k[0], w_cmp_k1[0], w_cmp_k2[0])
    pe_v, w1_v, w2_v = _cmp_weights(pe_cmp_v[0], w_cmp_v1[0], w_cmp_v2[0])
    chunk_w = CMP_STRIDE * kvw
    kc_p = qkv_b[:seq, qw:qw + kvw].reshape(seq // CMP_STRIDE, chunk_w)
    vc_p = qkv_b[:seq, qw + kvw:qw + 2 * kvw].reshape(seq // CMP_STRIDE, chunk_w)
    kcmp_p = _compress(kc_p, pe_k, w1_k, w2_k, "compress_k_prompt")
    vcmp_p = _compress(vc_p, pe_v, w1_v, w2_v, "compress_v_prompt")
    kcmp_s = _compress_paged(page_table, _token_group_rows(cache_k_cmp), pe_k, w1_k, w2_k, "compress_k_sample")
    vcmp_s = _compress_paged(page_table, _token_group_rows(cache_v_cmp), pe_v, w1_v, w2_v, "compress_v_sample")

    off2_p = (seq - KEY_TILE) // CMP_STRIDE
    nct_p = KEY_TILE * (-(-(off2_p + seq // CMP_STRIDE) // KEY_TILE))
    img_p, ct_p = _bias_images(rel_bias, HPG, KEY_TILE, 1, nct_p, off2_p, "bias_prompt")
    gates_p = _gates_to_lanes(g_sig[:seq], seq // KEY_TILE, KEY_TILE, KEY_TILE)
    o_p = _attn_prompt(qkv_f, qkv_b, gates_p, kcmp_p, vcmp_p, img_p, ct_p, seq, off2_p)

    chunks_s = past // CMP_STRIDE
    img_s, ct_s = _bias_images(rel_bias, N_HEADS, dec_seq, N_KV, chunks_s, chunks_s, "bias_sample")
    gates_s = g_sig[seq:].reshape(n_seq, dec_seq, 3, N_HEADS).transpose(0, 2, 3, 1).reshape(n_seq, 3, N_HEADS * dec_seq)
    new_f = qkv_f[seq:].reshape(n_seq, dec_seq, qw + 6 * kvw)

    def tail(col):
        return new_f[:, :, col:col + kvw].reshape(n_seq, dec_seq * N_KV, HEAD_DIM)

    o_s = _attn_sample(page_table, qkv_f[seq:, :qw], gates_s,
                       kcmp_s.reshape(-1, HEAD_DIM), vcmp_s.reshape(-1, HEAD_DIM),
                       _token_group_rows(state_k_win), _token_group_rows(state_v_win),
                       tail(qw + 2 * kvw), tail(qw + 3 * kvw), tail(qw + 4 * kvw), tail(qw + 5 * kvw),
                       img_s, ct_s, _token_group_rows(cache_k_sel), _token_group_rows(cache_v_sel), dec_seq)
    o_all = jnp.concatenate([o_p, o_s.astype(BF16)], axis=0)

    mix = _merge(x_bf, z_conv, o_all, w_br_conv[0].astype(BF16), w_br_nsa[0].astype(BF16), w_mix, 512, 256)
    r1 = _resid_mm_split(mix, w_out[0].astype(BF16), x_prompt[0], x_s2, tm, 512, 48, "out_proj")
    h_f, h_b = _layer_norm(r1, ln1_g, ln1_b, 256, "ln1")
    f1, f2 = _history_rows(state_ffn_conv[0], dec_seq)
    tn_f = 256
    act, gp = _gconv(h_b, w_ffn_in[0], (D_FF // tn_f, 0), ffn_conv_w[0], ffn_conv_b, f1, f2,
                     seq, tm, tn_f, 56, "ffn_in")
    r2 = _resid_mm(act, w_ffn_out[0].astype(BF16), h_f, 512, 256, 56, "ffn_out")
    y_p, y_s = _layer_norm_split(r2, ln2_g, ln2_b, seq, 256, "ln2")

    def rows_p(col):
        return qkv_f[:seq, col:col + kvw].reshape(1, 1, seq, N_KV, HEAD_DIM)

    def rows_s(col):
        return new_f[:, :, col:col + kvw].reshape(1, n_seq, dec_seq, N_KV, HEAD_DIM)

    def win_p(col):
        return qkv_f[seq - WINDOW:seq, col:col + kvw].reshape(1, 1, WINDOW, N_KV, HEAD_DIM)

    def win_s(state, col):
        new = new_f[:, :, col:col + kvw].reshape(n_seq, dec_seq, N_KV, HEAD_DIM)
        return jnp.concatenate([state[0][:, dec_seq:], new], axis=1)[None]

    def last2(u):
        c = u.shape[1]
        return (u[seq - 2:seq].reshape(1, 1, 2, c),
                u[seq:].reshape(n_seq, dec_seq, c)[:, dec_seq - 2:].reshape(1, n_seq, 2, c))

    conv_p, conv_s = last2(u_conv)
    ffn_p, ffn_s = last2(gp)
    c = qw
    return (y_p.reshape(1, seq, D_MODEL), y_s.reshape(n_seq, dec_seq, D_MODEL),
            rows_p(c), rows_s(c), rows_p(c + kvw), rows_s(c + kvw),
            rows_p(c + 2 * kvw), rows_s(c + 2 * kvw), rows_p(c + 3 * kvw), rows_s(c + 3 * kvw),
            win_p(c + 4 * kvw), win_s(state_k_win, c + 4 * kvw), win_p(c + 5 * kvw), win_s(state_v_win, c + 5 * kvw),
            conv_p, conv_s, ffn_p, ffn_s)
```

```python
import functools
import math

import jax
import jax.numpy as jnp
from jax import lax
from jax.experimental import pallas as pl
from jax.experimental.pallas import tpu as pltpu

F32 = jnp.float32
BF16 = jnp.bfloat16

D_MODEL = 4096
D_CONV = 2048
N_HEADS = 16
HEAD_DIM = 128
N_KV = 4
HPG = 4
CMP_STRIDE = 16
CMP_HID = 256
SEL_BLOCK = 64
N_SEL = 16
WINDOW = 512
FORCE_SCORE = 1e4
REL_BUCKETS = 32
REL_EXACT = 16
REL_MAX_DIST = 128
D_FF = 11008
PAGE = 128
ALPHA = 2.0 ** 0.25
LN_EPS = 1e-5
LOG2E = 1.0 / math.log(2.0)
Q_SCALE = HEAD_DIM ** -0.5 * LOG2E

KEY_TILE = 128
SEL_GROUP = 4
GCONV_PARTS = 2
SEL_STATES = 3
PAGE_ROWS = PAGE * N_KV
NEG = -1e30
MIB = 1024 * 1024

IMG_DIAG, IMG_PREV, IMG_ZERO, IMG_EDGE, IMG_NEG = 0, 1, 2, 3, 4
N_IMG = 5


def _params(sem, vmem_mib):
    return pltpu.CompilerParams(dimension_semantics=sem, vmem_limit_bytes=vmem_mib * MIB)


def _dot(a, b):
    return jnp.dot(a, b, preferred_element_type=F32)


def _proj_kernel(x_ref, w_ref, of_ref, ob_ref):
    acc = _dot(x_ref[...], w_ref[...].astype(BF16))
    of_ref[...] = acc
    ob_ref[...] = acc.astype(BF16)


def _proj(x, w, col0, ncols, tm, tn):
    m_rows, k = x.shape
    c0 = col0 // tn
    return pl.pallas_call(
        _proj_kernel,
        grid=(m_rows // tm, ncols // tn),
        in_specs=[pl.BlockSpec((tm, k), lambda m, n: (m, 0)),
                  pl.BlockSpec((k, tn), lambda m, n: (0, c0 + n))],
        out_specs=[pl.BlockSpec((tm, tn), lambda m, n: (m, n)),
                   pl.BlockSpec((tm, tn), lambda m, n: (m, n))],
        out_shape=[jax.ShapeDtypeStruct((m_rows, ncols), F32),
                   jax.ShapeDtypeStruct((m_rows, ncols), BF16)],
        compiler_params=_params(("arbitrary", "arbitrary"), 56),
        name="proj_qkv",
    )(x, w)


def _gate_kernel(x_ref, w_ref, o_ref):
    o_ref[...] = jax.nn.sigmoid(_dot(x_ref[...], w_ref[...]))


def _gate_proj(x, w, tm):
    m_rows, k = x.shape
    n = w.shape[1]
    return pl.pallas_call(
        _gate_kernel,
        grid=(m_rows // tm,),
        in_specs=[pl.BlockSpec((tm, k), lambda m: (m, 0)),
                  pl.BlockSpec((k, n), lambda m: (0, 0))],
        out_specs=pl.BlockSpec((tm, n), lambda m: (m, 0)),
        out_shape=jax.ShapeDtypeStruct((m_rows, n), F32),
        compiler_params=_params(("arbitrary",), 40),
        name="proj_gates",
    )(x, w)


def _gconv_kernel(*refs, three, n_prompt_tiles):
    if three:
        x_ref, w0_ref, w1_ref, w2_ref, cw_ref, cb_ref, s1_ref, s2_ref, z_ref, u_ref, carry_ref = refs
    else:
        x_ref, w0_ref, w1_ref, cw_ref, cb_ref, s1_ref, s2_ref, z_ref, u_ref, carry_ref = refs
    m = pl.program_id(0)
    n = pl.program_id(1)
    is_sample = m >= n_prompt_tiles
    tm, tn = u_ref.shape
    hm = tm // GCONV_PARTS
    cw = cw_ref[...]
    cb = cb_ref[...]
    c = carry_ref[n]
    c = jnp.where(m == 0, jnp.zeros_like(c), c)
    prev_a, prev_b = c[6:7], c[7:8]
    row = lax.broadcasted_iota(jnp.int32, (hm, tn), 0)
    rmask = row & jnp.where(is_sample, 7, hm - 1)
    w0 = w0_ref[...].astype(BF16)
    w1 = w1_ref[...].astype(BF16)
    w2 = w2_ref[...].astype(BF16) if three else None
    for part in range(GCONV_PARTS):
        rows = slice(part * hm, (part + 1) * hm)
        x = x_ref[rows, :]
        p0 = _dot(x, w0)
        u = _dot(x, w1)
        if three:
            u = u * _dot(x, w2)
        u_ref[rows, :] = u
        fill1 = jnp.where(is_sample, s1_ref[rows, :], prev_b)
        fill2 = jnp.where(is_sample, s2_ref[rows, :], jnp.where(row == 0, prev_a, prev_b))
        prev1 = jnp.where(rmask == 0, fill1, pltpu.roll(u, 1, 0))
        prev2 = jnp.where(rmask < 2, fill2, pltpu.roll(u, 2, 0))
        conv = cb + prev2 * cw[0:1] + prev1 * cw[1:2] + u * cw[2:3]
        if three:
            z = p0 * conv
        else:
            z = jax.nn.silu(conv) * p0
        z_ref[rows, :] = z.astype(z_ref.dtype)
        prev_a, prev_b = u[hm - 2:hm - 1], u[hm - 1:hm]
        if part == GCONV_PARTS - 1:
            carry_ref[n] = u[hm - 8:hm]


def _gconv(x, w, col_blocks, cw, cb, s1, s2, n_prompt_rows, tm, tn, vmem_mib, name):
    m_rows, k = x.shape
    ncols = cw.shape[1]
    three = len(col_blocks) == 3
    assert m_rows - n_prompt_rows == tm and n_prompt_rows % tm == 0
    nt = ncols // tn
    w_specs = [pl.BlockSpec((k, tn), functools.partial(lambda m, n, c: (0, c + n), c=c)) for c in col_blocks]
    col_spec = pl.BlockSpec((tm, tn), lambda m, n: (m, n))
    return pl.pallas_call(
        functools.partial(_gconv_kernel, three=three, n_prompt_tiles=n_prompt_rows // tm),
        grid=(m_rows // tm, nt),
        in_specs=[pl.BlockSpec((tm, k), lambda m, n: (m, 0))] + w_specs + [
            pl.BlockSpec((3, tn), lambda m, n: (0, n)),
            pl.BlockSpec((1, tn), lambda m, n: (0, n)),
            pl.BlockSpec((tm, tn), lambda m, n: (0, n)),
            pl.BlockSpec((tm, tn), lambda m, n: (0, n))],
        out_specs=[col_spec, col_spec],
        out_shape=[jax.ShapeDtypeStruct((m_rows, ncols), BF16),
                   jax.ShapeDtypeStruct((m_rows, ncols), F32)],
        scratch_shapes=[pltpu.VMEM((nt, 8, tn), F32)],
        compiler_params=_params(("arbitrary", "arbitrary"), vmem_mib),
        name=name,
    )(x, *([w] * len(col_blocks)), cw, cb, s1, s2)


def _merge_kernel(x_ref, z_ref, o_ref, wc_ref, wn_ref, wga_ref, wgb_ref, out_ref):
    x = x_ref[...]
    a = _dot(z_ref[...], wc_ref[...])
    b = _dot(o_ref[...], wn_ref[...])
    ga = jax.nn.sigmoid(_dot(x, wga_ref[...]))
    gb = jax.nn.sigmoid(_dot(x, wgb_ref[...]))
    out_ref[...] = (ga * a + gb * b).astype(BF16)


def _merge(x, z, o, wc, wn, wg, tm, tn):
    m_rows, k = x.shape
    kc = z.shape[1]
    nt = D_MODEL // tn
    return pl.pallas_call(
        _merge_kernel,
        grid=(m_rows // tm, nt),
        in_specs=[pl.BlockSpec((tm, k), lambda m, n: (m, 0)),
                  pl.BlockSpec((tm, kc), lambda m, n: (m, 0)),
                  pl.BlockSpec((tm, kc), lambda m, n: (m, 0)),
                  pl.BlockSpec((kc, tn), lambda m, n: (0, n)),
                  pl.BlockSpec((kc, tn), lambda m, n: (0, n)),
                  pl.BlockSpec((k, tn), lambda m, n: (0, n)),
                  pl.BlockSpec((k, tn), lambda m, n: (0, nt + n))],
        out_specs=pl.BlockSpec((tm, tn), lambda m, n: (m, n)),
        out_shape=jax.ShapeDtypeStruct((m_rows, D_MODEL), BF16),
        compiler_params=_params(("arbitrary", "arbitrary"), 56),
        name="merge_branches",
    )(x, z, o, wc, wn, wg, wg)


def _resid_kernel(l_ref, w_ref, r_ref, o_ref):
    o_ref[...] = ALPHA * r_ref[...] + _dot(l_ref[...], w_ref[...])


def _resid2_kernel(l_ref, w_ref, rp_ref, rs_ref, o_ref, *, n_prompt_tiles):
    res = jnp.where(pl.program_id(0) < n_prompt_tiles, rp_ref[...], rs_ref[...])
    o_ref[...] = ALPHA * res + _dot(l_ref[...], w_ref[...])


def _resid_mm_split(lhs, w, res_p, res_s, tm, tn, vmem_mib, name):
    m_rows, k = lhs.shape
    n_cols = w.shape[1]
    npt = res_p.shape[0] // tm
    assert res_s.shape[0] == tm and npt * tm + tm == m_rows
    return pl.pallas_call(
        functools.partial(_resid2_kernel, n_prompt_tiles=npt),
        grid=(m_rows // tm, n_cols // tn),
        in_specs=[pl.BlockSpec((tm, k), lambda m, n: (m, 0)),
                  pl.BlockSpec((k, tn), lambda m, n: (0, n)),
                  pl.BlockSpec((tm, tn), lambda m, n: (jnp.minimum(m, npt - 1), n)),
                  pl.BlockSpec((tm, tn), lambda m, n: (0, n))],
        out_specs=pl.BlockSpec((tm, tn), lambda m, n: (m, n)),
        out_shape=jax.ShapeDtypeStruct((m_rows, n_cols), F32),
        compiler_params=_params(("arbitrary", "arbitrary"), vmem_mib),
        name=name,
    )(lhs, w, res_p, res_s)


def _resid_mm(lhs, w, res, tm, tn, vmem_mib, name):
    m_rows, k = lhs.shape
    n_cols = w.shape[1]
    return pl.pallas_call(
        _resid_kernel,
        grid=(m_rows // tm, n_cols // tn),
        in_specs=[pl.BlockSpec((tm, k), lambda m, n: (m, 0)),
                  pl.BlockSpec((k, tn), lambda m, n: (0, n)),
                  pl.BlockSpec((tm, tn), lambda m, n: (m, n))],
        out_specs=pl.BlockSpec((tm, tn), lambda m, n: (m, n)),
        out_shape=jax.ShapeDtypeStruct((m_rows, n_cols), F32),
        compiler_params=_params(("arbitrary", "arbitrary"), vmem_mib),
        name=name,
    )(lhs, w, res)


def _ln_kernel(x_ref, g_ref, b_ref, of_ref, ob_ref):
    x = x_ref[...]
    mu = jnp.mean(x, axis=-1, keepdims=True)
    xc = x - mu
    var = jnp.mean(xc * xc, axis=-1, keepdims=True)
    y = xc * lax.rsqrt(var + LN_EPS) * g_ref[...] + b_ref[...]
    of_ref[...] = y
    ob_ref[...] = y.astype(BF16)


def _layer_norm(x, g, b, tr, name):
    m_rows, d = x.shape
    row_spec = pl.BlockSpec((tr, d), lambda m: (m, 0))
    vec_spec = pl.BlockSpec((1, d), lambda m: (0, 0))
    return pl.pallas_call(
        _ln_kernel,
        grid=(m_rows // tr,),
        in_specs=[row_spec, vec_spec, vec_spec],
        out_specs=[row_spec, row_spec],
        out_shape=[jax.ShapeDtypeStruct((m_rows, d), F32), jax.ShapeDtypeStruct((m_rows, d), BF16)],
        compiler_params=_params(("arbitrary",), 48),
        name=name,
    )(x, g, b)


def _ln_split_kernel(x_ref, g_ref, b_ref, yp_ref, ys_ref, *, n_prompt_tiles):
    x = x_ref[...]
    mu = jnp.mean(x, axis=-1, keepdims=True)
    xc = x - mu
    var = jnp.mean(xc * xc, axis=-1, keepdims=True)
    y = xc * lax.rsqrt(var + LN_EPS) * g_ref[...] + b_ref[...]
    m = pl.program_id(0)

    @pl.when(m < n_prompt_tiles)
    def _():
        yp_ref[...] = y

    @pl.when(m >= n_prompt_tiles)
    def _():
        ys_ref[...] = y


def _layer_norm_split(x, g, b, n_prompt_rows, tr, name):
    m_rows, d = x.shape
    npt = n_prompt_rows // tr
    vec_spec = pl.BlockSpec((1, d), lambda m: (0, 0))
    return pl.pallas_call(
        functools.partial(_ln_split_kernel, n_prompt_tiles=npt),
        grid=(m_rows // tr,),
        in_specs=[pl.BlockSpec((tr, d), lambda m: (m, 0)), vec_spec, vec_spec],
        out_specs=[pl.BlockSpec((tr, d), lambda m: (jnp.minimum(m, npt - 1), 0)),
                   pl.BlockSpec((tr, d), lambda m: (jnp.maximum(m - npt, 0), 0))],
        out_shape=[jax.ShapeDtypeStruct((n_prompt_rows, d), F32),
                   jax.ShapeDtypeStruct((m_rows - n_prompt_rows, d), F32)],
        compiler_params=_params(("arbitrary",), 48),
        name=name,
    )(x, g, b)


def _page_copies(pt_ref, seq, cache_ref, buf, slot, sem):
    return [pltpu.make_async_copy(
        cache_ref.at[pl.ds(pl.multiple_of(pt_ref[seq, j] * PAGE_ROWS, PAGE_ROWS), PAGE_ROWS)],
        buf.at[slot, pl.ds(j * PAGE_ROWS, PAGE_ROWS)], sem)
        for j in range(pt_ref.shape[1])]


def _prefetch_pages(pt_ref, b, n_seq, streams):
    slot = b % 2

    @pl.when(b == 0)
    def _():
        for cache_ref, buf, sem in streams:
            for cp in _page_copies(pt_ref, 0, cache_ref, buf, 0, sem.at[0]):
                cp.start()

    @pl.when(b + 1 < n_seq)
    def _():
        for cache_ref, buf, sem in streams:
            for cp in _page_copies(pt_ref, b + 1, cache_ref, buf, 1 - slot, sem.at[1 - slot]):
                cp.start()

    for cache_ref, buf, sem in streams:
        for cp in _page_copies(pt_ref, b, cache_ref, buf, slot, sem.at[slot]):
            cp.wait()
    return slot


def _compress_rows(xg, pe, w1, w2, next_chunk=1):
    rows = xg.shape[0]
    half = CMP_STRIDE * HEAD_DIM
    pa = _dot(jnp.broadcast_to(pe[0:1], (8, half)).astype(BF16), w1[:, :CMP_HID])[0:1]
    pb = _dot(jnp.broadcast_to(pe[1:2], (8, half)).astype(BF16), w1[:, CMP_HID:])[0:1]
    fs = _dot(xg, w1)
    first = fs[:, :CMP_HID] + pa
    second = fs[:, CMP_HID:] + pb
    hid = jax.nn.gelu(first + pltpu.roll(second, rows - next_chunk, 0))
    return _dot(hid.astype(BF16), w2)


def _compress_kernel(x_ref, pe_ref, w1_ref, w2_ref, o_ref):
    row_w = N_KV * HEAD_DIM
    for g in range(N_KV):
        xg = jnp.concatenate(
            [x_ref[:, r * row_w + g * HEAD_DIM: r * row_w + (g + 1) * HEAD_DIM] for r in range(CMP_STRIDE)],
            axis=1).astype(BF16)
        o_ref[:, g * HEAD_DIM:(g + 1) * HEAD_DIM] = _compress_rows(
            xg, pe_ref[...], w1_ref[...], w2_ref[...]).astype(BF16)


def _compress(x2, pe2, w1cat, w2, name):
    rows, width = x2.shape
    return pl.pallas_call(
        _compress_kernel,
        grid=(1,),
        in_specs=[pl.BlockSpec((rows, width), lambda t: (0, 0)),
                  pl.BlockSpec(pe2.shape, lambda t: (0, 0)),
                  pl.BlockSpec(w1cat.shape, lambda t: (0, 0)),
                  pl.BlockSpec(w2.shape, lambda t: (0, 0))],
        out_specs=pl.BlockSpec((rows, N_KV * HEAD_DIM), lambda t: (0, 0)),
        out_shape=jax.ShapeDtypeStruct((rows, N_KV * HEAD_DIM), BF16),
        compiler_params=_params(("arbitrary",), 48),
        name=name,
    )(x2, pe2, w1cat, w2)


def _compress_paged_kernel(pt_ref, cache_ref, pe_ref, w1_ref, w2_ref, o_ref, buf, sem, *, chunks):
    b = pl.program_id(0)
    slot = _prefetch_pages(pt_ref, b, pl.num_programs(0), [(cache_ref, buf, sem)])
    x3 = buf[slot].reshape(chunks, CMP_STRIDE * N_KV, HEAD_DIM)
    xg = jnp.concatenate(
        [x3[:, r * N_KV:(r + 1) * N_KV, :].reshape(chunks * N_KV, HEAD_DIM) for r in range(CMP_STRIDE)],
        axis=1).astype(BF16)
    o_ref[...] = _compress_rows(xg, pe_ref[...], w1_ref[...], w2_ref[...], N_KV).astype(BF16)


def _compress_paged(page_table, cache2, pe2, w1cat, w2, name):
    n_seq, n_pages = page_table.shape
    chunks = n_pages * PAGE // CMP_STRIDE
    return pl.pallas_call(
        functools.partial(_compress_paged_kernel, chunks=chunks),
        grid_spec=pltpu.PrefetchScalarGridSpec(
            num_scalar_prefetch=1, grid=(n_seq,),
            in_specs=[pl.BlockSpec(memory_space=pl.ANY),
                      pl.BlockSpec(pe2.shape, lambda b, pt: (0, 0)),
                      pl.BlockSpec(w1cat.shape, lambda b, pt: (0, 0)),
                      pl.BlockSpec(w2.shape, lambda b, pt: (0, 0))],
            out_specs=pl.BlockSpec((chunks * N_KV, HEAD_DIM), lambda b, pt: (b, 0)),
            scratch_shapes=[pltpu.VMEM((2, n_pages * PAGE_ROWS, HEAD_DIM), F32),
                            pltpu.SemaphoreType.DMA((2,))]),
        out_shape=jax.ShapeDtypeStruct((n_seq * chunks * N_KV, HEAD_DIM), BF16),
        compiler_params=_params(("arbitrary",), 40),
        name=name,
    )(page_table, cache2, pe2, w1cat, w2)


def _bias_kernel(rb_ref, img_ref, ct_ref, *, heads, qbp, rpk, off2):
    g = pl.program_id(0)
    r_l = heads * qbp
    lane = lax.broadcasted_iota(jnp.int32, (1, r_l), 1)
    n_l = lane // qbp
    tbl = []
    for k in range(REL_BUCKETS):
        row = jnp.zeros((1, r_l), F32)
        for n in range(heads):
            row = jnp.where(n_l == n, rb_ref[k, g * heads + n], row)
        tbl.append(row)

    def bias_of(dist):
        nn = jnp.maximum(dist, 0)
        nf = jnp.maximum(nn, 1).astype(F32)
        large = REL_EXACT + (jnp.log(nf / REL_EXACT) / math.log(REL_MAX_DIST / REL_EXACT)
                             * (REL_BUCKETS - REL_EXACT)).astype(jnp.int32)
        bucket = jnp.where(nn < REL_EXACT, nn, jnp.minimum(large, REL_BUCKETS - 1))
        val = jnp.zeros(dist.shape, F32)
        for k in range(REL_BUCKETS):
            val = jnp.where(bucket == k, tbl[k], val)
        return val

    shape = (KEY_TILE, r_l)
    kj = lax.broadcasted_iota(jnp.int32, shape, 0) // rpk
    qi = lax.broadcasted_iota(jnp.int32, shape, 1) % qbp
    far = tbl[REL_BUCKETS - 1]
    for r0 in range(0, rpk * KEY_TILE, KEY_TILE):
        rows = slice(r0, r0 + KEY_TILE)
        d0 = qi - (kj + r0 // rpk)
        img_ref[IMG_DIAG, rows] = jnp.where(d0 >= 0, (bias_of(d0) - far) * LOG2E, NEG)
        img_ref[IMG_PREV, rows] = (bias_of(d0 + KEY_TILE) - far) * LOG2E
        img_ref[IMG_ZERO, rows] = jnp.zeros(shape, F32)
        img_ref[IMG_EDGE, rows] = jnp.where(d0 < 0, 0.0, NEG)
        img_ref[IMG_NEG, rows] = jnp.full(shape, NEG, F32)
    for r0 in range(0, ct_ref.shape[0], KEY_TILE):
        dist = qi + CMP_STRIDE * (off2 - r0 // rpk - kj) - (2 * CMP_STRIDE - 1)
        ct_ref[r0:r0 + KEY_TILE, :] = jnp.where(dist >= 0, bias_of(dist) * LOG2E, NEG)


def _bias_images(rel_bias, heads, qbp, rpk, nct, off2, name):
    r_l = heads * qbp
    groups = N_HEADS // heads
    return pl.pallas_call(
        functools.partial(_bias_kernel, heads=heads, qbp=qbp, rpk=rpk, off2=off2),
        grid=(groups,),
        in_specs=[pl.BlockSpec(memory_space=pltpu.SMEM)],
        out_specs=[pl.BlockSpec((None, N_IMG, rpk * KEY_TILE, r_l), lambda g: (g, 0, 0, 0)),
                   pl.BlockSpec((None, rpk * nct, r_l), lambda g: (g, 0, 0))],
        out_shape=[jax.ShapeDtypeStruct((groups, N_IMG, rpk * KEY_TILE, r_l), F32),
                   jax.ShapeDtypeStruct((groups, rpk * nct, r_l), F32)],
        compiler_params=_params(("arbitrary",), 32),
        name=name,
    )(rel_bias)


def _pv(v, p):
    return lax.dot_general(v, p.astype(BF16), (((0,), (0,)), ((), ())), preferred_element_type=F32)


def _online_update(carry, st, v):
    m_run, l_run, acc = carry
    m_new = jnp.maximum(m_run, jnp.max(st, axis=0, keepdims=True))
    alpha = jnp.exp2(m_run - m_new)
    p = jnp.exp2(st - m_new)
    l_new = alpha * l_run + jnp.sum(p, axis=0, keepdims=True)
    return m_new, l_new, alpha * acc + _pv(v, p)


def _softmax_init(r_l):
    return jnp.full((1, r_l), NEG, F32), jnp.zeros((1, r_l), F32), jnp.zeros((HEAD_DIM, r_l), F32)


def _block_onehot(n_keys):
    kj = lax.broadcasted_iota(jnp.int32, (n_keys, KEY_TILE), 0)
    col = lax.broadcasted_iota(jnp.int32, (n_keys, KEY_TILE), 1)
    return jnp.where(col == kj // SEL_BLOCK, 1.0, 0.0).astype(BF16)


def _query_lanes(q):
    q = q * Q_SCALE
    qg = jnp.concatenate([q[:, n * HEAD_DIM:(n + 1) * HEAD_DIM] for n in range(HPG)], axis=0)
    return qg.T.astype(BF16)


def _cmp_and_select(q_t, kc, vc, cbias, t0, ps_ref, *, qbp, nc, nb, side_work=None):
    r_l = HPG * qbp
    s = _dot(kc, q_t) + cbias
    valid = cbias > 0.5 * NEG
    mx = jnp.max(s, axis=0, keepdims=True)
    mx = jnp.where(mx > 0.5 * NEG, mx, 0.0)
    e = jnp.where(valid, jnp.exp2(s - mx), 0.0)
    p_c = e / jnp.maximum(jnp.sum(e, axis=0, keepdims=True), 1e-30)
    o_c = _pv(vc, p_c)

    if qbp == KEY_TILE:
        psum = p_c[:, 0:qbp]
        for n in range(1, HPG):
            psum = psum + p_c[:, n * qbp:(n + 1) * qbp]
    else:
        psum = p_c
        for n in range(1, HPG):
            psum = psum + pltpu.roll(p_c, n * qbp, 1)
    ps_ref[...] = psum
    per_sel = SEL_BLOCK // CMP_STRIDE
    nbi = nc // per_sel
    imp = ps_ref[pl.ds(0, nbi, stride=per_sel), :]
    for c in range(1, per_sel):
        imp = imp + ps_ref[pl.ds(c, nbi, stride=per_sel), :]
    if nb > nbi:
        imp = jnp.concatenate([imp, jnp.zeros((nb - nbi, KEY_TILE), F32)], axis=0)
    jb = lax.broadcasted_iota(jnp.int32, (nb, KEY_TILE), 0)
    tq = t0 + lax.broadcasted_iota(jnp.int32, (nb, KEY_TILE), 1) % qbp
    cur = tq // SEL_BLOCK
    forced = (jb == 0) | (jb == cur) | (jb == cur - 1)
    score = jnp.where(jb <= cur, jnp.where(forced, FORCE_SCORE, imp), NEG)
    jbf = jb.astype(F32)
    side = side_work() if side_work is not None else None
    selm = jnp.full((nb, KEY_TILE), NEG, F32)
    for _ in range(N_SEL):
        best = jnp.max(score, axis=0, keepdims=True)
        first = jnp.min(jnp.where(score == best, jbf, 1e9), axis=0, keepdims=True)
        pick = jbf == first
        selm = jnp.where(pick, 0.0, selm)
        score = jnp.where(pick, 3.0 * NEG, score)
    if r_l > KEY_TILE:
        selm = jnp.concatenate([selm] * (r_l // KEY_TILE), axis=1)
    return o_c, selm, side


def _combine(gt, o_c, sel, win, qbp):
    (_, l_s, acc_s), (_, l_w, acc_w) = sel, win
    o_t = gt[0:1] * o_c + gt[1:2] * (acc_s / l_s) + gt[2:3] * (acc_w / l_w)
    return o_t.T


def _attn_prompt_kernel(q_ref, gt_ref, kc_ref, vc_ref, ks_ref, vs_ref, kw_ref, vw_ref, img_ref, ct_ref,
                        o_ref, selg_ref, ps_ref, *, nc, nb, off2):
    qbp = KEY_TILE
    r_l = HPG * qbp
    i_tile = pl.program_id(1)
    t0 = i_tile * KEY_TILE
    q_t = _query_lanes(q_ref[...])
    n_win = WINDOW // KEY_TILE

    def img_of(d, edge):
        idx = jnp.minimum(d, IMG_ZERO)
        if edge:
            idx = jnp.where(d == n_win, IMG_EDGE, idx)
        return img_ref[jnp.where(d < 0, IMG_NEG, idx)]

    def window():
        first_tile = jnp.maximum(i_tile - n_win, 0)
        rows = pl.ds(pl.multiple_of(first_tile * KEY_TILE, KEY_TILE), (n_win + 1) * KEY_TILE)
        add = jnp.concatenate([img_of(i_tile - (first_tile + t), True) for t in range(n_win + 1)], axis=0)
        return _online_update(_softmax_init(r_l), _dot(kw_ref[rows, :], q_t) + add, vw_ref[rows, :])

    start = pl.multiple_of(off2 - i_tile * (KEY_TILE // CMP_STRIDE), 8)
    o_c, selm, win = _cmp_and_select(q_t, kc_ref[...], vc_ref[...], ct_ref[pl.ds(start, nc), :], t0, ps_ref,
                                     qbp=qbp, nc=nc, nb=nb, side_work=window)

    grp_keys = SEL_GROUP * KEY_TILE
    grp_blocks = grp_keys // SEL_BLOCK
    n_grp, pad_rows = selg_ref.shape[0], selg_ref.shape[1] - grp_blocks
    selg_ref[...] = jnp.concatenate(
        [selm.reshape(n_grp, grp_blocks, r_l), jnp.zeros((n_grp, pad_rows, r_l), F32)], axis=1).astype(BF16)
    onehot = _block_onehot(grp_keys)
    zpad = jnp.zeros((KEY_TILE - selg_ref.shape[1], r_l), BF16)

    def grp_rows(it):
        return pl.ds(pl.multiple_of(it * grp_keys, grp_keys), grp_keys)

    n_it = (i_tile + SEL_GROUP) // SEL_GROUP
    n_far = jnp.maximum(i_tile - 1, 0) // SEL_GROUP


    def scores(it):
        k_aug = jnp.concatenate([ks_ref[grp_rows(it), :], onehot], axis=1)
        q_aug = jnp.concatenate([q_t, selg_ref[it], zpad], axis=0)
        return _dot(k_aug, q_aug)

    def multi_body(itn, states):
        sts = [scores(SEL_STATES * itn + k) for k in range(SEL_STATES)]
        return tuple(_online_update(states[k], sts[k], vs_ref[grp_rows(SEL_STATES * itn + k), :])
                     for k in range(SEL_STATES))

    def single_body(it, state):
        st = scores(it) + jnp.concatenate(
            [img_of(i_tile - (it * SEL_GROUP + t), False) for t in range(SEL_GROUP)], axis=0)
        return _online_update(state, st, vs_ref[grp_rows(it), :])

    n_multi = n_far // SEL_STATES
    states = lax.fori_loop(0, n_multi, multi_body, tuple(_softmax_init(r_l) for _ in range(SEL_STATES)))
    states = (lax.fori_loop(SEL_STATES * n_multi, n_it, single_body, states[0]),) + states[1:]
    m_all = states[0][0]
    for st_k in states[1:]:
        m_all = jnp.maximum(m_all, st_k[0])
    l_all, acc_all = None, None
    for m_k, l_k, acc_k in states:
        f_k = jnp.exp2(m_k - m_all)
        l_all = f_k * l_k if l_all is None else l_all + f_k * l_k
        acc_all = f_k * acc_k if acc_all is None else acc_all + f_k * acc_k
    sel = (m_all, l_all, acc_all)

    o_r = _combine(gt_ref[...], o_c, sel, win, qbp)
    for n in range(HPG):
        o_ref[:, n * HEAD_DIM:(n + 1) * HEAD_DIM] = o_r[n * qbp:(n + 1) * qbp].astype(o_ref.dtype)


SAMPLE_GROUP_COL = 64
SAMPLE_FAR_PARTS = 6


def _softmax_segments(segs):
    m = jnp.max(segs[0][0], axis=0, keepdims=True)
    for st, _ in segs[1:]:
        m = jnp.maximum(m, jnp.max(st, axis=0, keepdims=True))
    l_sum, acc = None, None
    for st, v in segs:
        p = jnp.exp2(st - m)
        l_part, a_part = jnp.sum(p, axis=0, keepdims=True), _pv(v, p)
        l_sum = l_part if l_sum is None else l_sum + l_part
        acc = a_part if acc is None else acc + a_part
    return m, l_sum, acc


def _attn_sample_kernel(pt_ref, q_ref, gt_ref, kc_ref, vc_ref, kw_ref, vw_ref, kst_ref, vst_ref, kwt_ref, vwt_ref,
                        img_ref, ct_ref, feat_ref, ks_hbm, vs_hbm, o_ref, kwo_ref, vwo_ref, selm_ref, kbuf, vbuf,
                        sem, *, qb, nb):
    b = pl.program_id(0)
    slot = _prefetch_pages(pt_ref, b, pl.num_programs(0), [(ks_hbm, kbuf, sem.at[0]), (vs_hbm, vbuf, sem.at[1])])
    lanes = N_HEADS * qb
    past_rows = kbuf.shape[1]
    past = past_rows // N_KV
    tile_rows = KEY_TILE * N_KV
    tail_rows = kst_ref.shape[0]
    win_rows = kw_ref.shape[0]
    cmp_rows = kc_ref.shape[0]

    q = q_ref[...] * Q_SCALE
    q_t = jnp.concatenate([q[:, h * HEAD_DIM:(h + 1) * HEAD_DIM] for h in range(N_HEADS)], axis=0).T.astype(BF16)
    lane = lax.broadcasted_iota(jnp.int32, (N_KV, lanes), 1)
    grp_rows = jnp.where(lane // (HPG * qb) == lax.broadcasted_iota(jnp.int32, (N_KV, lanes), 0), 0.0, NEG)

    q_aug = jnp.concatenate(
        [q_t, jnp.concatenate([jnp.zeros((SAMPLE_GROUP_COL, lanes), F32), grp_rows,
                               jnp.zeros((KEY_TILE - SAMPLE_GROUP_COL - N_KV, lanes), F32)], axis=0).astype(BF16)],
        axis=0)

    def scores(k_rows, feat):
        return _dot(jnp.concatenate([k_rows.astype(BF16), feat], axis=1), q_aug)

    far = past_rows - tile_rows
    part = far // SAMPLE_FAR_PARTS
    half = win_rows // 2
    sel_rows = [slice(r0, r0 + part) for r0 in range(0, far, part)] + [slice(far, past_rows)]
    pending = [lambda rows=rows: scores(kbuf[slot, rows, :], feat_ref[rows, :]) for rows in sel_rows]
    pending += [lambda: scores(kst_ref[...], feat_ref[0:tail_rows, :]),
                lambda: scores(kw_ref[0:half, :], feat_ref[0:half, :]),
                lambda: scores(kw_ref[half:win_rows, :], feat_ref[half:win_rows, :]),
                lambda: scores(kwt_ref[...], feat_ref[0:tail_rows, :])]
    raw = []

    def issue(n=1):
        for _ in range(n):
            if len(raw) < len(pending):
                raw.append(pending[len(raw)]())

    s = scores(kc_ref[...], feat_ref[0:cmp_rows, :]) + ct_ref[...]
    issue()
    mx = jnp.max(s, axis=0, keepdims=True)
    mx = jnp.where(mx > 0.5 * NEG, mx, 0.0)
    e = jnp.where(s > 0.5 * NEG, jnp.exp2(s - mx), 0.0)
    p_c = e / jnp.maximum(jnp.sum(e, axis=0, keepdims=True), 1e-30)
    o_c = _pv(vc_ref[...], p_c)
    issue()

    l32 = lax.broadcasted_iota(jnp.int32, p_c.shape, 1) % (HPG * qb)
    psum = p_c
    for n in range(1, HPG):
        sh = n * qb
        psum = psum + jnp.where(l32 >= sh, pltpu.roll(p_c, sh, 1), pltpu.roll(p_c, sh + lanes - HPG * qb, 1))
    rows_per_block = SEL_BLOCK // CMP_STRIDE * N_KV
    nbi = cmp_rows // rows_per_block
    imp = jnp.sum(psum.reshape(nbi, rows_per_block, lanes), axis=1)
    imp = jnp.concatenate([imp, jnp.zeros((nb - nbi, lanes), F32)], axis=0)
    jb = lax.broadcasted_iota(jnp.int32, (nb, lanes), 0)
    cur = (past + lax.broadcasted_iota(jnp.int32, (nb, lanes), 1) % qb) // SEL_BLOCK
    forced = (jb == 0) | (jb == cur) | (jb == cur - 1)
    score = jnp.where(jb <= cur, jnp.where(forced, FORCE_SCORE, imp), NEG)
    jbf = jb.astype(F32)
    selm = jnp.full((nb, lanes), NEG, F32)
    for rnd in range(N_SEL):
        best = jnp.max(score, axis=0, keepdims=True)
        first = jnp.min(jnp.where(score == best, jbf, 1e9), axis=0, keepdims=True)
        pick = jbf == first
        selm = jnp.where(pick, 0.0, selm)
        score = jnp.where(pick, 3.0 * NEG, score)
        if rnd % 2 == 1:
            issue()
    issue(len(pending))
    sel_st = raw[:len(sel_rows)]
    sel_st[-1] = sel_st[-1] + img_ref[IMG_PREV]
    st_new, st_old, st_mid, st_wn = raw[len(sel_rows):]
    st_new = st_new + img_ref[IMG_DIAG, 0:tail_rows]
    st_old = jnp.concatenate([st_old[:tile_rows] + img_ref[IMG_EDGE], st_old[tile_rows:]], axis=0)
    st_mid = jnp.concatenate([st_mid[:half - tile_rows], st_mid[half - tile_rows:] + img_ref[IMG_PREV]], axis=0)
    st_wn = st_wn + img_ref[IMG_DIAG, 0:tail_rows]

    for j in range(nb):
        selm_ref[j] = jnp.broadcast_to(selm[j:j + 1, :], (8, lanes))
    block_rows = SEL_BLOCK * N_KV

    def select(st, row0):
        n_rows = st.shape[0]
        nblk = -(-n_rows // block_rows)
        per = n_rows // nblk // 8
        mask = selm_ref[row0 // block_rows:row0 // block_rows + nblk]
        return (st.reshape(nblk, per, 8, lanes) + mask[:, None]).reshape(n_rows, lanes)

    segs = [(select(st, rows.start), vbuf[slot, rows, :].astype(BF16)) for st, rows in zip(sel_st, sel_rows)]
    segs.append((select(st_new, past_rows), vst_ref[...].astype(BF16)))
    sel = _softmax_segments(segs)

    win = _softmax_segments([(st_old, vw_ref[0:half, :].astype(BF16)),
                             (st_mid, vw_ref[half:win_rows, :].astype(BF16)),
                             (st_wn, vwt_ref[...].astype(BF16))])

    o_r = _combine(gt_ref[...], o_c, sel, win, qb)
    for h in range(N_HEADS):
        o_ref[:, h * HEAD_DIM:(h + 1) * HEAD_DIM] = o_r[h * qb:(h + 1) * qb]

    for src, new, dst in ((kw_ref, kwt_ref, kwo_ref), (vw_ref, vwt_ref, vwo_ref)):
        dst[0:win_rows - tail_rows, :] = src[tail_rows:win_rows, :]
        dst[win_rows - tail_rows:win_rows, :] = new[...]


def _attn_prompt(qkv_f, qkv_b, gates_t, kcmp, vcmp, img, ct, seq, off2):
    qbp = KEY_TILE
    r_l = HPG * qbp
    n_tiles = seq // qbp
    nc = seq // CMP_STRIDE
    nb = seq // SEL_BLOCK
    q_cols = N_HEADS * HEAD_DIM // HEAD_DIM

    def kv_spec(which):
        return pl.BlockSpec((seq, HEAD_DIM), functools.partial(lambda g, i, c: (0, c + g), c=q_cols + which * N_KV))

    cmp_spec = pl.BlockSpec((nc, HEAD_DIM), lambda g, i: (0, g))
    return pl.pallas_call(
        functools.partial(_attn_prompt_kernel, nc=nc, nb=nb, off2=off2),
        grid=(N_KV, n_tiles),
        in_specs=[pl.BlockSpec((qbp, HPG * HEAD_DIM), lambda g, i: (i, g)),
                  pl.BlockSpec((None, None, 3, r_l), lambda g, i: (i, g, 0, 0)),
                  cmp_spec, cmp_spec,
                  kv_spec(2), kv_spec(3), kv_spec(4), kv_spec(5),
                  pl.BlockSpec((None, N_IMG, KEY_TILE, r_l), lambda g, i: (g, 0, 0, 0)),
                  pl.BlockSpec((None, ct.shape[1], r_l), lambda g, i: (g, 0, 0))],
        out_specs=pl.BlockSpec((qbp, HPG * HEAD_DIM), lambda g, i: (i, g)),
        out_shape=jax.ShapeDtypeStruct((seq, N_HEADS * HEAD_DIM), BF16),
        scratch_shapes=[pltpu.VMEM((nb * SEL_BLOCK // (SEL_GROUP * KEY_TILE), 16, r_l), BF16),
                        pltpu.VMEM((nc, KEY_TILE), F32)],
        compiler_params=_params(("arbitrary", "arbitrary"), 48),
        name="nsa_prompt",
    )(qkv_f, gates_t, kcmp, vcmp, qkv_b, qkv_b, qkv_b, qkv_b, img, ct)


def _key_features(n_rows):
    r = jnp.arange(n_rows, dtype=jnp.int32)[:, None]
    c = jnp.arange(KEY_TILE, dtype=jnp.int32)[None, :]
    hit = (c == r // (SEL_BLOCK * N_KV)) | (c == SAMPLE_GROUP_COL + r % N_KV)
    return hit.astype(BF16)


def _attn_sample(page_table, q_new, gates_t, kcmp, vcmp, kw, vw, ks_t, vs_t, kw_t, vw_t, img, ct,
                 ks_cache, vs_cache, qb):
    n_seq, n_pages = page_table.shape
    past = n_pages * PAGE
    lanes = N_HEADS * qb
    nb = 8 * (-(-(past // SEL_BLOCK + 2) // 8))
    assert nb <= SAMPLE_GROUP_COL and lanes == KEY_TILE
    qw = N_HEADS * HEAD_DIM
    tail_rows = ks_t.shape[1]
    cmp_rows = past // CMP_STRIDE * N_KV
    feat = _key_features(past * N_KV + tail_rows)
    tail_spec = pl.BlockSpec((None, tail_rows, HEAD_DIM), lambda b, pt: (b, 0, 0))
    win_spec = pl.BlockSpec((WINDOW * N_KV, HEAD_DIM), lambda b, pt: (b, 0))
    cmp_spec = pl.BlockSpec((cmp_rows, HEAD_DIM), lambda b, pt: (b, 0))
    any_spec = pl.BlockSpec(memory_space=pl.ANY)
    return pl.pallas_call(
        functools.partial(_attn_sample_kernel, qb=qb, nb=nb),
        grid_spec=pltpu.PrefetchScalarGridSpec(
            num_scalar_prefetch=1, grid=(n_seq,),
            in_specs=[pl.BlockSpec((qb, qw), lambda b, pt: (b, 0)),
                      pl.BlockSpec((None, 3, lanes), lambda b, pt: (b, 0, 0)),
                      cmp_spec, cmp_spec, win_spec, win_spec,
                      tail_spec, tail_spec, tail_spec, tail_spec,
                      pl.BlockSpec((None,) + img.shape[1:], lambda b, pt: (0, 0, 0, 0)),
                      pl.BlockSpec((None,) + ct.shape[1:], lambda b, pt: (0, 0, 0)),
                      pl.BlockSpec(feat.shape, lambda b, pt: (0, 0)),
                      any_spec, any_spec],
            out_specs=[pl.BlockSpec((qb, qw), lambda b, pt: (b, 0)), win_spec, win_spec],
            scratch_shapes=[pltpu.VMEM((nb, 8, lanes), F32),
                            pltpu.VMEM((2, n_pages * PAGE_ROWS, HEAD_DIM), F32),
                            pltpu.VMEM((2, n_pages * PAGE_ROWS, HEAD_DIM), F32),
                            pltpu.SemaphoreType.DMA((2, 2))]),
        out_shape=[jax.ShapeDtypeStruct((n_seq * qb, qw), F32),
                   jax.ShapeDtypeStruct(kw.shape, kw.dtype), jax.ShapeDtypeStruct(vw.shape, vw.dtype)],
        compiler_params=_params(("arbitrary",), 56),
        name="nsa_sample",
    )(page_table, q_new, gates_t, kcmp, vcmp, kw, vw, ks_t, vs_t, kw_t, vw_t, img, ct, feat, ks_cache, vs_cache)


def _gates_to_lanes(g_sig, n_tiles, q_rows, qbp):
    g5 = g_sig.reshape(n_tiles, q_rows, 3, N_KV, HPG)
    if qbp > q_rows:
        g5 = jnp.pad(g5, ((0, 0), (0, qbp - q_rows), (0, 0), (0, 0), (0, 0)))
    return g5.transpose(0, 3, 2, 4, 1).reshape(n_tiles, N_KV, 3, HPG * qbp)


def _history_rows(state, dec_seq):
    n_seq, _, c = state.shape
    z = jnp.zeros((n_seq, dec_seq - 2, c), state.dtype)
    s1 = jnp.concatenate([state[:, 1:2], jnp.zeros((n_seq, 1, c), state.dtype), z], axis=1)
    s2 = jnp.concatenate([state, z], axis=1)
    return s1.reshape(n_seq * dec_seq, c), s2.reshape(n_seq * dec_seq, c)


def _cmp_weights(pe, w1, w2):
    half = CMP_STRIDE * HEAD_DIM
    w1cat = jnp.concatenate([w1[:half], w1[half:]], axis=1).astype(BF16)
    return pe.reshape(2, half), w1cat, w2.astype(BF16)


def _token_group_rows(x):
    return x.reshape(-1, HEAD_DIM)


def kernel(x_prompt, x_sample, cache_k_cmp, cache_v_cmp, cache_k_sel, cache_v_sel, state_k_win, state_v_win,
           state_conv, state_ffn_conv, page_table, rel_bias, w_in, conv_w, conv_b, w_br_conv, w_br_nsa, w_out,
           pe_cmp_k, w_cmp_k1, w_cmp_k2, pe_cmp_v, w_cmp_v1, w_cmp_v2, ln1_g, ln1_b, w_ffn_in, ffn_conv_w,
           ffn_conv_b, w_ffn_out, ln2_g, ln2_b):
    seq = x_prompt.shape[1]
    n_seq, dec_seq = x_sample.shape[0], x_sample.shape[1]
    n_s = n_seq * dec_seq
    past = page_table.shape[1] * PAGE
    kvw = N_KV * HEAD_DIM
    qw = N_HEADS * HEAD_DIM
    tm = 1024

    x_s2 = x_sample.reshape(n_s, D_MODEL)
    x_bf = jnp.concatenate([x_prompt[0].astype(BF16), x_s2.astype(BF16)], axis=0)
    c_qkv = 3 * D_CONV
    c_gate = c_qkv + qw + 6 * kvw
    c_mix = c_gate + 3 * N_HEADS
    w_main = w_in[0, :, :c_gate].astype(BF16)
    w_gate = jnp.pad(w_in[0, :, c_gate:c_mix], ((0, 0), (0, 128 - 3 * N_HEADS))).astype(BF16)
    w_mix = w_in[0, :, c_mix:].astype(BF16)

    s1, s2 = _history_rows(state_conv[0], dec_seq)
    tn_c = 256
    z_conv, u_conv = _gconv(x_bf, w_main, (0, D_CONV // tn_c, 2 * D_CONV // tn_c), conv_w[0], conv_b, s1, s2,
                            seq, tm, tn_c, 48, "proj_conv")
    qkv_f, qkv_b = _proj(x_bf, w_main, c_qkv, qw + 6 * kvw, tm, 512)
    g_sig = _gate_proj(x_bf, w_gate, tm)[:, :3 * N_HEADS]

    pe_k, w1_k, w2_k = _cmp_weights(pe_cmp_k[0], w_cmp_k1[0], w_cmp_k2[0])
    pe_v, w1_v, w2_v = _cmp_weights(pe_cmp_v[0], w_cmp_v1[0], w_cmp_v2[0])
    chunk_w = CMP_STRIDE * kvw
    kc_p = qkv_b[:seq, qw:qw + kvw].reshape(seq // CMP_STRIDE, chunk_w)
    vc_p = qkv_b[:seq, qw + kvw:qw + 2 * kvw].reshape(seq // CMP_STRIDE, chunk_w)
    kcmp_p = _compress(kc_p, pe_k, w1_k, w2_k, "compress_k_prompt")
    vcmp_p = _compress(vc_p, pe_v, w1_v, w2_v, "compress_v_prompt")
    kcmp_s = _compress_paged(page_table, _token_group_rows(cache_k_cmp), pe_k, w1_k, w2_k, "compress_k_sample")
    vcmp_s = _compress_paged(page_table, _token_group_rows(cache_v_cmp), pe_v, w1_v, w2_v, "compress_v_sample")

    off2_p = (seq - KEY_TILE) // CMP_STRIDE
    nct_p = KEY_TILE * (-(-(off2_p + seq // CMP_STRIDE) // KEY_TILE))
    img_p, ct_p = _bias_images(rel_bias, HPG, KEY_TILE, 1, nct_p, off2_p, "bias_prompt")
    gates_p = _gates_to_lanes(g_sig[:seq], seq // KEY_TILE, KEY_TILE, KEY_TILE)
    o_p = _attn_prompt(qkv_f, qkv_b, gates_p, kcmp_p, vcmp_p, img_p, ct_p, seq, off2_p)

    chunks_s = past // CMP_STRIDE
    img_s, ct_s = _bias_images(rel_bias, N_HEADS, dec_seq, N_KV, chunks_s, chunks_s, "bias_sample")
    gates_s = g_sig[seq:].reshape(n_seq, dec_seq, 3, N_HEADS).transpose(0, 2, 3, 1).reshape(n_seq, 3, N_HEADS * dec_seq)
    new_f = qkv_f[seq:].reshape(n_seq, dec_seq, qw + 6 * kvw)

    def tail(col):
        return new_f[:, :, col:col + kvw].reshape(n_seq, dec_seq * N_KV, HEAD_DIM)

    o_s, kw_next, vw_next = _attn_sample(page_table, qkv_f[seq:, :qw], gates_s,
                       kcmp_s.reshape(-1, HEAD_DIM), vcmp_s.reshape(-1, HEAD_DIM),
                       _token_group_rows(state_k_win), _token_group_rows(state_v_win),
                       tail(qw + 2 * kvw), tail(qw + 3 * kvw), tail(qw + 4 * kvw), tail(qw + 5 * kvw),
                       img_s, ct_s, _token_group_rows(cache_k_sel), _token_group_rows(cache_v_sel), dec_seq)
    o_all = jnp.concatenate([o_p, o_s.astype(BF16)], axis=0)

    mix = _merge(x_bf, z_conv, o_all, w_br_conv[0].astype(BF16), w_br_nsa[0].astype(BF16), w_mix, 512, 256)
    r1 = _resid_mm_split(mix, w_out[0].astype(BF16), x_prompt[0], x_s2, tm, 512, 48, "out_proj")
    h_f, h_b = _layer_norm(r1, ln1_g, ln1_b, 256, "ln1")
    f1, f2 = _history_rows(state_ffn_conv[0], dec_seq)
    tn_f = 256
    act, gp = _gconv(h_b, w_ffn_in[0], (D_FF // tn_f, 0), ffn_conv_w[0], ffn_conv_b, f1, f2,
                     seq, tm, tn_f, 56, "ffn_in")
    r2 = _resid_mm(act, w_ffn_out[0].astype(BF16), h_f, 512, 256, 56, "ffn_out")
    y_p, y_s = _layer_norm_split(r2, ln2_g, ln2_b, seq, 256, "ln2")

    def rows_p(col):
        return qkv_f[:seq, col:col + kvw].reshape(1, 1, seq, N_KV, HEAD_DIM)

    def rows_s(col):
        return new_f[:, :, col:col + kvw].reshape(1, n_seq, dec_seq, N_KV, HEAD_DIM)

    def win_p(col):
        return qkv_f[seq - WINDOW:seq, col:col + kvw].reshape(1, 1, WINDOW, N_KV, HEAD_DIM)


    def last2(u):
        c = u.shape[1]
        return (u[seq - 2:seq].reshape(1, 1, 2, c),
                u[seq:].reshape(n_seq, dec_seq, c)[:, dec_seq - 2:].reshape(1, n_seq, 2, c))

    conv_p, conv_s = last2(u_conv)
    ffn_p, ffn_s = last2(gp)
    c = qw
    return (y_p.reshape(1, seq, D_MODEL), y_s.reshape(n_seq, dec_seq, D_MODEL),
            rows_p(c), rows_s(c), rows_p(c + kvw), rows_s(c + kvw),
            rows_p(c + 2 * kvw), rows_s(c + 2 * kvw), rows_p(c + 3 * kvw), rows_s(c + 3 * kvw),
            win_p(c + 4 * kvw), kw_next.reshape(state_k_win.shape),
            win_p(c + 5 * kvw), vw_next.reshape(state_v_win.shape),
            conv_p, conv_s, ffn_p, ffn_s)
```

```python
import functools
import math

import jax
import jax.numpy as jnp
from jax import lax
from jax.experimental import pallas as pl
from jax.experimental.pallas import tpu as pltpu

F32 = jnp.float32
BF16 = jnp.bfloat16

D_MODEL = 4096
D_CONV = 2048
N_HEADS = 16
HEAD_DIM = 128
N_KV = 4
HPG = 4
CMP_STRIDE = 16
CMP_HID = 256
SEL_BLOCK = 64
N_SEL = 16
WINDOW = 512
FORCE_SCORE = 1e4
REL_BUCKETS = 32
REL_EXACT = 16
REL_MAX_DIST = 128
D_FF = 11008
PAGE = 128
ALPHA = 2.0 ** 0.25
LN_EPS = 1e-5
LOG2E = 1.0 / math.log(2.0)
Q_SCALE = HEAD_DIM ** -0.5 * LOG2E

KEY_TILE = 128
SEL_GROUP = 4
GCONV_PARTS = 4
SEL_STATES = 3
PAGE_ROWS = PAGE * N_KV
NEG = -1e30
MIB = 1024 * 1024

IMG_DIAG, IMG_PREV, IMG_ZERO, IMG_EDGE, IMG_NEG = 0, 1, 2, 3, 4
N_IMG = 5


def _params(sem, vmem_mib):
    return pltpu.CompilerParams(dimension_semantics=sem, vmem_limit_bytes=vmem_mib * MIB)


def _dot(a, b):
    return jnp.dot(a, b, preferred_element_type=F32)


def _proj_kernel(x_ref, w_ref, of_ref, ob_ref):
    acc = _dot(x_ref[...], w_ref[...].astype(BF16))
    of_ref[...] = acc
    ob_ref[...] = acc.astype(BF16)


def _proj(x, w, col0, ncols, tm, tn):
    m_rows, k = x.shape
    c0 = col0 // tn
    return pl.pallas_call(
        _proj_kernel,
        grid=(m_rows // tm, ncols // tn),
        in_specs=[pl.BlockSpec((tm, k), lambda m, n: (m, 0)),
                  pl.BlockSpec((k, tn), lambda m, n: (0, c0 + n))],
        out_specs=[pl.BlockSpec((tm, tn), lambda m, n: (m, n)),
                   pl.BlockSpec((tm, tn), lambda m, n: (m, n))],
        out_shape=[jax.ShapeDtypeStruct((m_rows, ncols), F32),
                   jax.ShapeDtypeStruct((m_rows, ncols), BF16)],
        compiler_params=_params(("arbitrary", "arbitrary"), 56),
        name="proj_qkv",
    )(x, w)


def _gate_kernel(x_ref, w_ref, o_ref):
    o_ref[...] = jax.nn.sigmoid(_dot(x_ref[...], w_ref[...]))


def _gate_proj(x, w, tm):
    m_rows, k = x.shape
    n = w.shape[1]
    return pl.pallas_call(
        _gate_kernel,
        grid=(m_rows // tm,),
        in_specs=[pl.BlockSpec((tm, k), lambda m: (m, 0)),
                  pl.BlockSpec((k, n), lambda m: (0, 0))],
        out_specs=pl.BlockSpec((tm, n), lambda m: (m, 0)),
        out_shape=jax.ShapeDtypeStruct((m_rows, n), F32),
        compiler_params=_params(("arbitrary",), 40),
        name="proj_gates",
    )(x, w)


def _gconv_kernel(*refs, three, n_prompt_tiles):
    if three:
        x_ref, w0_ref, w1_ref, w2_ref, cw_ref, cb_ref, s1_ref, s2_ref, z_ref, u_ref, carry_ref = refs
    else:
        x_ref, w0_ref, w1_ref, cw_ref, cb_ref, s1_ref, s2_ref, z_ref, u_ref, carry_ref = refs
    m = pl.program_id(0)
    n = pl.program_id(1)
    is_sample = m >= n_prompt_tiles
    tm, tn = u_ref.shape
    hm = tm // GCONV_PARTS
    cw = cw_ref[...]
    cb = cb_ref[...]
    c = carry_ref[n]
    c = jnp.where(m == 0, jnp.zeros_like(c), c)
    prev_a, prev_b = c[6:7], c[7:8]
    row = lax.broadcasted_iota(jnp.int32, (hm, tn), 0)
    rmask = row & jnp.where(is_sample, 7, hm - 1)
    w0 = w0_ref[...].astype(BF16)
    w1 = w1_ref[...].astype(BF16)
    w2 = w2_ref[...].astype(BF16) if three else None

    def project(part):
        x = x_ref[part * hm:(part + 1) * hm, :]
        p0 = _dot(x, w0)
        u = _dot(x, w1)
        if three:
            u = u * _dot(x, w2)
        return p0, u

    def epilogue(part, p0, u, prev_a, prev_b):
        rows = slice(part * hm, (part + 1) * hm)
        u_ref[rows, :] = u
        fill1 = jnp.where(is_sample, s1_ref[rows, :], prev_b)
        fill2 = jnp.where(is_sample, s2_ref[rows, :], jnp.where(row == 0, prev_a, prev_b))
        prev1 = jnp.where(rmask == 0, fill1, pltpu.roll(u, 1, 0))
        prev2 = jnp.where(rmask < 2, fill2, pltpu.roll(u, 2, 0))
        conv = cb + prev2 * cw[0:1] + prev1 * cw[1:2] + u * cw[2:3]
        if three:
            z = p0 * conv
        else:
            z = jax.nn.silu(conv) * p0
        z_ref[rows, :] = z.astype(z_ref.dtype)
        return u[hm - 2:hm - 1], u[hm - 1:hm]

    done = project(0)
    for part in range(1, GCONV_PARTS):
        nxt = project(part)
        prev_a, prev_b = epilogue(part - 1, *done, prev_a, prev_b)
        done = nxt
    epilogue(GCONV_PARTS - 1, *done, prev_a, prev_b)
    carry_ref[n] = done[1][hm - 8:hm]


def _gconv(x, w, col_blocks, cw, cb, s1, s2, n_prompt_rows, tm, tn, vmem_mib, name):
    m_rows, k = x.shape
    ncols = cw.shape[1]
    three = len(col_blocks) == 3
    assert m_rows - n_prompt_rows == tm and n_prompt_rows % tm == 0
    nt = ncols // tn
    w_specs = [pl.BlockSpec((k, tn), functools.partial(lambda m, n, c: (0, c + n), c=c)) for c in col_blocks]
    col_spec = pl.BlockSpec((tm, tn), lambda m, n: (m, n))
    return pl.pallas_call(
        functools.partial(_gconv_kernel, three=three, n_prompt_tiles=n_prompt_rows // tm),
        grid=(m_rows // tm, nt),
        in_specs=[pl.BlockSpec((tm, k), lambda m, n: (m, 0))] + w_specs + [
            pl.BlockSpec((3, tn), lambda m, n: (0, n)),
            pl.BlockSpec((1, tn), lambda m, n: (0, n)),
            pl.BlockSpec((tm, tn), lambda m, n: (0, n)),
            pl.BlockSpec((tm, tn), lambda m, n: (0, n))],
        out_specs=[col_spec, col_spec],
        out_shape=[jax.ShapeDtypeStruct((m_rows, ncols), BF16),
                   jax.ShapeDtypeStruct((m_rows, ncols), F32)],
        scratch_shapes=[pltpu.VMEM((nt, 8, tn), F32)],
        compiler_params=_params(("arbitrary", "arbitrary"), vmem_mib),
        name=name,
    )(x, *([w] * len(col_blocks)), cw, cb, s1, s2)


def _merge_kernel(x_ref, z_ref, o_ref, wc_ref, wn_ref, wga_ref, wgb_ref, out_ref):
    x = x_ref[...]
    a = _dot(z_ref[...], wc_ref[...].astype(BF16))
    b = _dot(o_ref[...], wn_ref[...].astype(BF16))
    ga = jax.nn.sigmoid(_dot(x, wga_ref[...]))
    gb = jax.nn.sigmoid(_dot(x, wgb_ref[...]))
    out_ref[...] = (ga * a + gb * b).astype(BF16)


def _merge(x, z, o, wc, wn, wg, tm, tn):
    m_rows, k = x.shape
    kc = z.shape[1]
    nt = D_MODEL // tn
    return pl.pallas_call(
        _merge_kernel,
        grid=(m_rows // tm, nt),
        in_specs=[pl.BlockSpec((tm, k), lambda m, n: (m, 0)),
                  pl.BlockSpec((tm, kc), lambda m, n: (m, 0)),
                  pl.BlockSpec((tm, kc), lambda m, n: (m, 0)),
                  pl.BlockSpec((kc, tn), lambda m, n: (0, n)),
                  pl.BlockSpec((kc, tn), lambda m, n: (0, n)),
                  pl.BlockSpec((k, tn), lambda m, n: (0, n)),
                  pl.BlockSpec((k, tn), lambda m, n: (0, nt + n))],
        out_specs=pl.BlockSpec((tm, tn), lambda m, n: (m, n)),
        out_shape=jax.ShapeDtypeStruct((m_rows, D_MODEL), BF16),
        compiler_params=_params(("arbitrary", "arbitrary"), 56),
        name="merge_branches",
    )(x, z, o, wc, wn, wg, wg)


def _resid_kernel(l_ref, w_ref, r_ref, o_ref):
    o_ref[...] = ALPHA * r_ref[...] + _dot(l_ref[...], w_ref[...])


def _resid2_kernel(l_ref, w_ref, rp_ref, rs_ref, o_ref, *, n_prompt_tiles):
    res = jnp.where(pl.program_id(0) < n_prompt_tiles, rp_ref[...], rs_ref[...])
    o_ref[...] = ALPHA * res + _dot(l_ref[...], w_ref[...].astype(BF16))


def _resid_mm_split(lhs, w, res_p, res_s, tm, tn, vmem_mib, name):
    m_rows, k = lhs.shape
    n_cols = w.shape[1]
    npt = res_p.shape[0] // tm
    assert res_s.shape[0] == tm and npt * tm + tm == m_rows
    return pl.pallas_call(
        functools.partial(_resid2_kernel, n_prompt_tiles=npt),
        grid=(m_rows // tm, n_cols // tn),
        in_specs=[pl.BlockSpec((tm, k), lambda m, n: (m, 0)),
                  pl.BlockSpec((k, tn), lambda m, n: (0, n)),
                  pl.BlockSpec((tm, tn), lambda m, n: (jnp.minimum(m, npt - 1), n)),
                  pl.BlockSpec((tm, tn), lambda m, n: (0, n))],
        out_specs=pl.BlockSpec((tm, tn), lambda m, n: (m, n)),
        out_shape=jax.ShapeDtypeStruct((m_rows, n_cols), F32),
        compiler_params=_params(("arbitrary", "arbitrary"), vmem_mib),
        name=name,
    )(lhs, w, res_p, res_s)


def _resid_mm(lhs, w, res, tm, tn, vmem_mib, name):
    m_rows, k = lhs.shape
    n_cols = w.shape[1]
    return pl.pallas_call(
        _resid_kernel,
        grid=(m_rows // tm, n_cols // tn),
        in_specs=[pl.BlockSpec((tm, k), lambda m, n: (m, 0)),
                  pl.BlockSpec((k, tn), lambda m, n: (0, n)),
                  pl.BlockSpec((tm, tn), lambda m, n: (m, n))],
        out_specs=pl.BlockSpec((tm, tn), lambda m, n: (m, n)),
        out_shape=jax.ShapeDtypeStruct((m_rows, n_cols), F32),
        compiler_params=_params(("arbitrary", "arbitrary"), vmem_mib),
        name=name,
    )(lhs, w, res)


def _ln_kernel(x_ref, g_ref, b_ref, of_ref, ob_ref):
    x = x_ref[...]
    mu = jnp.mean(x, axis=-1, keepdims=True)
    xc = x - mu
    var = jnp.mean(xc * xc, axis=-1, keepdims=True)
    y = xc * lax.rsqrt(var + LN_EPS) * g_ref[...] + b_ref[...]
    of_ref[...] = y
    ob_ref[...] = y.astype(BF16)


def _layer_norm(x, g, b, tr, name):
    m_rows, d = x.shape
    row_spec = pl.BlockSpec((tr, d), lambda m: (m, 0))
    vec_spec = pl.BlockSpec((1, d), lambda m: (0, 0))
    return pl.pallas_call(
        _ln_kernel,
        grid=(m_rows // tr,),
        in_specs=[row_spec, vec_spec, vec_spec],
        out_specs=[row_spec, row_spec],
        out_shape=[jax.ShapeDtypeStruct((m_rows, d), F32), jax.ShapeDtypeStruct((m_rows, d), BF16)],
        compiler_params=_params(("arbitrary",), 48),
        name=name,
    )(x, g, b)


def _ln_split_kernel(x_ref, g_ref, b_ref, yp_ref, ys_ref, *, n_prompt_tiles):
    x = x_ref[...]
    mu = jnp.mean(x, axis=-1, keepdims=True)
    xc = x - mu
    var = jnp.mean(xc * xc, axis=-1, keepdims=True)
    y = xc * lax.rsqrt(var + LN_EPS) * g_ref[...] + b_ref[...]
    m = pl.program_id(0)

    @pl.when(m < n_prompt_tiles)
    def _():
        yp_ref[...] = y

    @pl.when(m >= n_prompt_tiles)
    def _():
        ys_ref[...] = y


def _layer_norm_split(x, g, b, n_prompt_rows, tr, name):
    m_rows, d = x.shape
    npt = n_prompt_rows // tr
    vec_spec = pl.BlockSpec((1, d), lambda m: (0, 0))
    return pl.pallas_call(
        functools.partial(_ln_split_kernel, n_prompt_tiles=npt),
        grid=(m_rows // tr,),
        in_specs=[pl.BlockSpec((tr, d), lambda m: (m, 0)), vec_spec, vec_spec],
        out_specs=[pl.BlockSpec((tr, d), lambda m: (jnp.minimum(m, npt - 1), 0)),
                   pl.BlockSpec((tr, d), lambda m: (jnp.maximum(m - npt, 0), 0))],
        out_shape=[jax.ShapeDtypeStruct((n_prompt_rows, d), F32),
                   jax.ShapeDtypeStruct((m_rows - n_prompt_rows, d), F32)],
        compiler_params=_params(("arbitrary",), 48),
        name=name,
    )(x, g, b)


def _page_copies(pt_ref, seq, cache_ref, buf, slot, sem):
    return [pltpu.make_async_copy(
        cache_ref.at[pl.ds(pl.multiple_of(pt_ref[seq, j] * PAGE_ROWS, PAGE_ROWS), PAGE_ROWS)],
        buf.at[slot, pl.ds(j * PAGE_ROWS, PAGE_ROWS)], sem)
        for j in range(pt_ref.shape[1])]


def _prefetch_pages(pt_ref, b, n_seq, streams):
    slot = b % 2

    @pl.when(b == 0)
    def _():
        for cache_ref, buf, sem in streams:
            for cp in _page_copies(pt_ref, 0, cache_ref, buf, 0, sem.at[0]):
                cp.start()

    @pl.when(b + 1 < n_seq)
    def _():
        for cache_ref, buf, sem in streams:
            for cp in _page_copies(pt_ref, b + 1, cache_ref, buf, 1 - slot, sem.at[1 - slot]):
                cp.start()

    for cache_ref, buf, sem in streams:
        for cp in _page_copies(pt_ref, b, cache_ref, buf, slot, sem.at[slot]):
            cp.wait()
    return slot


def _compress_rows(xg, pe, w1, w2, next_chunk=1):
    return _compress_finish(_compress_first_layer(xg, pe, w1), w2, next_chunk)


def _compress_first_layer(xg, pe, w1):
    half = CMP_STRIDE * HEAD_DIM
    pa = _dot(jnp.broadcast_to(pe[0:1], (8, half)).astype(BF16), w1[:, :CMP_HID])[0:1]
    pb = _dot(jnp.broadcast_to(pe[1:2], (8, half)).astype(BF16), w1[:, CMP_HID:])[0:1]
    fs = _dot(xg, w1)
    return fs[:, :CMP_HID] + pa, fs[:, CMP_HID:] + pb


def _compress_finish(halves, w2, next_chunk):
    first, second = halves
    hid = jax.nn.gelu(first + pltpu.roll(second, second.shape[0] - next_chunk, 0))
    return _dot(hid.astype(BF16), w2)


def _compress_kernel(x_ref, pe_ref, w1_ref, w2_ref, o_ref):
    row_w = N_KV * HEAD_DIM
    for g in range(N_KV):
        xg = jnp.concatenate(
            [x_ref[:, r * row_w + g * HEAD_DIM: r * row_w + (g + 1) * HEAD_DIM] for r in range(CMP_STRIDE)],
            axis=1).astype(BF16)
        o_ref[:, g * HEAD_DIM:(g + 1) * HEAD_DIM] = _compress_rows(
            xg, pe_ref[...], w1_ref[...], w2_ref[...]).astype(BF16)


def _compress(x2, pe2, w1cat, w2, name):
    rows, width = x2.shape
    return pl.pallas_call(
        _compress_kernel,
        grid=(1,),
        in_specs=[pl.BlockSpec((rows, width), lambda t: (0, 0)),
                  pl.BlockSpec(pe2.shape, lambda t: (0, 0)),
                  pl.BlockSpec(w1cat.shape, lambda t: (0, 0)),
                  pl.BlockSpec(w2.shape, lambda t: (0, 0))],
        out_specs=pl.BlockSpec((rows, N_KV * HEAD_DIM), lambda t: (0, 0)),
        out_shape=jax.ShapeDtypeStruct((rows, N_KV * HEAD_DIM), BF16),
        compiler_params=_params(("arbitrary",), 48),
        name=name,
    )(x2, pe2, w1cat, w2)


def _compress_paged_kernel(pt_ref, kc_hbm, vc_hbm, pek_ref, w1k_ref, w2k_ref, pev_ref, w1v_ref, w2v_ref,
                           ok_ref, ov_ref, kbuf, vbuf, sem, *, chunks):
    b = pl.program_id(0)
    slot = _prefetch_pages(pt_ref, b, pl.num_programs(0), [(kc_hbm, kbuf, sem.at[0]), (vc_hbm, vbuf, sem.at[1])])

    def chunk_rows(buf):
        x3 = buf[slot].reshape(chunks, CMP_STRIDE * N_KV, HEAD_DIM)
        return jnp.concatenate(
            [x3[:, r * N_KV:(r + 1) * N_KV, :].reshape(chunks * N_KV, HEAD_DIM) for r in range(CMP_STRIDE)],
            axis=1).astype(BF16)

    hk = _compress_first_layer(chunk_rows(kbuf), pek_ref[...], w1k_ref[...])
    hv = _compress_first_layer(chunk_rows(vbuf), pev_ref[...], w1v_ref[...])
    ok_ref[...] = _compress_finish(hk, w2k_ref[...], N_KV).astype(BF16)
    ov_ref[...] = _compress_finish(hv, w2v_ref[...], N_KV).astype(BF16)


def _compress_paged(page_table, kc_cache, vc_cache, wk, wv):
    n_seq, n_pages = page_table.shape
    chunks = n_pages * PAGE // CMP_STRIDE
    w_specs = [pl.BlockSpec(w.shape, lambda b, pt: (0, 0)) for w in wk + wv]
    out_spec = pl.BlockSpec((chunks * N_KV, HEAD_DIM), lambda b, pt: (b, 0))
    out_shape = jax.ShapeDtypeStruct((n_seq * chunks * N_KV, HEAD_DIM), BF16)
    buf = pltpu.VMEM((2, n_pages * PAGE_ROWS, HEAD_DIM), F32)
    return pl.pallas_call(
        functools.partial(_compress_paged_kernel, chunks=chunks),
        grid_spec=pltpu.PrefetchScalarGridSpec(
            num_scalar_prefetch=1, grid=(n_seq,),
            in_specs=[pl.BlockSpec(memory_space=pl.ANY), pl.BlockSpec(memory_space=pl.ANY)] + w_specs,
            out_specs=[out_spec, out_spec],
            scratch_shapes=[buf, buf, pltpu.SemaphoreType.DMA((2, 2))]),
        out_shape=[out_shape, out_shape],
        compiler_params=_params(("arbitrary",), 48),
        name="compress_sample",
    )(page_table, kc_cache, vc_cache, *wk, *wv)


def _bias_kernel(rb_ref, img_ref, ct_ref, *, heads, qbp, rpk, off2):
    g = pl.program_id(0)
    r_l = heads * qbp
    lane = lax.broadcasted_iota(jnp.int32, (1, r_l), 1)
    n_l = lane // qbp
    tbl = []
    for k in range(REL_BUCKETS):
        row = jnp.zeros((1, r_l), F32)
        for n in range(heads):
            row = jnp.where(n_l == n, rb_ref[k, g * heads + n], row)
        tbl.append(row)

    def bias_of(dist):
        nn = jnp.maximum(dist, 0)
        nf = jnp.maximum(nn, 1).astype(F32)
        large = REL_EXACT + (jnp.log(nf / REL_EXACT) / math.log(REL_MAX_DIST / REL_EXACT)
                             * (REL_BUCKETS - REL_EXACT)).astype(jnp.int32)
        bucket = jnp.where(nn < REL_EXACT, nn, jnp.minimum(large, REL_BUCKETS - 1))
        val = jnp.zeros(dist.shape, F32)
        for k in range(REL_BUCKETS):
            val = jnp.where(bucket == k, tbl[k], val)
        return val

    shape = (KEY_TILE, r_l)
    kj = lax.broadcasted_iota(jnp.int32, shape, 0) // rpk
    qi = lax.broadcasted_iota(jnp.int32, shape, 1) % qbp
    far = tbl[REL_BUCKETS - 1]
    for r0 in range(0, rpk * KEY_TILE, KEY_TILE):
        rows = slice(r0, r0 + KEY_TILE)
        d0 = qi - (kj + r0 // rpk)
        img_ref[IMG_DIAG, rows] = jnp.where(d0 >= 0, (bias_of(d0) - far) * LOG2E, NEG)
        img_ref[IMG_PREV, rows] = (bias_of(d0 + KEY_TILE) - far) * LOG2E
        img_ref[IMG_ZERO, rows] = jnp.zeros(shape, F32)
        img_ref[IMG_EDGE, rows] = jnp.where(d0 < 0, 0.0, NEG)
        img_ref[IMG_NEG, rows] = jnp.full(shape, NEG, F32)
    for r0 in range(0, ct_ref.shape[0], KEY_TILE):
        dist = qi + CMP_STRIDE * (off2 - r0 // rpk - kj) - (2 * CMP_STRIDE - 1)
        ct_ref[r0:r0 + KEY_TILE, :] = jnp.where(dist >= 0, bias_of(dist) * LOG2E, NEG)


def _bias_images(rel_bias, heads, qbp, rpk, nct, off2, name):
    r_l = heads * qbp
    groups = N_HEADS // heads
    return pl.pallas_call(
        functools.partial(_bias_kernel, heads=heads, qbp=qbp, rpk=rpk, off2=off2),
        grid=(groups,),
        in_specs=[pl.BlockSpec(memory_space=pltpu.SMEM)],
        out_specs=[pl.BlockSpec((None, N_IMG, rpk * KEY_TILE, r_l), lambda g: (g, 0, 0, 0)),
                   pl.BlockSpec((None, rpk * nct, r_l), lambda g: (g, 0, 0))],
        out_shape=[jax.ShapeDtypeStruct((groups, N_IMG, rpk * KEY_TILE, r_l), F32),
                   jax.ShapeDtypeStruct((groups, rpk * nct, r_l), F32)],
        compiler_params=_params(("arbitrary",), 32),
        name=name,
    )(rel_bias)


def _pv(v, p):
    return lax.dot_general(v, p.astype(BF16), (((0,), (0,)), ((), ())), preferred_element_type=F32)


def _online_update(carry, st, v):
    m_run, l_run, acc = carry
    m_new = jnp.maximum(m_run, jnp.max(st, axis=0, keepdims=True))
    alpha = jnp.exp2(m_run - m_new)
    p = jnp.exp2(st - m_new)
    l_new = alpha * l_run + jnp.sum(p, axis=0, keepdims=True)
    return m_new, l_new, alpha * acc + _pv(v, p)


def _softmax_init(r_l):
    return jnp.full((1, r_l), NEG, F32), jnp.zeros((1, r_l), F32), jnp.zeros((HEAD_DIM, r_l), F32)


def _block_onehot(n_keys):
    kj = lax.broadcasted_iota(jnp.int32, (n_keys, KEY_TILE), 0)
    col = lax.broadcasted_iota(jnp.int32, (n_keys, KEY_TILE), 1)
    return jnp.where(col == kj // SEL_BLOCK, 1.0, 0.0).astype(BF16)


def _query_lanes(q):
    q = q * Q_SCALE
    qg = jnp.concatenate([q[:, n * HEAD_DIM:(n + 1) * HEAD_DIM] for n in range(HPG)], axis=0)
    return qg.T.astype(BF16)


def _cmp_and_select(q_t, kc, vc, cbias, t0, ps_ref, *, qbp, nc, nb, side_work=None):
    r_l = HPG * qbp
    s = _dot(kc, q_t) + cbias
    valid = cbias > 0.5 * NEG
    mx = jnp.max(s, axis=0, keepdims=True)
    mx = jnp.where(mx > 0.5 * NEG, mx, 0.0)
    e = jnp.where(valid, jnp.exp2(s - mx), 0.0)
    p_c = e / jnp.maximum(jnp.sum(e, axis=0, keepdims=True), 1e-30)
    o_c = _pv(vc, p_c)

    if qbp == KEY_TILE:
        psum = p_c[:, 0:qbp]
        for n in range(1, HPG):
            psum = psum + p_c[:, n * qbp:(n + 1) * qbp]
    else:
        psum = p_c
        for n in range(1, HPG):
            psum = psum + pltpu.roll(p_c, n * qbp, 1)
    ps_ref[...] = psum
    per_sel = SEL_BLOCK // CMP_STRIDE
    nbi = nc // per_sel
    imp = ps_ref[pl.ds(0, nbi, stride=per_sel), :]
    for c in range(1, per_sel):
        imp = imp + ps_ref[pl.ds(c, nbi, stride=per_sel), :]
    if nb > nbi:
        imp = jnp.concatenate([imp, jnp.zeros((nb - nbi, KEY_TILE), F32)], axis=0)
    jb = lax.broadcasted_iota(jnp.int32, (nb, KEY_TILE), 0)
    tq = t0 + lax.broadcasted_iota(jnp.int32, (nb, KEY_TILE), 1) % qbp
    cur = tq // SEL_BLOCK
    forced = (jb == 0) | (jb == cur) | (jb == cur - 1)
    score = jnp.where(jb <= cur, jnp.where(forced, FORCE_SCORE, imp), NEG)
    jbf = jb.astype(F32)
    side = side_work() if side_work is not None else None
    selm = jnp.full((nb, KEY_TILE), NEG, F32)
    for _ in range(N_SEL):
        best = jnp.max(score, axis=0, keepdims=True)
        first = jnp.min(jnp.where(score == best, jbf, 1e9), axis=0, keepdims=True)
        pick = jbf == first
        selm = jnp.where(pick, 0.0, selm)
        score = jnp.where(pick, 3.0 * NEG, score)
    if r_l > KEY_TILE:
        selm = jnp.concatenate([selm] * (r_l // KEY_TILE), axis=1)
    return o_c, selm, side


def _combine(gt, o_c, sel, win, qbp):
    (_, l_s, acc_s), (_, l_w, acc_w) = sel, win
    o_t = gt[0:1] * o_c + gt[1:2] * (acc_s / l_s) + gt[2:3] * (acc_w / l_w)
    return o_t.T


def _attn_prompt_kernel(q_ref, gt_ref, kc_ref, vc_ref, ks_ref, vs_ref, kw_ref, vw_ref, img_ref, ct_ref,
                        o_ref, selg_ref, ps_ref, *, nc, nb, off2):
    qbp = KEY_TILE
    r_l = HPG * qbp
    i_tile = pl.program_id(1)
    t0 = i_tile * KEY_TILE
    q_t = _query_lanes(q_ref[...])
    n_win = WINDOW // KEY_TILE

    def img_of(d, edge):
        idx = jnp.minimum(d, IMG_ZERO)
        if edge:
            idx = jnp.where(d == n_win, IMG_EDGE, idx)
        return img_ref[jnp.where(d < 0, IMG_NEG, idx)]

    def window():
        first_tile = jnp.maximum(i_tile - n_win, 0)
        rows = pl.ds(pl.multiple_of(first_tile * KEY_TILE, KEY_TILE), (n_win + 1) * KEY_TILE)
        add = jnp.concatenate([img_of(i_tile - (first_tile + t), True) for t in range(n_win + 1)], axis=0)
        return _online_update(_softmax_init(r_l), _dot(kw_ref[rows, :], q_t) + add, vw_ref[rows, :])

    start = pl.multiple_of(off2 - i_tile * (KEY_TILE // CMP_STRIDE), 8)
    o_c, selm, win = _cmp_and_select(q_t, kc_ref[...], vc_ref[...], ct_ref[pl.ds(start, nc), :], t0, ps_ref,
                                     qbp=qbp, nc=nc, nb=nb, side_work=window)

    grp_keys = SEL_GROUP * KEY_TILE
    grp_blocks = grp_keys // SEL_BLOCK
    n_grp, pad_rows = selg_ref.shape[0], selg_ref.shape[1] - grp_blocks
    selg_ref[...] = jnp.concatenate(
        [selm.reshape(n_grp, grp_blocks, r_l), jnp.zeros((n_grp, pad_rows, r_l), F32)], axis=1).astype(BF16)
    onehot = _block_onehot(grp_keys)
    zpad = jnp.zeros((KEY_TILE - selg_ref.shape[1], r_l), BF16)

    def grp_rows(it):
        return pl.ds(pl.multiple_of(it * grp_keys, grp_keys), grp_keys)

    n_it = (i_tile + SEL_GROUP) // SEL_GROUP
    n_far = jnp.maximum(i_tile - 1, 0) // SEL_GROUP


    def scores(it):
        k_aug = jnp.concatenate([ks_ref[grp_rows(it), :], onehot], axis=1)
        q_aug = jnp.concatenate([q_t, selg_ref[it], zpad], axis=0)
        return _dot(k_aug, q_aug)

    def multi_body(itn, states):
        sts = [scores(SEL_STATES * itn + k) for k in range(SEL_STATES)]
        return tuple(_online_update(states[k], sts[k], vs_ref[grp_rows(SEL_STATES * itn + k), :])
                     for k in range(SEL_STATES))

    def single_body(it, state):
        st = scores(it) + jnp.concatenate(
            [img_of(i_tile - (it * SEL_GROUP + t), False) for t in range(SEL_GROUP)], axis=0)
        return _online_update(state, st, vs_ref[grp_rows(it), :])

    n_multi = n_far // SEL_STATES
    states = lax.fori_loop(0, n_multi, multi_body, tuple(_softmax_init(r_l) for _ in range(SEL_STATES)))
    states = (lax.fori_loop(SEL_STATES * n_multi, n_it, single_body, states[0]),) + states[1:]
    m_all = states[0][0]
    for st_k in states[1:]:
        m_all = jnp.maximum(m_all, st_k[0])
    l_all, acc_all = None, None
    for m_k, l_k, acc_k in states:
        f_k = jnp.exp2(m_k - m_all)
        l_all = f_k * l_k if l_all is None else l_all + f_k * l_k
        acc_all = f_k * acc_k if acc_all is None else acc_all + f_k * acc_k
    sel = (m_all, l_all, acc_all)

    o_r = _combine(gt_ref[...], o_c, sel, win, qbp)
    for n in range(HPG):
        o_ref[:, n * HEAD_DIM:(n + 1) * HEAD_DIM] = o_r[n * qbp:(n + 1) * qbp].astype(o_ref.dtype)


SAMPLE_GROUP_COL = 64
SAMPLE_FAR_PARTS = 6


def _softmax_segments(segs):
    m = jnp.max(segs[0][0], axis=0, keepdims=True)
    for st, _ in segs[1:]:
        m = jnp.maximum(m, jnp.max(st, axis=0, keepdims=True))
    l_sum, acc = None, None
    for st, v in segs:
        p = jnp.exp2(st - m)
        l_part, a_part = jnp.sum(p, axis=0, keepdims=True), _pv(v, p)
        l_sum = l_part if l_sum is None else l_sum + l_part
        acc = a_part if acc is None else acc + a_part
    return m, l_sum, acc


def _attn_sample_kernel(pt_ref, q_ref, gt_ref, kc_ref, vc_ref, kw_ref, vw_ref, kst_ref, vst_ref, kwt_ref, vwt_ref,
                        img_ref, ct_ref, feat_ref, ks_hbm, vs_hbm, o_ref, kwo_ref, vwo_ref, selm_ref, kbuf, vbuf,
                        sem, *, qb, nb):
    b = pl.program_id(0)
    slot = _prefetch_pages(pt_ref, b, pl.num_programs(0), [(ks_hbm, kbuf, sem.at[0]), (vs_hbm, vbuf, sem.at[1])])
    lanes = N_HEADS * qb
    past_rows = kbuf.shape[1]
    past = past_rows // N_KV
    tile_rows = KEY_TILE * N_KV
    tail_rows = kst_ref.shape[0]
    win_rows = kw_ref.shape[0]
    cmp_rows = kc_ref.shape[0]

    q = q_ref[...] * Q_SCALE
    q_t = jnp.concatenate([q[:, h * HEAD_DIM:(h + 1) * HEAD_DIM] for h in range(N_HEADS)], axis=0).T.astype(BF16)
    lane = lax.broadcasted_iota(jnp.int32, (N_KV, lanes), 1)
    grp_rows = jnp.where(lane // (HPG * qb) == lax.broadcasted_iota(jnp.int32, (N_KV, lanes), 0), 0.0, NEG)

    q_aug = jnp.concatenate(
        [q_t, jnp.concatenate([jnp.zeros((SAMPLE_GROUP_COL, lanes), F32), grp_rows,
                               jnp.zeros((KEY_TILE - SAMPLE_GROUP_COL - N_KV, lanes), F32)], axis=0).astype(BF16)],
        axis=0)

    def scores(k_rows, feat):
        return _dot(jnp.concatenate([k_rows.astype(BF16), feat], axis=1), q_aug)

    far = past_rows - tile_rows
    part = far // SAMPLE_FAR_PARTS
    half = win_rows // 2
    sel_rows = [slice(r0, r0 + part) for r0 in range(0, far, part)] + [slice(far, past_rows)]
    pending = [lambda rows=rows: scores(kbuf[slot, rows, :], feat_ref[rows, :]) for rows in sel_rows]
    pending += [lambda: scores(kst_ref[...], feat_ref[0:tail_rows, :]),
                lambda: scores(kw_ref[0:half, :], feat_ref[0:half, :]),
                lambda: scores(kw_ref[half:win_rows, :], feat_ref[half:win_rows, :]),
                lambda: scores(kwt_ref[...], feat_ref[0:tail_rows, :])]
    raw = []

    def issue(n=1):
        for _ in range(n):
            if len(raw) < len(pending):
                raw.append(pending[len(raw)]())

    s = scores(kc_ref[...], feat_ref[0:cmp_rows, :]) + ct_ref[...]
    issue()
    mx = jnp.max(s, axis=0, keepdims=True)
    mx = jnp.where(mx > 0.5 * NEG, mx, 0.0)
    e = jnp.where(s > 0.5 * NEG, jnp.exp2(s - mx), 0.0)
    p_c = e / jnp.maximum(jnp.sum(e, axis=0, keepdims=True), 1e-30)
    o_c = _pv(vc_ref[...], p_c)
    issue()

    l32 = lax.broadcasted_iota(jnp.int32, p_c.shape, 1) % (HPG * qb)
    psum = p_c
    for n in range(1, HPG):
        sh = n * qb
        psum = psum + jnp.where(l32 >= sh, pltpu.roll(p_c, sh, 1), pltpu.roll(p_c, sh + lanes - HPG * qb, 1))
    rows_per_block = SEL_BLOCK // CMP_STRIDE * N_KV
    nbi = cmp_rows // rows_per_block
    imp = jnp.sum(psum.reshape(nbi, rows_per_block, lanes), axis=1)
    imp = jnp.concatenate([imp, jnp.zeros((nb - nbi, lanes), F32)], axis=0)
    jb = lax.broadcasted_iota(jnp.int32, (nb, lanes), 0)
    cur = (past + lax.broadcasted_iota(jnp.int32, (nb, lanes), 1) % qb) // SEL_BLOCK
    forced = (jb == 0) | (jb == cur) | (jb == cur - 1)
    score = jnp.where(jb <= cur, jnp.where(forced, FORCE_SCORE, imp), NEG)
    jbf = jb.astype(F32)
    selm = jnp.full((nb, lanes), NEG, F32)
    for rnd in range(N_SEL):
        best = jnp.max(score, axis=0, keepdims=True)
        first = jnp.min(jnp.where(score == best, jbf, 1e9), axis=0, keepdims=True)
        pick = jbf == first
        selm = jnp.where(pick, 0.0, selm)
        score = jnp.where(pick, 3.0 * NEG, score)
        if rnd % 2 == 1:
            issue()
    issue(len(pending))
    sel_st = raw[:len(sel_rows)]
    sel_st[-1] = sel_st[-1] + img_ref[IMG_PREV]
    st_new, st_old, st_mid, st_wn = raw[len(sel_rows):]
    st_new = st_new + img_ref[IMG_DIAG, 0:tail_rows]
    st_old = jnp.concatenate([st_old[:tile_rows] + img_ref[IMG_EDGE], st_old[tile_rows:]], axis=0)
    st_mid = jnp.concatenate([st_mid[:half - tile_rows], st_mid[half - tile_rows:] + img_ref[IMG_PREV]], axis=0)
    st_wn = st_wn + img_ref[IMG_DIAG, 0:tail_rows]

    for j in range(nb):
        selm_ref[j] = jnp.broadcast_to(selm[j:j + 1, :], (8, lanes))
    block_rows = SEL_BLOCK * N_KV

    def select(st, row0):
        n_rows = st.shape[0]
        nblk = -(-n_rows // block_rows)
        per = n_rows // nblk // 8
        mask = selm_ref[row0 // block_rows:row0 // block_rows + nblk]
        return (st.reshape(nblk, per, 8, lanes) + mask[:, None]).reshape(n_rows, lanes)

    segs = [(select(st, rows.start), vbuf[slot, rows, :].astype(BF16)) for st, rows in zip(sel_st, sel_rows)]
    segs.append((select(st_new, past_rows), vst_ref[...].astype(BF16)))
    sel = _softmax_segments(segs)

    win = _softmax_segments([(st_old, vw_ref[0:half, :].astype(BF16)),
                             (st_mid, vw_ref[half:win_rows, :].astype(BF16)),
                             (st_wn, vwt_ref[...].astype(BF16))])

    o_r = _combine(gt_ref[...], o_c, sel, win, qb)
    for h in range(N_HEADS):
        o_ref[:, h * HEAD_DIM:(h + 1) * HEAD_DIM] = o_r[h * qb:(h + 1) * qb]

    for src, new, dst in ((kw_ref, kwt_ref, kwo_ref), (vw_ref, vwt_ref, vwo_ref)):
        dst[0:win_rows - tail_rows, :] = src[tail_rows:win_rows, :]
        dst[win_rows - tail_rows:win_rows, :] = new[...]


def _attn_prompt(qkv_f, qkv_b, gates_t, kcmp, vcmp, img, ct, seq, off2):
    qbp = KEY_TILE
    r_l = HPG * qbp
    n_tiles = seq // qbp
    nc = seq // CMP_STRIDE
    nb = seq // SEL_BLOCK
    q_cols = N_HEADS * HEAD_DIM // HEAD_DIM

    def kv_spec(which):
        return pl.BlockSpec((seq, HEAD_DIM), functools.partial(lambda g, i, c: (0, c + g), c=q_cols + which * N_KV))

    cmp_spec = pl.BlockSpec((nc, HEAD_DIM), lambda g, i: (0, g))
    return pl.pallas_call(
        functools.partial(_attn_prompt_kernel, nc=nc, nb=nb, off2=off2),
        grid=(N_KV, n_tiles),
        in_specs=[pl.BlockSpec((qbp, HPG * HEAD_DIM), lambda g, i: (i, g)),
                  pl.BlockSpec((None, None, 3, r_l), lambda g, i: (i, g, 0, 0)),
                  cmp_spec, cmp_spec,
                  kv_spec(2), kv_spec(3), kv_spec(4), kv_spec(5),
                  pl.BlockSpec((None, N_IMG, KEY_TILE, r_l), lambda g, i: (g, 0, 0, 0)),
                  pl.BlockSpec((None, ct.shape[1], r_l), lambda g, i: (g, 0, 0))],
        out_specs=pl.BlockSpec((qbp, HPG * HEAD_DIM), lambda g, i: (i, g)),
        out_shape=jax.ShapeDtypeStruct((seq, N_HEADS * HEAD_DIM), BF16),
        scratch_shapes=[pltpu.VMEM((nb * SEL_BLOCK // (SEL_GROUP * KEY_TILE), 16, r_l), BF16),
                        pltpu.VMEM((nc, KEY_TILE), F32)],
        compiler_params=_params(("arbitrary", "arbitrary"), 48),
        name="nsa_prompt",
    )(qkv_f, gates_t, kcmp, vcmp, qkv_b, qkv_b, qkv_b, qkv_b, img, ct)


def _key_features(n_rows):
    r = jnp.arange(n_rows, dtype=jnp.int32)[:, None]
    c = jnp.arange(KEY_TILE, dtype=jnp.int32)[None, :]
    hit = (c == r // (SEL_BLOCK * N_KV)) | (c == SAMPLE_GROUP_COL + r % N_KV)
    return hit.astype(BF16)


def _attn_sample(page_table, q_new, gates_t, kcmp, vcmp, kw, vw, ks_t, vs_t, kw_t, vw_t, img, ct,
                 ks_cache, vs_cache, qb):
    n_seq, n_pages = page_table.shape
    past = n_pages * PAGE
    lanes = N_HEADS * qb
    nb = 8 * (-(-(past // SEL_BLOCK + 2) // 8))
    assert nb <= SAMPLE_GROUP_COL and lanes == KEY_TILE
    qw = N_HEADS * HEAD_DIM
    tail_rows = ks_t.shape[1]
    cmp_rows = past // CMP_STRIDE * N_KV
    feat = _key_features(past * N_KV + tail_rows)
    tail_spec = pl.BlockSpec((None, tail_rows, HEAD_DIM), lambda b, pt: (b, 0, 0))
    win_spec = pl.BlockSpec((WINDOW * N_KV, HEAD_DIM), lambda b, pt: (b, 0))
    cmp_spec = pl.BlockSpec((cmp_rows, HEAD_DIM), lambda b, pt: (b, 0))
    any_spec = pl.BlockSpec(memory_space=pl.ANY)
    return pl.pallas_call(
        functools.partial(_attn_sample_kernel, qb=qb, nb=nb),
        grid_spec=pltpu.PrefetchScalarGridSpec(
            num_scalar_prefetch=1, grid=(n_seq,),
            in_specs=[pl.BlockSpec((qb, qw), lambda b, pt: (b, 0)),
                      pl.BlockSpec((None, 3, lanes), lambda b, pt: (b, 0, 0)),
                      cmp_spec, cmp_spec, win_spec, win_spec,
                      tail_spec, tail_spec, tail_spec, tail_spec,
                      pl.BlockSpec((None,) + img.shape[1:], lambda b, pt: (0, 0, 0, 0)),
                      pl.BlockSpec((None,) + ct.shape[1:], lambda b, pt: (0, 0, 0)),
                      pl.BlockSpec(feat.shape, lambda b, pt: (0, 0)),
                      any_spec, any_spec],
            out_specs=[pl.BlockSpec((qb, qw), lambda b, pt: (b, 0)), win_spec, win_spec],
            scratch_shapes=[pltpu.VMEM((nb, 8, lanes), F32),
                            pltpu.VMEM((2, n_pages * PAGE_ROWS, HEAD_DIM), F32),
                            pltpu.VMEM((2, n_pages * PAGE_ROWS, HEAD_DIM), F32),
                            pltpu.SemaphoreType.DMA((2, 2))]),
        out_shape=[jax.ShapeDtypeStruct((n_seq * qb, qw), F32),
                   jax.ShapeDtypeStruct(kw.shape, kw.dtype), jax.ShapeDtypeStruct(vw.shape, vw.dtype)],
        compiler_params=_params(("arbitrary",), 56),
        name="nsa_sample",
    )(page_table, q_new, gates_t, kcmp, vcmp, kw, vw, ks_t, vs_t, kw_t, vw_t, img, ct, feat, ks_cache, vs_cache)


def _gates_to_lanes(g_sig, n_tiles, q_rows, qbp):
    g5 = g_sig.reshape(n_tiles, q_rows, 3, N_KV, HPG)
    if qbp > q_rows:
        g5 = jnp.pad(g5, ((0, 0), (0, qbp - q_rows), (0, 0), (0, 0), (0, 0)))
    return g5.transpose(0, 3, 2, 4, 1).reshape(n_tiles, N_KV, 3, HPG * qbp)


def _history_rows(state, dec_seq):
    n_seq, _, c = state.shape
    z = jnp.zeros((n_seq, dec_seq - 2, c), state.dtype)
    s1 = jnp.concatenate([state[:, 1:2], jnp.zeros((n_seq, 1, c), state.dtype), z], axis=1)
    s2 = jnp.concatenate([state, z], axis=1)
    return s1.reshape(n_seq * dec_seq, c), s2.reshape(n_seq * dec_seq, c)


def _cmp_weights(pe, w1, w2):
    half = CMP_STRIDE * HEAD_DIM
    w1cat = jnp.concatenate([w1[:half], w1[half:]], axis=1).astype(BF16)
    return pe.reshape(2, half), w1cat, w2.astype(BF16)


def _token_group_rows(x):
    return x.reshape(-1, HEAD_DIM)


def kernel(x_prompt, x_sample, cache_k_cmp, cache_v_cmp, cache_k_sel, cache_v_sel, state_k_win, state_v_win,
           state_conv, state_ffn_conv, page_table, rel_bias, w_in, conv_w, conv_b, w_br_conv, w_br_nsa, w_out,
           pe_cmp_k, w_cmp_k1, w_cmp_k2, pe_cmp_v, w_cmp_v1, w_cmp_v2, ln1_g, ln1_b, w_ffn_in, ffn_conv_w,
           ffn_conv_b, w_ffn_out, ln2_g, ln2_b):
    seq = x_prompt.shape[1]
    n_seq, dec_seq = x_sample.shape[0], x_sample.shape[1]
    n_s = n_seq * dec_seq
    past = page_table.shape[1] * PAGE
    kvw = N_KV * HEAD_DIM
    qw = N_HEADS * HEAD_DIM
    tm = 1024

    x_s2 = x_sample.reshape(n_s, D_MODEL)
    x_bf = jnp.concatenate([x_prompt[0].astype(BF16), x_s2.astype(BF16)], axis=0)
    c_qkv = 3 * D_CONV
    c_gate = c_qkv + qw + 6 * kvw
    c_mix = c_gate + 3 * N_HEADS
    w_main = w_in[0, :, :c_gate].astype(BF16)
    w_gate = jnp.pad(w_in[0, :, c_gate:c_mix], ((0, 0), (0, 128 - 3 * N_HEADS))).astype(BF16)
    w_mix = w_in[0, :, c_mix:].astype(BF16)

    s1, s2 = _history_rows(state_conv[0], dec_seq)
    tn_c = 256
    z_conv, u_conv = _gconv(x_bf, w_main, (0, D_CONV // tn_c, 2 * D_CONV // tn_c), conv_w[0], conv_b, s1, s2,
                            seq, tm, tn_c, 48, "proj_conv")
    qkv_f, qkv_b = _proj(x_bf, w_main, c_qkv, qw + 6 * kvw, tm, 512)
    g_sig = _gate_proj(x_bf, w_gate, tm)[:, :3 * N_HEADS]

    pe_k, w1_k, w2_k = _cmp_weights(pe_cmp_k[0], w_cmp_k1[0], w_cmp_k2[0])
    pe_v, w1_v, w2_v = _cmp_weights(pe_cmp_v[0], w_cmp_v1[0], w_cmp_v2[0])
    chunk_w = CMP_STRIDE * kvw
    kc_p = qkv_b[:seq, qw:qw + kvw].reshape(seq // CMP_STRIDE, chunk_w)
    vc_p = qkv_b[:seq, qw + kvw:qw + 2 * kvw].reshape(seq // CMP_STRIDE, chunk_w)
    kcmp_p = _compress(kc_p, pe_k, w1_k, w2_k, "compress_k_prompt")
    vcmp_p = _compress(vc_p, pe_v, w1_v, w2_v, "compress_v_prompt")
    kcmp_s, vcmp_s = _compress_paged(page_table, _token_group_rows(cache_k_cmp), _token_group_rows(cache_v_cmp),
                                     (pe_k, w1_k, w2_k), (pe_v, w1_v, w2_v))

    off2_p = (seq - KEY_TILE) // CMP_STRIDE
    nct_p = KEY_TILE * (-(-(off2_p + seq // CMP_STRIDE) // KEY_TILE))
    img_p, ct_p = _bias_images(rel_bias, HPG, KEY_TILE, 1, nct_p, off2_p, "bias_prompt")
    gates_p = _gates_to_lanes(g_sig[:seq], seq // KEY_TILE, KEY_TILE, KEY_TILE)
    o_p = _attn_prompt(qkv_f, qkv_b, gates_p, kcmp_p, vcmp_p, img_p, ct_p, seq, off2_p)

    chunks_s = past // CMP_STRIDE
    img_s, ct_s = _bias_images(rel_bias, N_HEADS, dec_seq, N_KV, chunks_s, chunks_s, "bias_sample")
    gates_s = g_sig[seq:].reshape(n_seq, dec_seq, 3, N_HEADS).transpose(0, 2, 3, 1).reshape(n_seq, 3, N_HEADS * dec_seq)
    new_f = qkv_f[seq:].reshape(n_seq, dec_seq, qw + 6 * kvw)

    def tail(col):
        return new_f[:, :, col:col + kvw].reshape(n_seq, dec_seq * N_KV, HEAD_DIM)

    o_s, kw_next, vw_next = _attn_sample(page_table, qkv_f[seq:, :qw], gates_s,
                       kcmp_s.reshape(-1, HEAD_DIM), vcmp_s.reshape(-1, HEAD_DIM),
                       _token_group_rows(state_k_win), _token_group_rows(state_v_win),
                       tail(qw + 2 * kvw), tail(qw + 3 * kvw), tail(qw + 4 * kvw), tail(qw + 5 * kvw),
                       img_s, ct_s, _token_group_rows(cache_k_sel), _token_group_rows(cache_v_sel), dec_seq)
    o_all = jnp.concatenate([o_p, o_s.astype(BF16)], axis=0)

    mix = _merge(x_bf, z_conv, o_all, w_br_conv[0], w_br_nsa[0], w_mix, 512, 256)
    r1 = _resid_mm_split(mix, w_out[0], x_prompt[0], x_s2, tm, 512, 56, "out_proj")
    h_f, h_b = _layer_norm(r1, ln1_g, ln1_b, 256, "ln1")
    f1, f2 = _history_rows(state_ffn_conv[0], dec_seq)
    tn_f = 256
    act, gp = _gconv(h_b, w_ffn_in[0], (D_FF // tn_f, 0), ffn_conv_w[0], ffn_conv_b, f1, f2,
                     seq, tm, tn_f, 56, "ffn_in")
    r2 = _resid_mm(act, w_ffn_out[0].astype(BF16), h_f, 512, 256, 56, "ffn_out")
    y_p, y_s = _layer_norm_split(r2, ln2_g, ln2_b, seq, 256, "ln2")

    def rows_p(col):
        return qkv_f[:seq, col:col + kvw].reshape(1, 1, seq, N_KV, HEAD_DIM)

    def rows_s(col):
        return new_f[:, :, col:col + kvw].reshape(1, n_seq, dec_seq, N_KV, HEAD_DIM)

    def win_p(col):
        return qkv_f[seq - WINDOW:seq, col:col + kvw].reshape(1, 1, WINDOW, N_KV, HEAD_DIM)


    def last2(u):
        c = u.shape[1]
        return (u[seq - 2:seq].reshape(1, 1, 2, c),
                u[seq:].reshape(n_seq, dec_seq, c)[:, dec_seq - 2:].reshape(1, n_seq, 2, c))

    conv_p, conv_s = last2(u_conv)
    ffn_p, ffn_s = last2(gp)
    c = qw
    return (y_p.reshape(1, seq, D_MODEL), y_s.reshape(n_seq, dec_seq, D_MODEL),
            rows_p(c), rows_s(c), rows_p(c + kvw), rows_s(c + kvw),
            rows_p(c + 2 * kvw), rows_s(c + 2 * kvw), rows_p(c + 3 * kvw), rows_s(c + 3 * kvw),
            win_p(c + 4 * kvw), kw_next.reshape(state_k_win.shape),
            win_p(c + 5 * kvw), vw_next.reshape(state_v_win.shape),
            conv_p, conv_s, ffn_p, ffn_s)
```

```python
import functools
import math

import jax
import jax.numpy as jnp
from jax import lax
from jax.experimental import pallas as pl
from jax.experimental.pallas import tpu as pltpu

F32 = jnp.float32
BF16 = jnp.bfloat16

D_MODEL = 4096
D_CONV = 2048
N_HEADS = 16
HEAD_DIM = 128
N_KV = 4
HPG = 4
CMP_STRIDE = 16
CMP_HID = 256
SEL_BLOCK = 64
N_SEL = 16
WINDOW = 512
FORCE_SCORE = 1e4
REL_BUCKETS = 32
REL_EXACT = 16
REL_MAX_DIST = 128
D_FF = 11008
PAGE = 128
ALPHA = 2.0 ** 0.25
LN_EPS = 1e-5
LOG2E = 1.0 / math.log(2.0)
Q_SCALE = HEAD_DIM ** -0.5 * LOG2E

KEY_TILE = 128
SEL_GROUP = 4
GCONV_PARTS = 2
SEL_STATES = 3
PAGE_ROWS = PAGE * N_KV
NEG = -1e30
MIB = 1024 * 1024

IMG_DIAG, IMG_PREV, IMG_ZERO, IMG_EDGE, IMG_NEG = 0, 1, 2, 3, 4
N_IMG = 5


def _params(sem, vmem_mib):
    return pltpu.CompilerParams(dimension_semantics=sem, vmem_limit_bytes=vmem_mib * MIB)


def _dot(a, b):
    return jnp.dot(a, b, preferred_element_type=F32)


def _proj_kernel(x_ref, w_ref, of_ref, ob_ref):
    acc = _dot(x_ref[...], w_ref[...].astype(BF16))
    of_ref[...] = acc
    ob_ref[...] = acc.astype(BF16)


def _proj(x, w, col0, ncols, tm, tn):
    m_rows, k = x.shape
    c0 = col0 // tn
    return pl.pallas_call(
        _proj_kernel,
        grid=(m_rows // tm, ncols // tn),
        in_specs=[pl.BlockSpec((tm, k), lambda m, n: (m, 0)),
                  pl.BlockSpec((k, tn), lambda m, n: (0, c0 + n))],
        out_specs=[pl.BlockSpec((tm, tn), lambda m, n: (m, n)),
                   pl.BlockSpec((tm, tn), lambda m, n: (m, n))],
        out_shape=[jax.ShapeDtypeStruct((m_rows, ncols), F32),
                   jax.ShapeDtypeStruct((m_rows, ncols), BF16)],
        compiler_params=_params(("arbitrary", "arbitrary"), 56),
        name="proj_qkv",
    )(x, w)


def _gate_kernel(x_ref, w_ref, o_ref):
    o_ref[...] = jax.nn.sigmoid(_dot(x_ref[...], w_ref[...]))


def _gate_proj(x, w, tm):
    m_rows, k = x.shape
    n = w.shape[1]
    return pl.pallas_call(
        _gate_kernel,
        grid=(m_rows // tm,),
        in_specs=[pl.BlockSpec((tm, k), lambda m: (m, 0)),
                  pl.BlockSpec((k, n), lambda m: (0, 0))],
        out_specs=pl.BlockSpec((tm, n), lambda m: (m, 0)),
        out_shape=jax.ShapeDtypeStruct((m_rows, n), F32),
        compiler_params=_params(("arbitrary",), 40),
        name="proj_gates",
    )(x, w)


def _gconv_kernel(*refs, three, n_prompt_tiles):
    if three:
        x_ref, w0_ref, w1_ref, w2_ref, cw_ref, cb_ref, s1_ref, s2_ref, z_ref, u_ref, carry_ref = refs
    else:
        x_ref, w0_ref, w1_ref, cw_ref, cb_ref, s1_ref, s2_ref, z_ref, u_ref, carry_ref = refs
    m = pl.program_id(0)
    n = pl.program_id(1)
    is_sample = m >= n_prompt_tiles
    tm, tn = u_ref.shape
    hm = tm // GCONV_PARTS
    cw = cw_ref[...]
    cb = cb_ref[...]
    c = carry_ref[n]
    c = jnp.where(m == 0, jnp.zeros_like(c), c)
    prev_a, prev_b = c[6:7], c[7:8]
    row = lax.broadcasted_iota(jnp.int32, (hm, tn), 0)
    rmask = row & jnp.where(is_sample, 7, hm - 1)
    w0 = w0_ref[...].astype(BF16)
    w1 = w1_ref[...].astype(BF16)
    w2 = w2_ref[...].astype(BF16) if three else None

    def project(part):
        x = x_ref[part * hm:(part + 1) * hm, :]
        p0 = _dot(x, w0)
        u = _dot(x, w1)
        if three:
            u = u * _dot(x, w2)
        return p0, u

    def epilogue(part, p0, u, prev_a, prev_b):
        rows = slice(part * hm, (part + 1) * hm)
        u_ref[rows, :] = u
        fill1 = jnp.where(is_sample, s1_ref[rows, :], prev_b)
        fill2 = jnp.where(is_sample, s2_ref[rows, :], jnp.where(row == 0, prev_a, prev_b))
        prev1 = jnp.where(rmask == 0, fill1, pltpu.roll(u, 1, 0))
        prev2 = jnp.where(rmask < 2, fill2, pltpu.roll(u, 2, 0))
        conv = cb + prev2 * cw[0:1] + prev1 * cw[1:2] + u * cw[2:3]
        if three:
            z = p0 * conv
        else:
            z = jax.nn.silu(conv) * p0
        z_ref[rows, :] = z.astype(z_ref.dtype)
        return u[hm - 2:hm - 1], u[hm - 1:hm]

    done = project(0)
    for part in range(1, GCONV_PARTS):
        nxt = project(part)
        prev_a, prev_b = epilogue(part - 1, *done, prev_a, prev_b)
        done = nxt
    epilogue(GCONV_PARTS - 1, *done, prev_a, prev_b)
    carry_ref[n] = done[1][hm - 8:hm]


def _gconv(x, w, col_blocks, cw, cb, s1, s2, n_prompt_rows, tm, tn, vmem_mib, name):
    m_rows, k = x.shape
    ncols = cw.shape[1]
    three = len(col_blocks) == 3
    assert m_rows - n_prompt_rows == tm and n_prompt_rows % tm == 0
    nt = ncols // tn
    w_specs = [pl.BlockSpec((k, tn), functools.partial(lambda m, n, c: (0, c + n), c=c)) for c in col_blocks]
    col_spec = pl.BlockSpec((tm, tn), lambda m, n: (m, n))
    npt = n_prompt_rows // tm
    hist_spec = pl.BlockSpec((tm, tn), lambda m, n: (0, jnp.where(m < npt, 0, n)))
    return pl.pallas_call(
        functools.partial(_gconv_kernel, three=three, n_prompt_tiles=npt),
        grid=(m_rows // tm, nt),
        in_specs=[pl.BlockSpec((tm, k), lambda m, n: (m, 0))] + w_specs + [
            pl.BlockSpec((3, tn), lambda m, n: (0, n)),
            pl.BlockSpec((1, tn), lambda m, n: (0, n)),
            hist_spec, hist_spec],
        out_specs=[col_spec, col_spec],
        out_shape=[jax.ShapeDtypeStruct((m_rows, ncols), BF16),
                   jax.ShapeDtypeStruct((m_rows, ncols), F32)],
        scratch_shapes=[pltpu.VMEM((nt, 8, tn), F32)],
        compiler_params=_params(("arbitrary", "arbitrary"), vmem_mib),
        name=name,
    )(x, *([w] * len(col_blocks)), cw, cb, s1, s2)


def _merge_kernel(x_ref, z_ref, op_ref, os_ref, wc_ref, wn_ref, wga_ref, wgb_ref, out_ref, *, n_prompt_tiles):
    x = x_ref[...]
    o = jnp.where(pl.program_id(0) < n_prompt_tiles, op_ref[...], os_ref[...].astype(BF16))
    a = _dot(z_ref[...], wc_ref[...].astype(BF16))
    b = _dot(o, wn_ref[...].astype(BF16))
    ga = jax.nn.sigmoid(_dot(x, wga_ref[...]))
    gb = jax.nn.sigmoid(_dot(x, wgb_ref[...]))
    out_ref[...] = (ga * a + gb * b).astype(BF16)


def _merge(x, z, o_p, o_s, wc, wn, wg, tm, tn):
    m_rows, k = x.shape
    kc = z.shape[1]
    nt = D_MODEL // tn
    npt = o_p.shape[0] // tm
    assert npt * tm == o_p.shape[0] and o_s.shape[0] % tm == 0 and o_p.shape[0] + o_s.shape[0] == m_rows
    return pl.pallas_call(
        functools.partial(_merge_kernel, n_prompt_tiles=npt),
        grid=(m_rows // tm, nt),
        in_specs=[pl.BlockSpec((tm, k), lambda m, n: (m, 0)),
                  pl.BlockSpec((tm, kc), lambda m, n: (m, 0)),
                  pl.BlockSpec((tm, kc), lambda m, n: (jnp.minimum(m, npt - 1), 0)),
                  pl.BlockSpec((tm, kc), lambda m, n: (jnp.maximum(m - npt, 0), 0)),
                  pl.BlockSpec((kc, tn), lambda m, n: (0, n)),
                  pl.BlockSpec((kc, tn), lambda m, n: (0, n)),
                  pl.BlockSpec((k, tn), lambda m, n: (0, n)),
                  pl.BlockSpec((k, tn), lambda m, n: (0, nt + n))],
        out_specs=pl.BlockSpec((tm, tn), lambda m, n: (m, n)),
        out_shape=jax.ShapeDtypeStruct((m_rows, D_MODEL), BF16),
        compiler_params=_params(("arbitrary", "arbitrary"), 56),
        name="merge_branches",
    )(x, z, o_p, o_s, wc, wn, wg, wg)


def _resid_kernel(l_ref, w_ref, r_ref, o_ref):
    o_ref[...] = ALPHA * r_ref[...] + _dot(l_ref[...], w_ref[...])


def _resid2_kernel(l_ref, w_ref, rp_ref, rs_ref, o_ref, *, n_prompt_tiles):
    res = jnp.where(pl.program_id(0) < n_prompt_tiles, rp_ref[...], rs_ref[...])
    o_ref[...] = ALPHA * res + _dot(l_ref[...], w_ref[...].astype(BF16))


def _resid_mm_split(lhs, w, res_p, res_s, tm, tn, vmem_mib, name):
    m_rows, k = lhs.shape
    n_cols = w.shape[1]
    npt = res_p.shape[0] // tm
    assert res_s.shape[0] == tm and npt * tm + tm == m_rows
    return pl.pallas_call(
        functools.partial(_resid2_kernel, n_prompt_tiles=npt),
        grid=(m_rows // tm, n_cols // tn),
        in_specs=[pl.BlockSpec((tm, k), lambda m, n: (m, 0)),
                  pl.BlockSpec((k, tn), lambda m, n: (0, n)),
                  pl.BlockSpec((tm, tn), lambda m, n: (jnp.minimum(m, npt - 1), n)),
                  pl.BlockSpec((tm, tn), lambda m, n: (0, n))],
        out_specs=pl.BlockSpec((tm, tn), lambda m, n: (m, n)),
        out_shape=jax.ShapeDtypeStruct((m_rows, n_cols), F32),
        compiler_params=_params(("arbitrary", "arbitrary"), vmem_mib),
        name=name,
    )(lhs, w, res_p, res_s)


def _resid_mm(lhs, w, res, tm, tn, vmem_mib, name):
    m_rows, k = lhs.shape
    n_cols = w.shape[1]
    return pl.pallas_call(
        _resid_kernel,
        grid=(m_rows // tm, n_cols // tn),
        in_specs=[pl.BlockSpec((tm, k), lambda m, n: (m, 0)),
                  pl.BlockSpec((k, tn), lambda m, n: (0, n)),
                  pl.BlockSpec((tm, tn), lambda m, n: (m, n))],
        out_specs=pl.BlockSpec((tm, tn), lambda m, n: (m, n)),
        out_shape=jax.ShapeDtypeStruct((m_rows, n_cols), F32),
        compiler_params=_params(("arbitrary", "arbitrary"), vmem_mib),
        name=name,
    )(lhs, w, res)


def _ln_kernel(x_ref, g_ref, b_ref, of_ref, ob_ref):
    x = x_ref[...]
    mu = jnp.mean(x, axis=-1, keepdims=True)
    xc = x - mu
    var = jnp.mean(xc * xc, axis=-1, keepdims=True)
    y = xc * lax.rsqrt(var + LN_EPS) * g_ref[...] + b_ref[...]
    of_ref[...] = y
    ob_ref[...] = y.astype(BF16)


def _layer_norm(x, g, b, tr, name):
    m_rows, d = x.shape
    row_spec = pl.BlockSpec((tr, d), lambda m: (m, 0))
    vec_spec = pl.BlockSpec((1, d), lambda m: (0, 0))
    return pl.pallas_call(
        _ln_kernel,
        grid=(m_rows // tr,),
        in_specs=[row_spec, vec_spec, vec_spec],
        out_specs=[row_spec, row_spec],
        out_shape=[jax.ShapeDtypeStruct((m_rows, d), F32), jax.ShapeDtypeStruct((m_rows, d), BF16)],
        compiler_params=_params(("arbitrary",), 48),
        name=name,
    )(x, g, b)


def _ln_split_kernel(x_ref, g_ref, b_ref, yp_ref, ys_ref, *, n_prompt_tiles):
    x = x_ref[...]
    mu = jnp.mean(x, axis=-1, keepdims=True)
    xc = x - mu
    var = jnp.mean(xc * xc, axis=-1, keepdims=True)
    y = xc * lax.rsqrt(var + LN_EPS) * g_ref[...] + b_ref[...]
    m = pl.program_id(0)

    @pl.when(m < n_prompt_tiles)
    def _():
        yp_ref[...] = y

    @pl.when(m >= n_prompt_tiles)
    def _():
        ys_ref[...] = y


def _layer_norm_split(x, g, b, n_prompt_rows, tr, name):
    m_rows, d = x.shape
    npt = n_prompt_rows // tr
    vec_spec = pl.BlockSpec((1, d), lambda m: (0, 0))
    return pl.pallas_call(
        functools.partial(_ln_split_kernel, n_prompt_tiles=npt),
        grid=(m_rows // tr,),
        in_specs=[pl.BlockSpec((tr, d), lambda m: (m, 0)), vec_spec, vec_spec],
        out_specs=[pl.BlockSpec((tr, d), lambda m: (jnp.minimum(m, npt - 1), 0)),
                   pl.BlockSpec((tr, d), lambda m: (jnp.maximum(m - npt, 0), 0))],
        out_shape=[jax.ShapeDtypeStruct((n_prompt_rows, d), F32),
                   jax.ShapeDtypeStruct((m_rows - n_prompt_rows, d), F32)],
        compiler_params=_params(("arbitrary",), 48),
        name=name,
    )(x, g, b)


def _page_copies(pt_ref, seq, cache_ref, buf, slot, sem):
    return [pltpu.make_async_copy(
        cache_ref.at[pl.ds(pl.multiple_of(pt_ref[seq, j] * PAGE_ROWS, PAGE_ROWS), PAGE_ROWS)],
        buf.at[slot, pl.ds(j * PAGE_ROWS, PAGE_ROWS)], sem)
        for j in range(pt_ref.shape[1])]


def _prefetch_pages(pt_ref, b, n_seq, streams):
    slot = b % 2

    @pl.when(b == 0)
    def _():
        for cache_ref, buf, sem in streams:
            for cp in _page_copies(pt_ref, 0, cache_ref, buf, 0, sem.at[0]):
                cp.start()

    @pl.when(b + 1 < n_seq)
    def _():
        for cache_ref, buf, sem in streams:
            for cp in _page_copies(pt_ref, b + 1, cache_ref, buf, 1 - slot, sem.at[1 - slot]):
                cp.start()

    for cache_ref, buf, sem in streams:
        for cp in _page_copies(pt_ref, b, cache_ref, buf, slot, sem.at[slot]):
            cp.wait()
    return slot


def _compress_rows(xg, pe, w1, w2, next_chunk=1):
    return _compress_finish(_compress_first_layer(xg, pe, w1), w2, next_chunk)


def _compress_first_layer(xg, pe, w1):
    half = CMP_STRIDE * HEAD_DIM
    pa = _dot(jnp.broadcast_to(pe[0:1], (8, half)).astype(BF16), w1[:, :CMP_HID])[0:1]
    pb = _dot(jnp.broadcast_to(pe[1:2], (8, half)).astype(BF16), w1[:, CMP_HID:])[0:1]
    fs = _dot(xg, w1)
    return fs[:, :CMP_HID] + pa, fs[:, CMP_HID:] + pb


def _compress_finish(halves, w2, next_chunk):
    first, second = halves
    hid = jax.nn.gelu(first + pltpu.roll(second, second.shape[0] - next_chunk, 0))
    return _dot(hid.astype(BF16), w2)


def _compress_kernel(x_ref, pe_ref, w1_ref, w2_ref, o_ref):
    row_w = N_KV * HEAD_DIM
    for g in range(N_KV):
        xg = jnp.concatenate(
            [x_ref[:, r * row_w + g * HEAD_DIM: r * row_w + (g + 1) * HEAD_DIM] for r in range(CMP_STRIDE)],
            axis=1).astype(BF16)
        o_ref[:, g * HEAD_DIM:(g + 1) * HEAD_DIM] = _compress_rows(
            xg, pe_ref[...], w1_ref[...], w2_ref[...]).astype(BF16)


def _compress(x2, pe2, w1cat, w2, name):
    rows, width = x2.shape
    return pl.pallas_call(
        _compress_kernel,
        grid=(1,),
        in_specs=[pl.BlockSpec((rows, width), lambda t: (0, 0)),
                  pl.BlockSpec(pe2.shape, lambda t: (0, 0)),
                  pl.BlockSpec(w1cat.shape, lambda t: (0, 0)),
                  pl.BlockSpec(w2.shape, lambda t: (0, 0))],
        out_specs=pl.BlockSpec((rows, N_KV * HEAD_DIM), lambda t: (0, 0)),
        out_shape=jax.ShapeDtypeStruct((rows, N_KV * HEAD_DIM), BF16),
        compiler_params=_params(("arbitrary",), 48),
        name=name,
    )(x2, pe2, w1cat, w2)


def _compress_paged_kernel(pt_ref, kc_hbm, vc_hbm, pek_ref, w1k_ref, w2k_ref, pev_ref, w1v_ref, w2v_ref,
                           ok_ref, ov_ref, kbuf, vbuf, sem, *, chunks):
    b = pl.program_id(0)
    slot = _prefetch_pages(pt_ref, b, pl.num_programs(0), [(kc_hbm, kbuf, sem.at[0]), (vc_hbm, vbuf, sem.at[1])])

    def chunk_rows(buf):
        x3 = buf[slot].reshape(chunks, CMP_STRIDE * N_KV, HEAD_DIM)
        return jnp.concatenate(
            [x3[:, r * N_KV:(r + 1) * N_KV, :].reshape(chunks * N_KV, HEAD_DIM) for r in range(CMP_STRIDE)],
            axis=1).astype(BF16)

    hk = _compress_first_layer(chunk_rows(kbuf), pek_ref[...], w1k_ref[...])
    hv = _compress_first_layer(chunk_rows(vbuf), pev_ref[...], w1v_ref[...])
    ok_ref[...] = _compress_finish(hk, w2k_ref[...], N_KV).astype(BF16)
    ov_ref[...] = _compress_finish(hv, w2v_ref[...], N_KV).astype(BF16)


def _compress_paged(page_table, kc_cache, vc_cache, wk, wv):
    n_seq, n_pages = page_table.shape
    chunks = n_pages * PAGE // CMP_STRIDE
    w_specs = [pl.BlockSpec(w.shape, lambda b, pt: (0, 0)) for w in wk + wv]
    out_spec = pl.BlockSpec((chunks * N_KV, HEAD_DIM), lambda b, pt: (b, 0))
    out_shape = jax.ShapeDtypeStruct((n_seq * chunks * N_KV, HEAD_DIM), BF16)
    buf = pltpu.VMEM((2, n_pages * PAGE_ROWS, HEAD_DIM), F32)
    return pl.pallas_call(
        functools.partial(_compress_paged_kernel, chunks=chunks),
        grid_spec=pltpu.PrefetchScalarGridSpec(
            num_scalar_prefetch=1, grid=(n_seq,),
            in_specs=[pl.BlockSpec(memory_space=pl.ANY), pl.BlockSpec(memory_space=pl.ANY)] + w_specs,
            out_specs=[out_spec, out_spec],
            scratch_shapes=[buf, buf, pltpu.SemaphoreType.DMA((2, 2))]),
        out_shape=[out_shape, out_shape],
        compiler_params=_params(("arbitrary",), 48),
        name="compress_sample",
    )(page_table, kc_cache, vc_cache, *wk, *wv)


def _bias_kernel(rb_ref, img_ref, ct_ref, *, heads, qbp, rpk, off2):
    g = pl.program_id(0)
    r_l = heads * qbp
    lane = lax.broadcasted_iota(jnp.int32, (1, r_l), 1)
    n_l = lane // qbp
    tbl = []
    for k in range(REL_BUCKETS):
        row = jnp.zeros((1, r_l), F32)
        for n in range(heads):
            row = jnp.where(n_l == n, rb_ref[k, g * heads + n], row)
        tbl.append(row)

    def bias_of(dist):
        nn = jnp.maximum(dist, 0)
        nf = jnp.maximum(nn, 1).astype(F32)
        large = REL_EXACT + (jnp.log(nf / REL_EXACT) / math.log(REL_MAX_DIST / REL_EXACT)
                             * (REL_BUCKETS - REL_EXACT)).astype(jnp.int32)
        bucket = jnp.where(nn < REL_EXACT, nn, jnp.minimum(large, REL_BUCKETS - 1))
        val = jnp.zeros(dist.shape, F32)
        for k in range(REL_BUCKETS):
            val = jnp.where(bucket == k, tbl[k], val)
        return val

    shape = (KEY_TILE, r_l)
    kj = lax.broadcasted_iota(jnp.int32, shape, 0) // rpk
    qi = lax.broadcasted_iota(jnp.int32, shape, 1) % qbp
    far = tbl[REL_BUCKETS - 1]
    for r0 in range(0, rpk * KEY_TILE, KEY_TILE):
        rows = slice(r0, r0 + KEY_TILE)
        d0 = qi - (kj + r0 // rpk)
        img_ref[IMG_DIAG, rows] = jnp.where(d0 >= 0, (bias_of(d0) - far) * LOG2E, NEG)
        img_ref[IMG_PREV, rows] = (bias_of(d0 + KEY_TILE) - far) * LOG2E
        img_ref[IMG_ZERO, rows] = jnp.zeros(shape, F32)
        img_ref[IMG_EDGE, rows] = jnp.where(d0 < 0, 0.0, NEG)
        img_ref[IMG_NEG, rows] = jnp.full(shape, NEG, F32)
    for r0 in range(0, ct_ref.shape[0], KEY_TILE):
        dist = qi + CMP_STRIDE * (off2 - r0 // rpk - kj) - (2 * CMP_STRIDE - 1)
        ct_ref[r0:r0 + KEY_TILE, :] = jnp.where(dist >= 0, bias_of(dist) * LOG2E, NEG)


def _bias_images(rel_bias, heads, qbp, rpk, nct, off2, name):
    r_l = heads * qbp
    groups = N_HEADS // heads
    return pl.pallas_call(
        functools.partial(_bias_kernel, heads=heads, qbp=qbp, rpk=rpk, off2=off2),
        grid=(groups,),
        in_specs=[pl.BlockSpec(memory_space=pltpu.SMEM)],
        out_specs=[pl.BlockSpec((None, N_IMG, rpk * KEY_TILE, r_l), lambda g: (g, 0, 0, 0)),
                   pl.BlockSpec((None, rpk * nct, r_l), lambda g: (g, 0, 0))],
        out_shape=[jax.ShapeDtypeStruct((groups, N_IMG, rpk * KEY_TILE, r_l), F32),
                   jax.ShapeDtypeStruct((groups, rpk * nct, r_l), F32)],
        compiler_params=_params(("arbitrary",), 32),
        name=name,
    )(rel_bias)


def _pv(v, p):
    return lax.dot_general(v, p.astype(BF16), (((0,), (0,)), ((), ())), preferred_element_type=F32)


def _online_update(carry, st, v):
    m_run, l_run, acc = carry
    m_new = jnp.maximum(m_run, jnp.max(st, axis=0, keepdims=True))
    alpha = jnp.exp2(m_run - m_new)
    p = jnp.exp2(st - m_new)
    l_new = alpha * l_run + jnp.sum(p, axis=0, keepdims=True)
    return m_new, l_new, alpha * acc + _pv(v, p)


def _softmax_init(r_l):
    return jnp.full((1, r_l), NEG, F32), jnp.zeros((1, r_l), F32), jnp.zeros((HEAD_DIM, r_l), F32)


def _block_onehot(n_keys):
    kj = lax.broadcasted_iota(jnp.int32, (n_keys, KEY_TILE), 0)
    col = lax.broadcasted_iota(jnp.int32, (n_keys, KEY_TILE), 1)
    return jnp.where(col == kj // SEL_BLOCK, 1.0, 0.0).astype(BF16)


def _query_lanes(q):
    q = q * Q_SCALE
    qg = jnp.concatenate([q[:, n * HEAD_DIM:(n + 1) * HEAD_DIM] for n in range(HPG)], axis=0)
    return qg.T.astype(BF16)


def _cmp_and_select(q_t, kc, vc, cbias, t0, ps_ref, *, qbp, nc, nb, side_work=None):
    r_l = HPG * qbp
    s = _dot(kc, q_t) + cbias
    valid = cbias > 0.5 * NEG
    mx = jnp.max(s, axis=0, keepdims=True)
    mx = jnp.where(mx > 0.5 * NEG, mx, 0.0)
    e = jnp.where(valid, jnp.exp2(s - mx), 0.0)
    p_c = e / jnp.maximum(jnp.sum(e, axis=0, keepdims=True), 1e-30)
    o_c = _pv(vc, p_c)

    if qbp == KEY_TILE:
        psum = p_c[:, 0:qbp]
        for n in range(1, HPG):
            psum = psum + p_c[:, n * qbp:(n + 1) * qbp]
    else:
        psum = p_c
        for n in range(1, HPG):
            psum = psum + pltpu.roll(p_c, n * qbp, 1)
    ps_ref[...] = psum
    per_sel = SEL_BLOCK // CMP_STRIDE
    nbi = nc // per_sel
    imp = ps_ref[pl.ds(0, nbi, stride=per_sel), :]
    for c in range(1, per_sel):
        imp = imp + ps_ref[pl.ds(c, nbi, stride=per_sel), :]
    if nb > nbi:
        imp = jnp.concatenate([imp, jnp.zeros((nb - nbi, KEY_TILE), F32)], axis=0)
    jb = lax.broadcasted_iota(jnp.int32, (nb, KEY_TILE), 0)
    tq = t0 + lax.broadcasted_iota(jnp.int32, (nb, KEY_TILE), 1) % qbp
    cur = tq // SEL_BLOCK
    forced = (jb == 0) | (jb == cur) | (jb == cur - 1)
    score = jnp.where(jb <= cur, jnp.where(forced, FORCE_SCORE, imp), NEG)
    jbf = jb.astype(F32)
    side = side_work() if side_work is not None else None
    selm = jnp.full((nb, KEY_TILE), NEG, F32)
    for _ in range(N_SEL):
        best = jnp.max(score, axis=0, keepdims=True)
        first = jnp.min(jnp.where(score == best, jbf, 1e9), axis=0, keepdims=True)
        pick = jbf == first
        selm = jnp.where(pick, 0.0, selm)
        score = jnp.where(pick, 3.0 * NEG, score)
    if r_l > KEY_TILE:
        selm = jnp.concatenate([selm] * (r_l // KEY_TILE), axis=1)
    return o_c, selm, side


def _combine(gt, o_c, sel, win, qbp):
    (_, l_s, acc_s), (_, l_w, acc_w) = sel, win
    o_t = gt[0:1] * o_c + gt[1:2] * (acc_s / l_s) + gt[2:3] * (acc_w / l_w)
    return o_t.T


def _attn_prompt_kernel(q_ref, gt_ref, kc_ref, vc_ref, ks_ref, vs_ref, kw_ref, vw_ref, img_ref, ct_ref,
                        o_ref, selg_ref, ps_ref, *, nc, nb, off2):
    qbp = KEY_TILE
    r_l = HPG * qbp
    i_tile = pl.program_id(1)
    t0 = i_tile * KEY_TILE
    q_t = _query_lanes(q_ref[...])
    n_win = WINDOW // KEY_TILE

    def img_of(d, edge):
        idx = jnp.minimum(d, IMG_ZERO)
        if edge:
            idx = jnp.where(d == n_win, IMG_EDGE, idx)
        return img_ref[jnp.where(d < 0, IMG_NEG, idx)]

    def window():
        first_tile = jnp.maximum(i_tile - n_win, 0)
        rows = pl.ds(pl.multiple_of(first_tile * KEY_TILE, KEY_TILE), (n_win + 1) * KEY_TILE)
        add = jnp.concatenate([img_of(i_tile - (first_tile + t), True) for t in range(n_win + 1)], axis=0)
        return _online_update(_softmax_init(r_l), _dot(kw_ref[rows, :], q_t) + add, vw_ref[rows, :])

    start = pl.multiple_of(off2 - i_tile * (KEY_TILE // CMP_STRIDE), 8)
    o_c, selm, win = _cmp_and_select(q_t, kc_ref[...], vc_ref[...], ct_ref[pl.ds(start, nc), :], t0, ps_ref,
                                     qbp=qbp, nc=nc, nb=nb, side_work=window)

    grp_keys = SEL_GROUP * KEY_TILE
    grp_blocks = grp_keys // SEL_BLOCK
    n_grp, pad_rows = selg_ref.shape[0], selg_ref.shape[1] - grp_blocks
    selg_ref[...] = jnp.concatenate(
        [selm.reshape(n_grp, grp_blocks, r_l), jnp.zeros((n_grp, pad_rows, r_l), F32)], axis=1).astype(BF16)
    onehot = _block_onehot(grp_keys)
    zpad = jnp.zeros((KEY_TILE - selg_ref.shape[1], r_l), BF16)

    def grp_rows(it):
        return pl.ds(pl.multiple_of(it * grp_keys, grp_keys), grp_keys)

    n_it = (i_tile + SEL_GROUP) // SEL_GROUP
    n_far = jnp.maximum(i_tile - 1, 0) // SEL_GROUP


    def scores(it):
        k_aug = jnp.concatenate([ks_ref[grp_rows(it), :], onehot], axis=1)
        q_aug = jnp.concatenate([q_t, selg_ref[it], zpad], axis=0)
        return _dot(k_aug, q_aug)

    def multi_body(itn, states):
        sts = [scores(SEL_STATES * itn + k) for k in range(SEL_STATES)]
        return tuple(_online_update(states[k], sts[k], vs_ref[grp_rows(SEL_STATES * itn + k), :])
                     for k in range(SEL_STATES))

    def single_body(it, state):
        st = scores(it) + jnp.concatenate(
            [img_of(i_tile - (it * SEL_GROUP + t), False) for t in range(SEL_GROUP)], axis=0)
        return _online_update(state, st, vs_ref[grp_rows(it), :])

    n_multi = n_far // SEL_STATES
    states = lax.fori_loop(0, n_multi, multi_body, tuple(_softmax_init(r_l) for _ in range(SEL_STATES)))
    states = (lax.fori_loop(SEL_STATES * n_multi, n_it, single_body, states[0]),) + states[1:]
    m_all = states[0][0]
    for st_k in states[1:]:
        m_all = jnp.maximum(m_all, st_k[0])
    l_all, acc_all = None, None
    for m_k, l_k, acc_k in states:
        f_k = jnp.exp2(m_k - m_all)
        l_all = f_k * l_k if l_all is None else l_all + f_k * l_k
        acc_all = f_k * acc_k if acc_all is None else acc_all + f_k * acc_k
    sel = (m_all, l_all, acc_all)

    o_r = _combine(gt_ref[...], o_c, sel, win, qbp)
    for n in range(HPG):
        o_ref[:, n * HEAD_DIM:(n + 1) * HEAD_DIM] = o_r[n * qbp:(n + 1) * qbp].astype(o_ref.dtype)


SAMPLE_GROUP_COL = 64
SAMPLE_FAR_PARTS = 6


def _softmax_segments(segs):
    m = jnp.max(segs[0][0], axis=0, keepdims=True)
    for st, _ in segs[1:]:
        m = jnp.maximum(m, jnp.max(st, axis=0, keepdims=True))
    l_sum, acc = None, None
    for st, v in segs:
        p = jnp.exp2(st - m)
        l_part, a_part = jnp.sum(p, axis=0, keepdims=True), _pv(v, p)
        l_sum = l_part if l_sum is None else l_sum + l_part
        acc = a_part if acc is None else acc + a_part
    return m, l_sum, acc


def _attn_sample_kernel(pt_ref, q_ref, gt_ref, kc_ref, vc_ref, kw_ref, vw_ref, kst_ref, vst_ref, kwt_ref, vwt_ref,
                        img_ref, ct_ref, feat_ref, ks_hbm, vs_hbm, o_ref, kwo_ref, vwo_ref, selm_ref, kbuf, vbuf,
                        sem, *, qb, nb):
    b = pl.program_id(0)
    slot = _prefetch_pages(pt_ref, b, pl.num_programs(0), [(ks_hbm, kbuf, sem.at[0]), (vs_hbm, vbuf, sem.at[1])])
    lanes = N_HEADS * qb
    past_rows = kbuf.shape[1]
    past = past_rows // N_KV
    tile_rows = KEY_TILE * N_KV
    tail_rows = kst_ref.shape[0]
    win_rows = kw_ref.shape[0]
    cmp_rows = kc_ref.shape[0]

    q = q_ref[...] * Q_SCALE
    q_t = jnp.concatenate([q[:, h * HEAD_DIM:(h + 1) * HEAD_DIM] for h in range(N_HEADS)], axis=0).T.astype(BF16)
    lane = lax.broadcasted_iota(jnp.int32, (N_KV, lanes), 1)
    grp_rows = jnp.where(lane // (HPG * qb) == lax.broadcasted_iota(jnp.int32, (N_KV, lanes), 0), 0.0, NEG)

    q_aug = jnp.concatenate(
        [q_t, jnp.concatenate([jnp.zeros((SAMPLE_GROUP_COL, lanes), F32), grp_rows,
                               jnp.zeros((KEY_TILE - SAMPLE_GROUP_COL - N_KV, lanes), F32)], axis=0).astype(BF16)],
        axis=0)

    def scores(k_rows, feat):
        return _dot(jnp.concatenate([k_rows.astype(BF16), feat], axis=1), q_aug)

    far = past_rows - tile_rows
    part = far // SAMPLE_FAR_PARTS
    half = win_rows // 2
    sel_rows = [slice(r0, r0 + part) for r0 in range(0, far, part)] + [slice(far, past_rows)]
    pending = [lambda rows=rows: scores(kbuf[slot, rows, :], feat_ref[rows, :]) for rows in sel_rows]
    pending += [lambda: scores(kst_ref[...], feat_ref[0:tail_rows, :]),
                lambda: scores(kw_ref[0:half, :], feat_ref[0:half, :]),
                lambda: scores(kw_ref[half:win_rows, :], feat_ref[half:win_rows, :]),
                lambda: scores(kwt_ref[...], feat_ref[0:tail_rows, :])]
    raw = []

    def issue(n=1):
        for _ in range(n):
            if len(raw) < len(pending):
                raw.append(pending[len(raw)]())

    s = scores(kc_ref[...], feat_ref[0:cmp_rows, :]) + ct_ref[...]
    issue()
    mx = jnp.max(s, axis=0, keepdims=True)
    mx = jnp.where(mx > 0.5 * NEG, mx, 0.0)
    e = jnp.where(s > 0.5 * NEG, jnp.exp2(s - mx), 0.0)
    p_c = e / jnp.maximum(jnp.sum(e, axis=0, keepdims=True), 1e-30)
    o_c = _pv(vc_ref[...], p_c)
    issue()

    l32 = lax.broadcasted_iota(jnp.int32, p_c.shape, 1) % (HPG * qb)
    psum = p_c
    for n in range(1, HPG):
        sh = n * qb
        psum = psum + jnp.where(l32 >= sh, pltpu.roll(p_c, sh, 1), pltpu.roll(p_c, sh + lanes - HPG * qb, 1))
    rows_per_block = SEL_BLOCK // CMP_STRIDE * N_KV
    nbi = cmp_rows // rows_per_block
    imp = jnp.sum(psum.reshape(nbi, rows_per_block, lanes), axis=1)
    imp = jnp.concatenate([imp, jnp.zeros((nb - nbi, lanes), F32)], axis=0)
    jb = lax.broadcasted_iota(jnp.int32, (nb, lanes), 0)
    cur = (past + lax.broadcasted_iota(jnp.int32, (nb, lanes), 1) % qb) // SEL_BLOCK
    forced = (jb == 0) | (jb == cur) | (jb == cur - 1)
    score = jnp.where(jb <= cur, jnp.where(forced, FORCE_SCORE, imp), NEG)
    jbf = jb.astype(F32)
    selm = jnp.full((nb, lanes), NEG, F32)
    for rnd in range(N_SEL):
        best = jnp.max(score, axis=0, keepdims=True)
        first = jnp.min(jnp.where(score == best, jbf, 1e9), axis=0, keepdims=True)
        pick = jbf == first
        selm = jnp.where(pick, 0.0, selm)
        score = jnp.where(pick, 3.0 * NEG, score)
        if rnd % 2 == 1:
            issue()
    issue(len(pending))
    sel_st = raw[:len(sel_rows)]
    sel_st[-1] = sel_st[-1] + img_ref[IMG_PREV]
    st_new, st_old, st_mid, st_wn = raw[len(sel_rows):]
    st_new = st_new + img_ref[IMG_DIAG, 0:tail_rows]
    st_old = jnp.concatenate([st_old[:tile_rows] + img_ref[IMG_EDGE], st_old[tile_rows:]], axis=0)
    st_mid = jnp.concatenate([st_mid[:half - tile_rows], st_mid[half - tile_rows:] + img_ref[IMG_PREV]], axis=0)
    st_wn = st_wn + img_ref[IMG_DIAG, 0:tail_rows]

    for j in range(nb):
        selm_ref[j] = jnp.broadcast_to(selm[j:j + 1, :], (8, lanes))
    block_rows = SEL_BLOCK * N_KV

    def select(st, row0):
        n_rows = st.shape[0]
        nblk = -(-n_rows // block_rows)
        per = n_rows // nblk // 8
        mask = selm_ref[row0 // block_rows:row0 // block_rows + nblk]
        return (st.reshape(nblk, per, 8, lanes) + mask[:, None]).reshape(n_rows, lanes)

    segs = [(select(st, rows.start), vbuf[slot, rows, :].astype(BF16)) for st, rows in zip(sel_st, sel_rows)]
    segs.append((select(st_new, past_rows), vst_ref[...].astype(BF16)))
    sel = _softmax_segments(segs)

    win = _softmax_segments([(st_old, vw_ref[0:half, :].astype(BF16)),
                             (st_mid, vw_ref[half:win_rows, :].astype(BF16)),
                             (st_wn, vwt_ref[...].astype(BF16))])

    o_r = _combine(gt_ref[...], o_c, sel, win, qb)
    for h in range(N_HEADS):
        o_ref[:, h * HEAD_DIM:(h + 1) * HEAD_DIM] = o_r[h * qb:(h + 1) * qb]

    for src, new, dst in ((kw_ref, kwt_ref, kwo_ref), (vw_ref, vwt_ref, vwo_ref)):
        dst[0:win_rows - tail_rows, :] = src[tail_rows:win_rows, :]
        dst[win_rows - tail_rows:win_rows, :] = new[...]


def _attn_prompt(qkv_f, qkv_b, gates_t, kcmp, vcmp, img, ct, seq, off2):
    qbp = KEY_TILE
    r_l = HPG * qbp
    n_tiles = seq // qbp
    nc = seq // CMP_STRIDE
    nb = seq // SEL_BLOCK
    q_cols = N_HEADS * HEAD_DIM // HEAD_DIM

    def kv_spec(which):
        return pl.BlockSpec((seq, HEAD_DIM), functools.partial(lambda g, i, c: (0, c + g), c=q_cols + which * N_KV))

    cmp_spec = pl.BlockSpec((nc, HEAD_DIM), lambda g, i: (0, g))
    return pl.pallas_call(
        functools.partial(_attn_prompt_kernel, nc=nc, nb=nb, off2=off2),
        grid=(N_KV, n_tiles),
        in_specs=[pl.BlockSpec((qbp, HPG * HEAD_DIM), lambda g, i: (i, g)),
                  pl.BlockSpec((None, None, 3, r_l), lambda g, i: (i, g, 0, 0)),
                  cmp_spec, cmp_spec,
                  kv_spec(2), kv_spec(3), kv_spec(4), kv_spec(5),
                  pl.BlockSpec((None, N_IMG, KEY_TILE, r_l), lambda g, i: (g, 0, 0, 0)),
                  pl.BlockSpec((None, ct.shape[1], r_l), lambda g, i: (g, 0, 0))],
        out_specs=pl.BlockSpec((qbp, HPG * HEAD_DIM), lambda g, i: (i, g)),
        out_shape=jax.ShapeDtypeStruct((seq, N_HEADS * HEAD_DIM), BF16),
        scratch_shapes=[pltpu.VMEM((nb * SEL_BLOCK // (SEL_GROUP * KEY_TILE), 16, r_l), BF16),
                        pltpu.VMEM((nc, KEY_TILE), F32)],
        compiler_params=_params(("arbitrary", "arbitrary"), 48),
        name="nsa_prompt",
    )(qkv_f, gates_t, kcmp, vcmp, qkv_b, qkv_b, qkv_b, qkv_b, img, ct)


def _key_features(n_rows):
    r = jnp.arange(n_rows, dtype=jnp.int32)[:, None]
    c = jnp.arange(KEY_TILE, dtype=jnp.int32)[None, :]
    hit = (c == r // (SEL_BLOCK * N_KV)) | (c == SAMPLE_GROUP_COL + r % N_KV)
    return hit.astype(BF16)


def _attn_sample(page_table, q_new, gates_t, kcmp, vcmp, kw, vw, ks_t, vs_t, kw_t, vw_t, img, ct,
                 ks_cache, vs_cache, qb):
    n_seq, n_pages = page_table.shape
    past = n_pages * PAGE
    lanes = N_HEADS * qb
    nb = 8 * (-(-(past // SEL_BLOCK + 2) // 8))
    assert nb <= SAMPLE_GROUP_COL and lanes == KEY_TILE
    qw = N_HEADS * HEAD_DIM
    tail_rows = ks_t.shape[1]
    cmp_rows = past // CMP_STRIDE * N_KV
    feat = _key_features(past * N_KV + tail_rows)
    tail_spec = pl.BlockSpec((None, tail_rows, HEAD_DIM), lambda b, pt: (b, 0, 0))
    win_spec = pl.BlockSpec((WINDOW * N_KV, HEAD_DIM), lambda b, pt: (b, 0))
    cmp_spec = pl.BlockSpec((cmp_rows, HEAD_DIM), lambda b, pt: (b, 0))
    any_spec = pl.BlockSpec(memory_space=pl.ANY)
    return pl.pallas_call(
        functools.partial(_attn_sample_kernel, qb=qb, nb=nb),
        grid_spec=pltpu.PrefetchScalarGridSpec(
            num_scalar_prefetch=1, grid=(n_seq,),
            in_specs=[pl.BlockSpec((qb, qw), lambda b, pt: (b, 0)),
                      pl.BlockSpec((None, 3, lanes), lambda b, pt: (b, 0, 0)),
                      cmp_spec, cmp_spec, win_spec, win_spec,
                      tail_spec, tail_spec, tail_spec, tail_spec,
                      pl.BlockSpec((None,) + img.shape[1:], lambda b, pt: (0, 0, 0, 0)),
                      pl.BlockSpec((None,) + ct.shape[1:], lambda b, pt: (0, 0, 0)),
                      pl.BlockSpec(feat.shape, lambda b, pt: (0, 0)),
                      any_spec, any_spec],
            out_specs=[pl.BlockSpec((qb, qw), lambda b, pt: (b, 0)), win_spec, win_spec],
            scratch_shapes=[pltpu.VMEM((nb, 8, lanes), F32),
                            pltpu.VMEM((2, n_pages * PAGE_ROWS, HEAD_DIM), F32),
                            pltpu.VMEM((2, n_pages * PAGE_ROWS, HEAD_DIM), F32),
                            pltpu.SemaphoreType.DMA((2, 2))]),
        out_shape=[jax.ShapeDtypeStruct((n_seq * qb, qw), F32),
                   jax.ShapeDtypeStruct(kw.shape, kw.dtype), jax.ShapeDtypeStruct(vw.shape, vw.dtype)],
        compiler_params=_params(("arbitrary",), 56),
        name="nsa_sample",
    )(page_table, q_new, gates_t, kcmp, vcmp, kw, vw, ks_t, vs_t, kw_t, vw_t, img, ct, feat, ks_cache, vs_cache)


def _gates_to_lanes(g_sig, n_tiles, q_rows, qbp):
    g5 = g_sig.reshape(n_tiles, q_rows, 3, N_KV, HPG)
    if qbp > q_rows:
        g5 = jnp.pad(g5, ((0, 0), (0, qbp - q_rows), (0, 0), (0, 0), (0, 0)))
    return g5.transpose(0, 3, 2, 4, 1).reshape(n_tiles, N_KV, 3, HPG * qbp)


def _history_rows(state, dec_seq):
    n_seq, _, c = state.shape
    z = jnp.zeros((n_seq, dec_seq - 2, c), state.dtype)
    s1 = jnp.concatenate([state[:, 1:2], jnp.zeros((n_seq, 1, c), state.dtype), z], axis=1)
    s2 = jnp.concatenate([state, z], axis=1)
    return s1.reshape(n_seq * dec_seq, c), s2.reshape(n_seq * dec_seq, c)


def _cmp_weights(pe, w1, w2):
    half = CMP_STRIDE * HEAD_DIM
    w1cat = jnp.concatenate([w1[:half], w1[half:]], axis=1).astype(BF16)
    return pe.reshape(2, half), w1cat, w2.astype(BF16)


def _token_group_rows(x):
    return x.reshape(-1, HEAD_DIM)


def kernel(x_prompt, x_sample, cache_k_cmp, cache_v_cmp, cache_k_sel, cache_v_sel, state_k_win, state_v_win,
           state_conv, state_ffn_conv, page_table, rel_bias, w_in, conv_w, conv_b, w_br_conv, w_br_nsa, w_out,
           pe_cmp_k, w_cmp_k1, w_cmp_k2, pe_cmp_v, w_cmp_v1, w_cmp_v2, ln1_g, ln1_b, w_ffn_in, ffn_conv_w,
           ffn_conv_b, w_ffn_out, ln2_g, ln2_b):
    seq = x_prompt.shape[1]
    n_seq, dec_seq = x_sample.shape[0], x_sample.shape[1]
    n_s = n_seq * dec_seq
    past = page_table.shape[1] * PAGE
    kvw = N_KV * HEAD_DIM
    qw = N_HEADS * HEAD_DIM
    tm = 1024

    x_s2 = x_sample.reshape(n_s, D_MODEL)
    x_bf = jnp.concatenate([x_prompt[0].astype(BF16), x_s2.astype(BF16)], axis=0)
    c_qkv = 3 * D_CONV
    c_gate = c_qkv + qw + 6 * kvw
    c_mix = c_gate + 3 * N_HEADS
    w_main = w_in[0].astype(BF16)
    w_gate = jnp.pad(w_main[:, c_gate:c_mix], ((0, 0), (0, 128 - 3 * N_HEADS)))
    w_mix = w_main[:, c_mix:]

    s1, s2 = _history_rows(state_conv[0], dec_seq)
    tn_c = 256
    z_conv, u_conv = _gconv(x_bf, w_main, (0, D_CONV // tn_c, 2 * D_CONV // tn_c), conv_w[0], conv_b, s1, s2,
                            seq, tm, tn_c, 48, "proj_conv")
    qkv_f, qkv_b = _proj(x_bf, w_main, c_qkv, qw + 6 * kvw, tm, 512)
    g_sig = _gate_proj(x_bf, w_gate, tm)[:, :3 * N_HEADS]

    pe_k, w1_k, w2_k = _cmp_weights(pe_cmp_k[0], w_cmp_k1[0], w_cmp_k2[0])
    pe_v, w1_v, w2_v = _cmp_weights(pe_cmp_v[0], w_cmp_v1[0], w_cmp_v2[0])
    chunk_w = CMP_STRIDE * kvw
    kc_p = qkv_b[:seq, qw:qw + kvw].reshape(seq // CMP_STRIDE, chunk_w)
    vc_p = qkv_b[:seq, qw + kvw:qw + 2 * kvw].reshape(seq // CMP_STRIDE, chunk_w)
    kcmp_p = _compress(kc_p, pe_k, w1_k, w2_k, "compress_k_prompt")
    vcmp_p = _compress(vc_p, pe_v, w1_v, w2_v, "compress_v_prompt")
    kcmp_s, vcmp_s = _compress_paged(page_table, _token_group_rows(cache_k_cmp), _token_group_rows(cache_v_cmp),
                                     (pe_k, w1_k, w2_k), (pe_v, w1_v, w2_v))

    off2_p = (seq - KEY_TILE) // CMP_STRIDE
    nct_p = KEY_TILE * (-(-(off2_p + seq // CMP_STRIDE) // KEY_TILE))
    img_p, ct_p = _bias_images(rel_bias, HPG, KEY_TILE, 1, nct_p, off2_p, "bias_prompt")
    gates_p = _gates_to_lanes(g_sig[:seq], seq // KEY_TILE, KEY_TILE, KEY_TILE)
    o_p = _attn_prompt(qkv_f, qkv_b, gates_p, kcmp_p, vcmp_p, img_p, ct_p, seq, off2_p)

    chunks_s = past // CMP_STRIDE
    img_s, ct_s = _bias_images(rel_bias, N_HEADS, dec_seq, N_KV, chunks_s, chunks_s, "bias_sample")
    gates_s = g_sig[seq:].reshape(n_seq, dec_seq, 3, N_HEADS).transpose(0, 2, 3, 1).reshape(n_seq, 3, N_HEADS * dec_seq)
    new_f = qkv_f[seq:].reshape(n_seq, dec_seq, qw + 6 * kvw)

    def tail(col):
        return new_f[:, :, col:col + kvw].reshape(n_seq, dec_seq * N_KV, HEAD_DIM)

    o_s, kw_next, vw_next = _attn_sample(page_table, qkv_f[seq:, :qw], gates_s,
                       kcmp_s.reshape(-1, HEAD_DIM), vcmp_s.reshape(-1, HEAD_DIM),
                       _token_group_rows(state_k_win), _token_group_rows(state_v_win),
                       tail(qw + 2 * kvw), tail(qw + 3 * kvw), tail(qw + 4 * kvw), tail(qw + 5 * kvw),
                       img_s, ct_s, _token_group_rows(cache_k_sel), _token_group_rows(cache_v_sel), dec_seq)

    mix = _merge(x_bf, z_conv, o_p, o_s, w_br_conv[0], w_br_nsa[0], w_mix, 512, 256)
    r1 = _resid_mm_split(mix, w_out[0].astype(BF16), x_prompt[0], x_s2, tm, 512, 48, "out_proj")
    h_f, h_b = _layer_norm(r1, ln1_g, ln1_b, 256, "ln1")
    f1, f2 = _history_rows(state_ffn_conv[0], dec_seq)
    tn_f = 256
    act, gp = _gconv(h_b, w_ffn_in[0], (D_FF // tn_f, 0), ffn_conv_w[0], ffn_conv_b, f1, f2,
                     seq, tm, tn_f, 56, "ffn_in")
    r2 = _resid_mm(act, w_ffn_out[0].astype(BF16), h_f, 512, 256, 56, "ffn_out")
    y_p, y_s = _layer_norm_split(r2, ln2_g, ln2_b, seq, 256, "ln2")

    def rows_p(col):
        return qkv_f[:seq, col:col + kvw].reshape(1, 1, seq, N_KV, HEAD_DIM)

    def rows_s(col):
        return new_f[:, :, col:col + kvw].reshape(1, n_seq, dec_seq, N_KV, HEAD_DIM)

    def win_p(col):
        return qkv_f[seq - WINDOW:seq, col:col + kvw].reshape(1, 1, WINDOW, N_KV, HEAD_DIM)


    def last2(u):
        c = u.shape[1]
        return (u[seq - 2:seq].reshape(1, 1, 2, c),
                u[seq:].reshape(n_seq, dec_seq, c)[:, dec_seq - 2:].reshape(1, n_seq, 2, c))

    conv_p, conv_s = last2(u_conv)
    ffn_p, ffn_s = last2(gp)
    c = qw
    return (y_p.reshape(1, seq, D_MODEL), y_s.reshape(n_seq, dec_seq, D_MODEL),
            rows_p(c), rows_s(c), rows_p(c + kvw), rows_s(c + kvw),
            rows_p(c + 2 * kvw), rows_s(c + 2 * kvw), rows_p(c + 3 * kvw), rows_s(c + 3 * kvw),
            win_p(c + 4 * kvw), kw_next.reshape(state_k_win.shape),
            win_p(c + 5 * kvw), vw_next.reshape(state_v_win.shape),
            conv_p, conv_s, ffn_p, ffn_s)
```

```python
import functools
import math

import jax
import jax.numpy as jnp
from jax import lax
from jax.experimental import pallas as pl
from jax.experimental.pallas import tpu as pltpu

F32 = jnp.float32
BF16 = jnp.bfloat16

D_MODEL = 4096
D_CONV = 2048
N_HEADS = 16
HEAD_DIM = 128
N_KV = 4
HPG = 4
CMP_STRIDE = 16
CMP_HID = 256
SEL_BLOCK = 64
N_SEL = 16
WINDOW = 512
FORCE_SCORE = 1e4
REL_BUCKETS = 32
REL_EXACT = 16
REL_MAX_DIST = 128
D_FF = 11008
PAGE = 128
ALPHA = 2.0 ** 0.25
LN_EPS = 1e-5
LOG2E = 1.0 / math.log(2.0)
Q_SCALE = HEAD_DIM ** -0.5 * LOG2E

KEY_TILE = 128
SEL_GROUP = 4
GCONV_PARTS = 2
PROMPT_TILES = 2
SEL_STATES = 3
PAGE_ROWS = PAGE * N_KV
NEG = -1e30
MIB = 1024 * 1024

IMG_DIAG, IMG_PREV, IMG_ZERO, IMG_EDGE, IMG_NEG = 0, 1, 2, 3, 4
N_IMG = 5


def _params(sem, vmem_mib):
    return pltpu.CompilerParams(dimension_semantics=sem, vmem_limit_bytes=vmem_mib * MIB)


def _dot(a, b):
    return jnp.dot(a, b, preferred_element_type=F32)


def _proj_kernel(x_ref, w_ref, of_ref, ob_ref):
    acc = _dot(x_ref[...], w_ref[...].astype(BF16))
    of_ref[...] = acc
    ob_ref[...] = acc.astype(BF16)


def _proj(x, w, col0, ncols, tm, tn):
    m_rows, k = x.shape
    c0 = col0 // tn
    return pl.pallas_call(
        _proj_kernel,
        grid=(m_rows // tm, ncols // tn),
        in_specs=[pl.BlockSpec((tm, k), lambda m, n: (m, 0)),
                  pl.BlockSpec((k, tn), lambda m, n: (0, c0 + n))],
        out_specs=[pl.BlockSpec((tm, tn), lambda m, n: (m, n)),
                   pl.BlockSpec((tm, tn), lambda m, n: (m, n))],
        out_shape=[jax.ShapeDtypeStruct((m_rows, ncols), F32),
                   jax.ShapeDtypeStruct((m_rows, ncols), BF16)],
        compiler_params=_params(("arbitrary", "arbitrary"), 56),
        name="proj_qkv",
    )(x, w)


def _gate_kernel(x_ref, w_ref, o_ref):
    o_ref[...] = jax.nn.sigmoid(_dot(x_ref[...], w_ref[...]))


def _gate_proj(x, w, tm):
    m_rows, k = x.shape
    n = w.shape[1]
    return pl.pallas_call(
        _gate_kernel,
        grid=(m_rows // tm,),
        in_specs=[pl.BlockSpec((tm, k), lambda m: (m, 0)),
                  pl.BlockSpec((k, n), lambda m: (0, 0))],
        out_specs=pl.BlockSpec((tm, n), lambda m: (m, 0)),
        out_shape=jax.ShapeDtypeStruct((m_rows, n), F32),
        compiler_params=_params(("arbitrary",), 40),
        name="proj_gates",
    )(x, w)


def _gconv_kernel(*refs, three, n_prompt_tiles):
    if three:
        x_ref, w0_ref, w1_ref, w2_ref, cw_ref, cb_ref, s1_ref, s2_ref, z_ref, u_ref, carry_ref = refs
    else:
        x_ref, w0_ref, w1_ref, cw_ref, cb_ref, s1_ref, s2_ref, z_ref, u_ref, carry_ref = refs
    m = pl.program_id(0)
    n = pl.program_id(1)
    is_sample = m >= n_prompt_tiles
    tm, tn = u_ref.shape
    hm = tm // GCONV_PARTS
    cw = cw_ref[...]
    cb = cb_ref[...]
    c = carry_ref[n]
    c = jnp.where(m == 0, jnp.zeros_like(c), c)
    prev_a, prev_b = c[6:7], c[7:8]
    row = lax.broadcasted_iota(jnp.int32, (hm, tn), 0)
    rmask = row & jnp.where(is_sample, 7, hm - 1)
    w0 = w0_ref[...].astype(BF16)
    w1 = w1_ref[...].astype(BF16)
    w2 = w2_ref[...].astype(BF16) if three else None

    def project(part):
        x = x_ref[part * hm:(part + 1) * hm, :]
        u = _dot(x, w1)
        if three:
            u = u * _dot(x, w2)
        p0 = _dot(x, w0)
        return p0, u

    def epilogue(part, p0, u, prev_a, prev_b):
        rows = slice(part * hm, (part + 1) * hm)
        u_ref[rows, :] = u
        fill1 = jnp.where(is_sample, s1_ref[rows, :], prev_b)
        fill2 = jnp.where(is_sample, s2_ref[rows, :], jnp.where(row == 0, prev_a, prev_b))
        prev1 = jnp.where(rmask == 0, fill1, pltpu.roll(u, 1, 0))
        prev2 = jnp.where(rmask < 2, fill2, pltpu.roll(u, 2, 0))
        conv = cb + prev2 * cw[0:1] + prev1 * cw[1:2] + u * cw[2:3]
        if three:
            z = p0 * conv
        else:
            z = jax.nn.silu(conv) * p0
        z_ref[rows, :] = z.astype(z_ref.dtype)
        return u[hm - 2:hm - 1], u[hm - 1:hm]

    done = project(0)
    for part in range(1, GCONV_PARTS):
        nxt = project(part)
        prev_a, prev_b = epilogue(part - 1, *done, prev_a, prev_b)
        done = nxt
    epilogue(GCONV_PARTS - 1, *done, prev_a, prev_b)
    carry_ref[n] = done[1][hm - 8:hm]


def _gconv(x, w, col_blocks, cw, cb, s1, s2, n_prompt_rows, tm, tn, vmem_mib, name):
    m_rows, k = x.shape
    ncols = cw.shape[1]
    three = len(col_blocks) == 3
    assert m_rows - n_prompt_rows == tm and n_prompt_rows % tm == 0
    nt = ncols // tn
    w_specs = [pl.BlockSpec((k, tn), functools.partial(lambda m, n, c: (0, c + n), c=c)) for c in col_blocks]
    col_spec = pl.BlockSpec((tm, tn), lambda m, n: (m, n))
    npt = n_prompt_rows // tm
    hist_spec = pl.BlockSpec((tm, tn), lambda m, n: (0, jnp.where(m < npt, 0, n)))
    return pl.pallas_call(
        functools.partial(_gconv_kernel, three=three, n_prompt_tiles=npt),
        grid=(m_rows // tm, nt),
        in_specs=[pl.BlockSpec((tm, k), lambda m, n: (m, 0))] + w_specs + [
            pl.BlockSpec((3, tn), lambda m, n: (0, n)),
            pl.BlockSpec((1, tn), lambda m, n: (0, n)),
            hist_spec, hist_spec],
        out_specs=[col_spec, col_spec],
        out_shape=[jax.ShapeDtypeStruct((m_rows, ncols), BF16),
                   jax.ShapeDtypeStruct((m_rows, ncols), F32)],
        scratch_shapes=[pltpu.VMEM((nt, 8, tn), F32)],
        compiler_params=_params(("arbitrary", "arbitrary"), vmem_mib),
        name=name,
    )(x, *([w] * len(col_blocks)), cw, cb, s1, s2)


def _merge_kernel(x_ref, z_ref, op_ref, os_ref, wc_ref, wn_ref, wga_ref, wgb_ref, out_ref, *, n_prompt_tiles):
    x = x_ref[...]
    o = jnp.where(pl.program_id(0) < n_prompt_tiles, op_ref[...], os_ref[...].astype(BF16))
    a = _dot(z_ref[...], wc_ref[...].astype(BF16))
    b = _dot(o, wn_ref[...].astype(BF16))
    ga = jax.nn.sigmoid(_dot(x, wga_ref[...]))
    gb = jax.nn.sigmoid(_dot(x, wgb_ref[...]))
    out_ref[...] = (ga * a + gb * b).astype(BF16)


def _merge(x, z, o_p, o_s, wc, wn, wg, tm, tn):
    m_rows, k = x.shape
    kc = z.shape[1]
    nt = D_MODEL // tn
    npt = o_p.shape[0] // tm
    assert npt * tm == o_p.shape[0] and o_s.shape[0] % tm == 0 and o_p.shape[0] + o_s.shape[0] == m_rows
    return pl.pallas_call(
        functools.partial(_merge_kernel, n_prompt_tiles=npt),
        grid=(m_rows // tm, nt),
        in_specs=[pl.BlockSpec((tm, k), lambda m, n: (m, 0)),
                  pl.BlockSpec((tm, kc), lambda m, n: (m, 0)),
                  pl.BlockSpec((tm, kc), lambda m, n: (jnp.minimum(m, npt - 1), 0)),
                  pl.BlockSpec((tm, kc), lambda m, n: (jnp.maximum(m - npt, 0), 0)),
                  pl.BlockSpec((kc, tn), lambda m, n: (0, n)),
                  pl.BlockSpec((kc, tn), lambda m, n: (0, n)),
                  pl.BlockSpec((k, tn), lambda m, n: (0, n)),
                  pl.BlockSpec((k, tn), lambda m, n: (0, nt + n))],
        out_specs=pl.BlockSpec((tm, tn), lambda m, n: (m, n)),
        out_shape=jax.ShapeDtypeStruct((m_rows, D_MODEL), BF16),
        compiler_params=_params(("arbitrary", "arbitrary"), 56),
        name="merge_branches",
    )(x, z, o_p, o_s, wc, wn, wg, wg)


def _resid_kernel(l_ref, w_ref, r_ref, o_ref):
    o_ref[...] = ALPHA * r_ref[...] + _dot(l_ref[...], w_ref[...])


def _resid2_kernel(l_ref, w_ref, rp_ref, rs_ref, o_ref, *, n_prompt_tiles):
    res = jnp.where(pl.program_id(0) < n_prompt_tiles, rp_ref[...], rs_ref[...])
    o_ref[...] = ALPHA * res + _dot(l_ref[...], w_ref[...].astype(BF16))


def _resid_mm_split(lhs, w, res_p, res_s, tm, tn, vmem_mib, name):
    m_rows, k = lhs.shape
    n_cols = w.shape[1]
    npt = res_p.shape[0] // tm
    assert res_s.shape[0] == tm and npt * tm + tm == m_rows
    return pl.pallas_call(
        functools.partial(_resid2_kernel, n_prompt_tiles=npt),
        grid=(m_rows // tm, n_cols // tn),
        in_specs=[pl.BlockSpec((tm, k), lambda m, n: (m, 0)),
                  pl.BlockSpec((k, tn), lambda m, n: (0, n)),
                  pl.BlockSpec((tm, tn), lambda m, n: (jnp.minimum(m, npt - 1), n)),
                  pl.BlockSpec((tm, tn), lambda m, n: (0, n))],
        out_specs=pl.BlockSpec((tm, tn), lambda m, n: (m, n)),
        out_shape=jax.ShapeDtypeStruct((m_rows, n_cols), F32),
        compiler_params=_params(("arbitrary", "arbitrary"), vmem_mib),
        name=name,
    )(lhs, w, res_p, res_s)


def _resid_mm(lhs, w, res, tm, tn, vmem_mib, name):
    m_rows, k = lhs.shape
    n_cols = w.shape[1]
    return pl.pallas_call(
        _resid_kernel,
        grid=(m_rows // tm, n_cols // tn),
        in_specs=[pl.BlockSpec((tm, k), lambda m, n: (m, 0)),
                  pl.BlockSpec((k, tn), lambda m, n: (0, n)),
                  pl.BlockSpec((tm, tn), lambda m, n: (m, n))],
        out_specs=pl.BlockSpec((tm, tn), lambda m, n: (m, n)),
        out_shape=jax.ShapeDtypeStruct((m_rows, n_cols), F32),
        compiler_params=_params(("arbitrary", "arbitrary"), vmem_mib),
        name=name,
    )(lhs, w, res)


def _ln_kernel(x_ref, g_ref, b_ref, of_ref, ob_ref):
    x = x_ref[...]
    mu = jnp.mean(x, axis=-1, keepdims=True)
    xc = x - mu
    var = jnp.mean(xc * xc, axis=-1, keepdims=True)
    y = xc * lax.rsqrt(var + LN_EPS) * g_ref[...] + b_ref[...]
    of_ref[...] = y
    ob_ref[...] = y.astype(BF16)


def _layer_norm(x, g, b, tr, name):
    m_rows, d = x.shape
    row_spec = pl.BlockSpec((tr, d), lambda m: (m, 0))
    vec_spec = pl.BlockSpec((1, d), lambda m: (0, 0))
    return pl.pallas_call(
        _ln_kernel,
        grid=(m_rows // tr,),
        in_specs=[row_spec, vec_spec, vec_spec],
        out_specs=[row_spec, row_spec],
        out_shape=[jax.ShapeDtypeStruct((m_rows, d), F32), jax.ShapeDtypeStruct((m_rows, d), BF16)],
        compiler_params=_params(("arbitrary",), 48),
        name=name,
    )(x, g, b)


def _ln_split_kernel(x_ref, g_ref, b_ref, yp_ref, ys_ref, *, n_prompt_tiles):
    x = x_ref[...]
    mu = jnp.mean(x, axis=-1, keepdims=True)
    xc = x - mu
    var = jnp.mean(xc * xc, axis=-1, keepdims=True)
    y = xc * lax.rsqrt(var + LN_EPS) * g_ref[...] + b_ref[...]
    m = pl.program_id(0)

    @pl.when(m < n_prompt_tiles)
    def _():
        yp_ref[...] = y

    @pl.when(m >= n_prompt_tiles)
    def _():
        ys_ref[...] = y


def _layer_norm_split(x, g, b, n_prompt_rows, tr, name):
    m_rows, d = x.shape
    npt = n_prompt_rows // tr
    vec_spec = pl.BlockSpec((1, d), lambda m: (0, 0))
    return pl.pallas_call(
        functools.partial(_ln_split_kernel, n_prompt_tiles=npt),
        grid=(m_rows // tr,),
        in_specs=[pl.BlockSpec((tr, d), lambda m: (m, 0)), vec_spec, vec_spec],
        out_specs=[pl.BlockSpec((tr, d), lambda m: (jnp.minimum(m, npt - 1), 0)),
                   pl.BlockSpec((tr, d), lambda m: (jnp.maximum(m - npt, 0), 0))],
        out_shape=[jax.ShapeDtypeStruct((n_prompt_rows, d), F32),
                   jax.ShapeDtypeStruct((m_rows - n_prompt_rows, d), F32)],
        compiler_params=_params(("arbitrary",), 48),
        name=name,
    )(x, g, b)


def _page_copies(pt_ref, seq, cache_ref, buf, slot, sem):
    return [pltpu.make_async_copy(
        cache_ref.at[pl.ds(pl.multiple_of(pt_ref[seq, j] * PAGE_ROWS, PAGE_ROWS), PAGE_ROWS)],
        buf.at[slot, pl.ds(j * PAGE_ROWS, PAGE_ROWS)], sem)
        for j in range(pt_ref.shape[1])]


def _prefetch_pages(pt_ref, b, n_seq, streams):
    slot = b % 2

    @pl.when(b == 0)
    def _():
        for cache_ref, buf, sem in streams:
            for cp in _page_copies(pt_ref, 0, cache_ref, buf, 0, sem.at[0]):
                cp.start()

    @pl.when(b + 1 < n_seq)
    def _():
        for cache_ref, buf, sem in streams:
            for cp in _page_copies(pt_ref, b + 1, cache_ref, buf, 1 - slot, sem.at[1 - slot]):
                cp.start()

    for cache_ref, buf, sem in streams:
        for cp in _page_copies(pt_ref, b, cache_ref, buf, slot, sem.at[slot]):
            cp.wait()
    return slot


def _compress_rows(xg, pe, w1, w2, next_chunk=1):
    return _compress_finish(_compress_first_layer(xg, pe, w1), w2, next_chunk)


def _compress_first_layer(xg, pe, w1):
    half = CMP_STRIDE * HEAD_DIM
    pa = _dot(jnp.broadcast_to(pe[0:1], (8, half)).astype(BF16), w1[:, :CMP_HID])[0:1]
    pb = _dot(jnp.broadcast_to(pe[1:2], (8, half)).astype(BF16), w1[:, CMP_HID:])[0:1]
    fs = _dot(xg, w1)
    return fs[:, :CMP_HID] + pa, fs[:, CMP_HID:] + pb


def _compress_finish(halves, w2, next_chunk):
    first, second = halves
    hid = jax.nn.gelu(first + pltpu.roll(second, second.shape[0] - next_chunk, 0))
    return _dot(hid.astype(BF16), w2)


def _compress_kernel(x_ref, pe_ref, w1_ref, w2_ref, o_ref):
    row_w = N_KV * HEAD_DIM
    for g in range(N_KV):
        xg = jnp.concatenate(
            [x_ref[:, r * row_w + g * HEAD_DIM: r * row_w + (g + 1) * HEAD_DIM] for r in range(CMP_STRIDE)],
            axis=1).astype(BF16)
        o_ref[:, g * HEAD_DIM:(g + 1) * HEAD_DIM] = _compress_rows(
            xg, pe_ref[...], w1_ref[...], w2_ref[...]).astype(BF16)


def _compress(x2, pe2, w1cat, w2, name):
    rows, width = x2.shape
    return pl.pallas_call(
        _compress_kernel,
        grid=(1,),
        in_specs=[pl.BlockSpec((rows, width), lambda t: (0, 0)),
                  pl.BlockSpec(pe2.shape, lambda t: (0, 0)),
                  pl.BlockSpec(w1cat.shape, lambda t: (0, 0)),
                  pl.BlockSpec(w2.shape, lambda t: (0, 0))],
        out_specs=pl.BlockSpec((rows, N_KV * HEAD_DIM), lambda t: (0, 0)),
        out_shape=jax.ShapeDtypeStruct((rows, N_KV * HEAD_DIM), BF16),
        compiler_params=_params(("arbitrary",), 48),
        name=name,
    )(x2, pe2, w1cat, w2)


def _compress_paged_kernel(pt_ref, kc_hbm, vc_hbm, pek_ref, w1k_ref, w2k_ref, pev_ref, w1v_ref, w2v_ref,
                           ok_ref, ov_ref, kbuf, vbuf, sem, *, chunks):
    b = pl.program_id(0)
    slot = _prefetch_pages(pt_ref, b, pl.num_programs(0), [(kc_hbm, kbuf, sem.at[0]), (vc_hbm, vbuf, sem.at[1])])

    def chunk_rows(buf):
        x3 = buf[slot].reshape(chunks, CMP_STRIDE * N_KV, HEAD_DIM)
        return jnp.concatenate(
            [x3[:, r * N_KV:(r + 1) * N_KV, :].reshape(chunks * N_KV, HEAD_DIM) for r in range(CMP_STRIDE)],
            axis=1).astype(BF16)

    hk = _compress_first_layer(chunk_rows(kbuf), pek_ref[...], w1k_ref[...])
    hv = _compress_first_layer(chunk_rows(vbuf), pev_ref[...], w1v_ref[...])
    ok_ref[...] = _compress_finish(hk, w2k_ref[...], N_KV).astype(BF16)
    ov_ref[...] = _compress_finish(hv, w2v_ref[...], N_KV).astype(BF16)


def _compress_paged(page_table, kc_cache, vc_cache, wk, wv):
    n_seq, n_pages = page_table.shape
    chunks = n_pages * PAGE // CMP_STRIDE
    w_specs = [pl.BlockSpec(w.shape, lambda b, pt: (0, 0)) for w in wk + wv]
    out_spec = pl.BlockSpec((chunks * N_KV, HEAD_DIM), lambda b, pt: (b, 0))
    out_shape = jax.ShapeDtypeStruct((n_seq * chunks * N_KV, HEAD_DIM), BF16)
    buf = pltpu.VMEM((2, n_pages * PAGE_ROWS, HEAD_DIM), F32)
    return pl.pallas_call(
        functools.partial(_compress_paged_kernel, chunks=chunks),
        grid_spec=pltpu.PrefetchScalarGridSpec(
            num_scalar_prefetch=1, grid=(n_seq,),
            in_specs=[pl.BlockSpec(memory_space=pl.ANY), pl.BlockSpec(memory_space=pl.ANY)] + w_specs,
            out_specs=[out_spec, out_spec],
            scratch_shapes=[buf, buf, pltpu.SemaphoreType.DMA((2, 2))]),
        out_shape=[out_shape, out_shape],
        compiler_params=_params(("arbitrary",), 48),
        name="compress_sample",
    )(page_table, kc_cache, vc_cache, *wk, *wv)


def _bias_kernel(rb_ref, img_ref, ct_ref, *, heads, qbp, rpk, off2):
    g = pl.program_id(0)
    r_l = heads * qbp
    lane = lax.broadcasted_iota(jnp.int32, (1, r_l), 1)
    n_l = lane // qbp
    tbl = []
    for k in range(REL_BUCKETS):
        row = jnp.zeros((1, r_l), F32)
        for n in range(heads):
            row = jnp.where(n_l == n, rb_ref[k, g * heads + n], row)
        tbl.append(row)

    def bias_of(dist):
        nn = jnp.maximum(dist, 0)
        nf = jnp.maximum(nn, 1).astype(F32)
        large = REL_EXACT + (jnp.log(nf / REL_EXACT) / math.log(REL_MAX_DIST / REL_EXACT)
                             * (REL_BUCKETS - REL_EXACT)).astype(jnp.int32)
        bucket = jnp.where(nn < REL_EXACT, nn, jnp.minimum(large, REL_BUCKETS - 1))
        val = jnp.zeros(dist.shape, F32)
        for k in range(REL_BUCKETS):
            val = jnp.where(bucket == k, tbl[k], val)
        return val

    shape = (KEY_TILE, r_l)
    kj = lax.broadcasted_iota(jnp.int32, shape, 0) // rpk
    qi = lax.broadcasted_iota(jnp.int32, shape, 1) % qbp
    far = tbl[REL_BUCKETS - 1]
    for r0 in range(0, rpk * KEY_TILE, KEY_TILE):
        rows = slice(r0, r0 + KEY_TILE)
        d0 = qi - (kj + r0 // rpk)
        img_ref[IMG_DIAG, rows] = jnp.where(d0 >= 0, (bias_of(d0) - far) * LOG2E, NEG)
        img_ref[IMG_PREV, rows] = (bias_of(d0 + KEY_TILE) - far) * LOG2E
        img_ref[IMG_ZERO, rows] = jnp.zeros(shape, F32)
        img_ref[IMG_EDGE, rows] = jnp.where(d0 < 0, 0.0, NEG)
        img_ref[IMG_NEG, rows] = jnp.full(shape, NEG, F32)
    for r0 in range(0, ct_ref.shape[0], KEY_TILE):
        dist = qi + CMP_STRIDE * (off2 - r0 // rpk - kj) - (2 * CMP_STRIDE - 1)
        ct_ref[r0:r0 + KEY_TILE, :] = jnp.where(dist >= 0, bias_of(dist) * LOG2E, NEG)


def _bias_images(rel_bias, heads, qbp, rpk, nct, off2, name):
    r_l = heads * qbp
    groups = N_HEADS // heads
    return pl.pallas_call(
        functools.partial(_bias_kernel, heads=heads, qbp=qbp, rpk=rpk, off2=off2),
        grid=(groups,),
        in_specs=[pl.BlockSpec(memory_space=pltpu.SMEM)],
        out_specs=[pl.BlockSpec((None, N_IMG, rpk * KEY_TILE, r_l), lambda g: (g, 0, 0, 0)),
                   pl.BlockSpec((None, rpk * nct, r_l), lambda g: (g, 0, 0))],
        out_shape=[jax.ShapeDtypeStruct((groups, N_IMG, rpk * KEY_TILE, r_l), F32),
                   jax.ShapeDtypeStruct((groups, rpk * nct, r_l), F32)],
        compiler_params=_params(("arbitrary",), 32),
        name=name,
    )(rel_bias)


def _pv(v, p):
    return lax.dot_general(v, p.astype(BF16), (((0,), (0,)), ((), ())), preferred_element_type=F32)


def _online_update(carry, st, v):
    m_run, l_run, acc = carry
    m_new = jnp.maximum(m_run, jnp.max(st, axis=0, keepdims=True))
    alpha = jnp.exp2(m_run - m_new)
    p = jnp.exp2(st - m_new)
    l_new = alpha * l_run + jnp.sum(p, axis=0, keepdims=True)
    return m_new, l_new, alpha * acc + _pv(v, p)


def _softmax_init(r_l):
    return jnp.full((1, r_l), NEG, F32), jnp.zeros((1, r_l), F32), jnp.zeros((HEAD_DIM, r_l), F32)


def _block_onehot(n_keys):
    kj = lax.broadcasted_iota(jnp.int32, (n_keys, KEY_TILE), 0)
    col = lax.broadcasted_iota(jnp.int32, (n_keys, KEY_TILE), 1)
    return jnp.where(col == kj // SEL_BLOCK, 1.0, 0.0).astype(BF16)


def _query_lanes(q):
    q = q * Q_SCALE
    qg = jnp.concatenate([q[:, n * HEAD_DIM:(n + 1) * HEAD_DIM] for n in range(HPG)], axis=0)
    return qg.T.astype(BF16)


def _cmp_and_select(q_t, kc, vc, cbias, t0, ps_ref, *, qbp, nc, nb, side_work=None):
    r_l = HPG * qbp
    s = _dot(kc, q_t) + cbias
    valid = cbias > 0.5 * NEG
    mx = jnp.max(s, axis=0, keepdims=True)
    mx = jnp.where(mx > 0.5 * NEG, mx, 0.0)
    e = jnp.where(valid, jnp.exp2(s - mx), 0.0)
    p_c = e / jnp.maximum(jnp.sum(e, axis=0, keepdims=True), 1e-30)
    o_c = _pv(vc, p_c)

    if qbp == KEY_TILE:
        psum = p_c[:, 0:qbp]
        for n in range(1, HPG):
            psum = psum + p_c[:, n * qbp:(n + 1) * qbp]
    else:
        psum = p_c
        for n in range(1, HPG):
            psum = psum + pltpu.roll(p_c, n * qbp, 1)
    ps_ref[...] = psum
    per_sel = SEL_BLOCK // CMP_STRIDE
    nbi = nc // per_sel
    imp = ps_ref[pl.ds(0, nbi, stride=per_sel), :]
    for c in range(1, per_sel):
        imp = imp + ps_ref[pl.ds(c, nbi, stride=per_sel), :]
    if nb > nbi:
        imp = jnp.concatenate([imp, jnp.zeros((nb - nbi, KEY_TILE), F32)], axis=0)
    jb = lax.broadcasted_iota(jnp.int32, (nb, KEY_TILE), 0)
    tq = t0 + lax.broadcasted_iota(jnp.int32, (nb, KEY_TILE), 1) % qbp
    cur = tq // SEL_BLOCK
    forced = (jb == 0) | (jb == cur) | (jb == cur - 1)
    score = jnp.where(jb <= cur, jnp.where(forced, FORCE_SCORE, imp), NEG)
    jbf = jb.astype(F32)
    side = side_work() if side_work is not None else None
    selm = jnp.full((nb, KEY_TILE), NEG, F32)
    for _ in range(N_SEL):
        best = jnp.max(score, axis=0, keepdims=True)
        first = jnp.min(jnp.where(score == best, jbf, 1e9), axis=0, keepdims=True)
        pick = jbf == first
        selm = jnp.where(pick, 0.0, selm)
        score = jnp.where(pick, 3.0 * NEG, score)
    if r_l > KEY_TILE:
        selm = jnp.concatenate([selm] * (r_l // KEY_TILE), axis=1)
    return o_c, selm, side


def _combine(gt, o_c, sel, win, qbp):
    (_, l_s, acc_s), (_, l_w, acc_w) = sel, win
    o_t = gt[0:1] * o_c + gt[1:2] * (acc_s / l_s) + gt[2:3] * (acc_w / l_w)
    return o_t.T


def _attn_prompt_kernel(q_ref, gt_ref, kc_ref, vc_ref, ks_ref, vs_ref, kw_ref, vw_ref, img_ref, ct_ref,
                        o_ref, selg_ref, ps_ref, *, nc, nb, off2):
    qbp = KEY_TILE
    r_l = HPG * qbp
    n_win = WINDOW // KEY_TILE
    grp_keys = SEL_GROUP * KEY_TILE
    grp_blocks = grp_keys // SEL_BLOCK
    n_grp, pad_rows = selg_ref.shape[1], selg_ref.shape[2] - grp_blocks
    onehot = _block_onehot(grp_keys)
    zpad = jnp.zeros((KEY_TILE - selg_ref.shape[2], r_l), BF16)

    def img_of(d, edge):
        idx = jnp.minimum(d, IMG_ZERO)
        if edge:
            idx = jnp.where(d == n_win, IMG_EDGE, idx)
        return img_ref[jnp.where(d < 0, IMG_NEG, idx)]

    def grp_rows(it):
        return pl.ds(pl.multiple_of(it * grp_keys, grp_keys), grp_keys)

    def before_selection(t):
        i_tile = PROMPT_TILES * pl.program_id(1) + t
        q_t = _query_lanes(q_ref[t * qbp:(t + 1) * qbp, :])

        def window():
            first_tile = jnp.maximum(i_tile - n_win, 0)
            rows = pl.ds(pl.multiple_of(first_tile * KEY_TILE, KEY_TILE), (n_win + 1) * KEY_TILE)
            add = jnp.concatenate([img_of(i_tile - (first_tile + w), True) for w in range(n_win + 1)], axis=0)
            return _online_update(_softmax_init(r_l), _dot(kw_ref[rows, :], q_t) + add, vw_ref[rows, :])

        start = pl.multiple_of(off2 - i_tile * (KEY_TILE // CMP_STRIDE), 8)
        o_c, selm, win = _cmp_and_select(q_t, kc_ref[...], vc_ref[...], ct_ref[pl.ds(start, nc), :],
                                         i_tile * KEY_TILE, ps_ref.at[t], qbp=qbp, nc=nc, nb=nb, side_work=window)
        selg_ref[t] = jnp.concatenate(
            [selm.reshape(n_grp, grp_blocks, r_l), jnp.zeros((n_grp, pad_rows, r_l), F32)], axis=1).astype(BF16)
        return i_tile, q_t, o_c, win

    def selected_branch(t, i_tile, q_t):
        n_it = (i_tile + SEL_GROUP) // SEL_GROUP
        n_far = jnp.maximum(i_tile - 1, 0) // SEL_GROUP

        def scores(it):
            k_aug = jnp.concatenate([ks_ref[grp_rows(it), :], onehot], axis=1)
            q_aug = jnp.concatenate([q_t, selg_ref[t, it], zpad], axis=0)
            return _dot(k_aug, q_aug)

        def multi_body(itn, states):
            sts = [scores(SEL_STATES * itn + k) for k in range(SEL_STATES)]
            return tuple(_online_update(states[k], sts[k], vs_ref[grp_rows(SEL_STATES * itn + k), :])
                         for k in range(SEL_STATES))

        def single_body(it, state):
            st = scores(it) + jnp.concatenate(
                [img_of(i_tile - (it * SEL_GROUP + w), False) for w in range(SEL_GROUP)], axis=0)
            return _online_update(state, st, vs_ref[grp_rows(it), :])

        n_multi = n_far // SEL_STATES
        states = lax.fori_loop(0, n_multi, multi_body, tuple(_softmax_init(r_l) for _ in range(SEL_STATES)))
        states = (lax.fori_loop(SEL_STATES * n_multi, n_it, single_body, states[0]),) + states[1:]
        m_all = states[0][0]
        for st_k in states[1:]:
            m_all = jnp.maximum(m_all, st_k[0])
        l_all, acc_all = None, None
        for m_k, l_k, acc_k in states:
            f_k = jnp.exp2(m_k - m_all)
            l_all = f_k * l_k if l_all is None else l_all + f_k * l_k
            acc_all = f_k * acc_k if acc_all is None else acc_all + f_k * acc_k
        return m_all, l_all, acc_all

    heads = [before_selection(t) for t in range(PROMPT_TILES)]
    for t, (i_tile, q_t, o_c, win) in enumerate(heads):
        sel = selected_branch(t, i_tile, q_t)
        o_r = _combine(gt_ref[t], o_c, sel, win, qbp)
        for n in range(HPG):
            o_ref[t * qbp:(t + 1) * qbp, n * HEAD_DIM:(n + 1) * HEAD_DIM] = \
                o_r[n * qbp:(n + 1) * qbp].astype(o_ref.dtype)


SAMPLE_GROUP_COL = 64
SAMPLE_FAR_PARTS = 6


def _softmax_segments(segs):
    m = jnp.max(segs[0][0], axis=0, keepdims=True)
    for st, _ in segs[1:]:
        m = jnp.maximum(m, jnp.max(st, axis=0, keepdims=True))
    l_sum, acc = None, None
    for st, v in segs:
        p = jnp.exp2(st - m)
        l_part, a_part = jnp.sum(p, axis=0, keepdims=True), _pv(v, p)
        l_sum = l_part if l_sum is None else l_sum + l_part
        acc = a_part if acc is None else acc + a_part
    return m, l_sum, acc


def _attn_sample_kernel(pt_ref, q_ref, gt_ref, kc_ref, vc_ref, kw_ref, vw_ref, kst_ref, vst_ref, kwt_ref, vwt_ref,
                        img_ref, ct_ref, feat_ref, ks_hbm, vs_hbm, o_ref, kwo_ref, vwo_ref, selm_ref, kbuf, vbuf,
                        sem, *, qb, nb):
    b = pl.program_id(0)
    slot = _prefetch_pages(pt_ref, b, pl.num_programs(0), [(ks_hbm, kbuf, sem.at[0]), (vs_hbm, vbuf, sem.at[1])])
    lanes = N_HEADS * qb
    past_rows = kbuf.shape[1]
    past = past_rows // N_KV
    tile_rows = KEY_TILE * N_KV
    tail_rows = kst_ref.shape[0]
    win_rows = kw_ref.shape[0]
    cmp_rows = kc_ref.shape[0]

    q = q_ref[...] * Q_SCALE
    q_t = jnp.concatenate([q[:, h * HEAD_DIM:(h + 1) * HEAD_DIM] for h in range(N_HEADS)], axis=0).T.astype(BF16)
    lane = lax.broadcasted_iota(jnp.int32, (N_KV, lanes), 1)
    grp_rows = jnp.where(lane // (HPG * qb) == lax.broadcasted_iota(jnp.int32, (N_KV, lanes), 0), 0.0, NEG)

    q_aug = jnp.concatenate(
        [q_t, jnp.concatenate([jnp.zeros((SAMPLE_GROUP_COL, lanes), F32), grp_rows,
                               jnp.zeros((KEY_TILE - SAMPLE_GROUP_COL - N_KV, lanes), F32)], axis=0).astype(BF16)],
        axis=0)

    def scores(k_rows, feat):
        return _dot(jnp.concatenate([k_rows.astype(BF16), feat], axis=1), q_aug)

    far = past_rows - tile_rows
    part = far // SAMPLE_FAR_PARTS
    half = win_rows // 2
    sel_rows = [slice(r0, r0 + part) for r0 in range(0, far, part)] + [slice(far, past_rows)]
    pending = [lambda rows=rows: scores(kbuf[slot, rows, :], feat_ref[rows, :]) for rows in sel_rows]
    pending += [lambda: scores(kst_ref[...], feat_ref[0:tail_rows, :]),
                lambda: scores(kw_ref[0:half, :], feat_ref[0:half, :]),
                lambda: scores(kw_ref[half:win_rows, :], feat_ref[half:win_rows, :]),
                lambda: scores(kwt_ref[...], feat_ref[0:tail_rows, :])]
    raw = []

    def issue(n=1):
        for _ in range(n):
            if len(raw) < len(pending):
                raw.append(pending[len(raw)]())

    s = scores(kc_ref[...], feat_ref[0:cmp_rows, :]) + ct_ref[...]
    issue()
    mx = jnp.max(s, axis=0, keepdims=True)
    mx = jnp.where(mx > 0.5 * NEG, mx, 0.0)
    e = jnp.where(s > 0.5 * NEG, jnp.exp2(s - mx), 0.0)
    p_c = e / jnp.maximum(jnp.sum(e, axis=0, keepdims=True), 1e-30)
    o_c = _pv(vc_ref[...], p_c)
    issue()

    l32 = lax.broadcasted_iota(jnp.int32, p_c.shape, 1) % (HPG * qb)
    psum = p_c
    for n in range(1, HPG):
        sh = n * qb
        psum = psum + jnp.where(l32 >= sh, pltpu.roll(p_c, sh, 1), pltpu.roll(p_c, sh + lanes - HPG * qb, 1))
    rows_per_block = SEL_BLOCK // CMP_STRIDE * N_KV
    nbi = cmp_rows // rows_per_block
    imp = jnp.sum(psum.reshape(nbi, rows_per_block, lanes), axis=1)
    imp = jnp.concatenate([imp, jnp.zeros((nb - nbi, lanes), F32)], axis=0)
    jb = lax.broadcasted_iota(jnp.int32, (nb, lanes), 0)
    cur = (past + lax.broadcasted_iota(jnp.int32, (nb, lanes), 1) % qb) // SEL_BLOCK
    forced = (jb == 0) | (jb == cur) | (jb == cur - 1)
    score = jnp.where(jb <= cur, jnp.where(forced, FORCE_SCORE, imp), NEG)
    jbf = jb.astype(F32)
    selm = jnp.full((nb, lanes), NEG, F32)
    for rnd in range(N_SEL):
        best = jnp.max(score, axis=0, keepdims=True)
        first = jnp.min(jnp.where(score == best, jbf, 1e9), axis=0, keepdims=True)
        pick = jbf == first
        selm = jnp.where(pick, 0.0, selm)
        score = jnp.where(pick, 3.0 * NEG, score)
        if rnd % 2 == 1:
            issue()
    issue(len(pending))
    sel_st = raw[:len(sel_rows)]
    sel_st[-1] = sel_st[-1] + img_ref[IMG_PREV]
    st_new, st_old, st_mid, st_wn = raw[len(sel_rows):]
    st_new = st_new + img_ref[IMG_DIAG, 0:tail_rows]
    st_old = jnp.concatenate([st_old[:tile_rows] + img_ref[IMG_EDGE], st_old[tile_rows:]], axis=0)
    st_mid = jnp.concatenate([st_mid[:half - tile_rows], st_mid[half - tile_rows:] + img_ref[IMG_PREV]], axis=0)
    st_wn = st_wn + img_ref[IMG_DIAG, 0:tail_rows]

    for j in range(nb):
        selm_ref[j] = jnp.broadcast_to(selm[j:j + 1, :], (8, lanes))
    block_rows = SEL_BLOCK * N_KV

    def select(st, row0):
        n_rows = st.shape[0]
        nblk = -(-n_rows // block_rows)
        per = n_rows // nblk // 8
        mask = selm_ref[row0 // block_rows:row0 // block_rows + nblk]
        return (st.reshape(nblk, per, 8, lanes) + mask[:, None]).reshape(n_rows, lanes)

    segs = [(select(st, rows.start), vbuf[slot, rows, :].astype(BF16)) for st, rows in zip(sel_st, sel_rows)]
    segs.append((select(st_new, past_rows), vst_ref[...].astype(BF16)))
    sel = _softmax_segments(segs)

    win = _softmax_segments([(st_old, vw_ref[0:half, :].astype(BF16)),
                             (st_mid, vw_ref[half:win_rows, :].astype(BF16)),
                             (st_wn, vwt_ref[...].astype(BF16))])

    o_r = _combine(gt_ref[...], o_c, sel, win, qb)
    for h in range(N_HEADS):
        o_ref[:, h * HEAD_DIM:(h + 1) * HEAD_DIM] = o_r[h * qb:(h + 1) * qb]

    for src, new, dst in ((kw_ref, kwt_ref, kwo_ref), (vw_ref, vwt_ref, vwo_ref)):
        dst[0:win_rows - tail_rows, :] = src[tail_rows:win_rows, :]
        dst[win_rows - tail_rows:win_rows, :] = new[...]


def _attn_prompt(qkv_f, qkv_b, gates_t, kcmp, vcmp, img, ct, seq, off2):
    qbp = KEY_TILE
    r_l = HPG * qbp
    n_tiles = seq // qbp
    nc = seq // CMP_STRIDE
    nb = seq // SEL_BLOCK
    q_cols = N_HEADS * HEAD_DIM // HEAD_DIM

    def kv_spec(which):
        return pl.BlockSpec((seq, HEAD_DIM), functools.partial(lambda g, i, c: (0, c + g), c=q_cols + which * N_KV))

    cmp_spec = pl.BlockSpec((nc, HEAD_DIM), lambda g, i: (0, g))
    return pl.pallas_call(
        functools.partial(_attn_prompt_kernel, nc=nc, nb=nb, off2=off2),
        grid=(N_KV, n_tiles // PROMPT_TILES),
        in_specs=[pl.BlockSpec((PROMPT_TILES * qbp, HPG * HEAD_DIM), lambda g, i: (i, g)),
                  pl.BlockSpec((PROMPT_TILES, None, 3, r_l), lambda g, i: (i, g, 0, 0)),
                  cmp_spec, cmp_spec,
                  kv_spec(2), kv_spec(3), kv_spec(4), kv_spec(5),
                  pl.BlockSpec((None, N_IMG, KEY_TILE, r_l), lambda g, i: (g, 0, 0, 0)),
                  pl.BlockSpec((None, ct.shape[1], r_l), lambda g, i: (g, 0, 0))],
        out_specs=pl.BlockSpec((PROMPT_TILES * qbp, HPG * HEAD_DIM), lambda g, i: (i, g)),
        out_shape=jax.ShapeDtypeStruct((seq, N_HEADS * HEAD_DIM), BF16),
        scratch_shapes=[pltpu.VMEM((PROMPT_TILES, nb * SEL_BLOCK // (SEL_GROUP * KEY_TILE), 16, r_l), BF16),
                        pltpu.VMEM((PROMPT_TILES, nc, KEY_TILE), F32)],
        compiler_params=_params(("arbitrary", "arbitrary"), 48),
        name="nsa_prompt",
    )(qkv_f, gates_t, kcmp, vcmp, qkv_b, qkv_b, qkv_b, qkv_b, img, ct)


def _key_features(n_rows):
    r = jnp.arange(n_rows, dtype=jnp.int32)[:, None]
    c = jnp.arange(KEY_TILE, dtype=jnp.int32)[None, :]
    hit = (c == r // (SEL_BLOCK * N_KV)) | (c == SAMPLE_GROUP_COL + r % N_KV)
    return hit.astype(BF16)


def _attn_sample(page_table, q_new, gates_t, kcmp, vcmp, kw, vw, ks_t, vs_t, kw_t, vw_t, img, ct,
                 ks_cache, vs_cache, qb):
    n_seq, n_pages = page_table.shape
    past = n_pages * PAGE
    lanes = N_HEADS * qb
    nb = 8 * (-(-(past // SEL_BLOCK + 2) // 8))
    assert nb <= SAMPLE_GROUP_COL and lanes == KEY_TILE
    qw = N_HEADS * HEAD_DIM
    tail_rows = ks_t.shape[1]
    cmp_rows = past // CMP_STRIDE * N_KV
    feat = _key_features(past * N_KV + tail_rows)
    tail_spec = pl.BlockSpec((None, tail_rows, HEAD_DIM), lambda b, pt: (b, 0, 0))
    win_spec = pl.BlockSpec((WINDOW * N_KV, HEAD_DIM), lambda b, pt: (b, 0))
    cmp_spec = pl.BlockSpec((cmp_rows, HEAD_DIM), lambda b, pt: (b, 0))
    any_spec = pl.BlockSpec(memory_space=pl.ANY)
    return pl.pallas_call(
        functools.partial(_attn_sample_kernel, qb=qb, nb=nb),
        grid_spec=pltpu.PrefetchScalarGridSpec(
            num_scalar_prefetch=1, grid=(n_seq,),
            in_specs=[pl.BlockSpec((qb, qw), lambda b, pt: (b, 0)),
                      pl.BlockSpec((None, 3, lanes), lambda b, pt: (b, 0, 0)),
                      cmp_spec, cmp_spec, win_spec, win_spec,
                      tail_spec, tail_spec, tail_spec, tail_spec,
                      pl.BlockSpec((None,) + img.shape[1:], lambda b, pt: (0, 0, 0, 0)),
                      pl.BlockSpec((None,) + ct.shape[1:], lambda b, pt: (0, 0, 0)),
                      pl.BlockSpec(feat.shape, lambda b, pt: (0, 0)),
                      any_spec, any_spec],
            out_specs=[pl.BlockSpec((qb, qw), lambda b, pt: (b, 0)), win_spec, win_spec],
            scratch_shapes=[pltpu.VMEM((nb, 8, lanes), F32),
                            pltpu.VMEM((2, n_pages * PAGE_ROWS, HEAD_DIM), F32),
                            pltpu.VMEM((2, n_pages * PAGE_ROWS, HEAD_DIM), F32),
                            pltpu.SemaphoreType.DMA((2, 2))]),
        out_shape=[jax.ShapeDtypeStruct((n_seq * qb, qw), F32),
                   jax.ShapeDtypeStruct(kw.shape, kw.dtype), jax.ShapeDtypeStruct(vw.shape, vw.dtype)],
        compiler_params=_params(("arbitrary",), 56),
        name="nsa_sample",
    )(page_table, q_new, gates_t, kcmp, vcmp, kw, vw, ks_t, vs_t, kw_t, vw_t, img, ct, feat, ks_cache, vs_cache)


def _gates_to_lanes(g_sig, n_tiles, q_rows, qbp):
    g5 = g_sig.reshape(n_tiles, q_rows, 3, N_KV, HPG)
    if qbp > q_rows:
        g5 = jnp.pad(g5, ((0, 0), (0, qbp - q_rows), (0, 0), (0, 0), (0, 0)))
    return g5.transpose(0, 3, 2, 4, 1).reshape(n_tiles, N_KV, 3, HPG * qbp)


def _history_rows(state, dec_seq):
    n_seq, _, c = state.shape
    z = jnp.zeros((n_seq, dec_seq - 2, c), state.dtype)
    s1 = jnp.concatenate([state[:, 1:2], jnp.zeros((n_seq, 1, c), state.dtype), z], axis=1)
    s2 = jnp.concatenate([state, z], axis=1)
    return s1.reshape(n_seq * dec_seq, c), s2.reshape(n_seq * dec_seq, c)


def _cmp_weights(pe, w1, w2):
    half = CMP_STRIDE * HEAD_DIM
    w1cat = jnp.concatenate([w1[:half], w1[half:]], axis=1).astype(BF16)
    return pe.reshape(2, half), w1cat, w2.astype(BF16)


def _token_group_rows(x):
    return x.reshape(-1, HEAD_DIM)


def kernel(x_prompt, x_sample, cache_k_cmp, cache_v_cmp, cache_k_sel, cache_v_sel, state_k_win, state_v_win,
           state_conv, state_ffn_conv, page_table, rel_bias, w_in, conv_w, conv_b, w_br_conv, w_br_nsa, w_out,
           pe_cmp_k, w_cmp_k1, w_cmp_k2, pe_cmp_v, w_cmp_v1, w_cmp_v2, ln1_g, ln1_b, w_ffn_in, ffn_conv_w,
           ffn_conv_b, w_ffn_out, ln2_g, ln2_b):
    seq = x_prompt.shape[1]
    n_seq, dec_seq = x_sample.shape[0], x_sample.shape[1]
    n_s = n_seq * dec_seq
    past = page_table.shape[1] * PAGE
    kvw = N_KV * HEAD_DIM
    qw = N_HEADS * HEAD_DIM
    tm = 1024

    x_s2 = x_sample.reshape(n_s, D_MODEL)
    x_bf = jnp.concatenate([x_prompt[0].astype(BF16), x_s2.astype(BF16)], axis=0)
    c_qkv = 3 * D_CONV
    c_gate = c_qkv + qw + 6 * kvw
    c_mix = c_gate + 3 * N_HEADS
    w_main = w_in[0].astype(BF16)
    w_gate = jnp.pad(w_main[:, c_gate:c_mix], ((0, 0), (0, 128 - 3 * N_HEADS)))
    w_mix = w_main[:, c_mix:]

    s1, s2 = _history_rows(state_conv[0], dec_seq)
    tn_c = 256
    z_conv, u_conv = _gconv(x_bf, w_main, (0, D_CONV // tn_c, 2 * D_CONV // tn_c), conv_w[0], conv_b, s1, s2,
                            seq, tm, tn_c, 48, "proj_conv")
    qkv_f, qkv_b = _proj(x_bf, w_main, c_qkv, qw + 6 * kvw, tm, 512)
    g_sig = _gate_proj(x_bf, w_gate, tm)[:, :3 * N_HEADS]

    pe_k, w1_k, w2_k = _cmp_weights(pe_cmp_k[0], w_cmp_k1[0], w_cmp_k2[0])
    pe_v, w1_v, w2_v = _cmp_weights(pe_cmp_v[0], w_cmp_v1[0], w_cmp_v2[0])
    chunk_w = CMP_STRIDE * kvw
    kc_p = qkv_b[:seq, qw:qw + kvw].reshape(seq // CMP_STRIDE, chunk_w)
    vc_p = qkv_b[:seq, qw + kvw:qw + 2 * kvw].reshape(seq // CMP_STRIDE, chunk_w)
    kcmp_p = _compress(kc_p, pe_k, w1_k, w2_k, "compress_k_prompt")
    vcmp_p = _compress(vc_p, pe_v, w1_v, w2_v, "compress_v_prompt")
    kcmp_s, vcmp_s = _compress_paged(page_table, _token_group_rows(cache_k_cmp), _token_group_rows(cache_v_cmp),
                                     (pe_k, w1_k, w2_k), (pe_v, w1_v, w2_v))

    off2_p = (seq - KEY_TILE) // CMP_STRIDE
    nct_p = KEY_TILE * (-(-(off2_p + seq // CMP_STRIDE) // KEY_TILE))
    img_p, ct_p = _bias_images(rel_bias, HPG, KEY_TILE, 1, nct_p, off2_p, "bias_prompt")
    gates_p = _gates_to_lanes(g_sig[:seq], seq // KEY_TILE, KEY_TILE, KEY_TILE)
    o_p = _attn_prompt(qkv_f, qkv_b, gates_p, kcmp_p, vcmp_p, img_p, ct_p, seq, off2_p)

    chunks_s = past // CMP_STRIDE
    img_s, ct_s = _bias_images(rel_bias, N_HEADS, dec_seq, N_KV, chunks_s, chunks_s, "bias_sample")
    gates_s = g_sig[seq:].reshape(n_seq, dec_seq, 3, N_HEADS).transpose(0, 2, 3, 1).reshape(n_seq, 3, N_HEADS * dec_seq)
    new_f = qkv_f[seq:].reshape(n_seq, dec_seq, qw + 6 * kvw)

    def tail(col):
        return new_f[:, :, col:col + kvw].reshape(n_seq, dec_seq * N_KV, HEAD_DIM)

    o_s, kw_next, vw_next = _attn_sample(page_table, qkv_f[seq:, :qw], gates_s,
                       kcmp_s.reshape(-1, HEAD_DIM), vcmp_s.reshape(-1, HEAD_DIM),
                       _token_group_rows(state_k_win), _token_group_rows(state_v_win),
                       tail(qw + 2 * kvw), tail(qw + 3 * kvw), tail(qw + 4 * kvw), tail(qw + 5 * kvw),
                       img_s, ct_s, _token_group_rows(cache_k_sel), _token_group_rows(cache_v_sel), dec_seq)

    mix = _merge(x_bf, z_conv, o_p, o_s, w_br_conv[0], w_br_nsa[0], w_mix, 512, 256)
    r1 = _resid_mm_split(mix, w_out[0].astype(BF16), x_prompt[0], x_s2, tm, 512, 48, "out_proj")
    h_f, h_b = _layer_norm(r1, ln1_g, ln1_b, 256, "ln1")
    f1, f2 = _history_rows(state_ffn_conv[0], dec_seq)
    tn_f = 256
    act, gp = _gconv(h_b, w_ffn_in[0], (D_FF // tn_f, 0), ffn_conv_w[0], ffn_conv_b, f1, f2,
                     seq, tm, tn_f, 56, "ffn_in")
    r2 = _resid_mm(act, w_ffn_out[0].astype(BF16), h_f, 512, 256, 56, "ffn_out")
    y_p, y_s = _layer_norm_split(r2, ln2_g, ln2_b, seq, 256, "ln2")

    def rows_p(col):
        return qkv_f[:seq, col:col + kvw].reshape(1, 1, seq, N_KV, HEAD_DIM)

    def rows_s(col):
        return new_f[:, :, col:col + kvw].reshape(1, n_seq, dec_seq, N_KV, HEAD_DIM)

    def win_p(col):
        return qkv_f[seq - WINDOW:seq, col:col + kvw].reshape(1, 1, WINDOW, N_KV, HEAD_DIM)


    def last2(u):
        c = u.shape[1]
        return (u[seq - 2:seq].reshape(1, 1, 2, c),
                u[seq:].reshape(n_seq, dec_seq, c)[:, dec_seq - 2:].reshape(1, n_seq, 2, c))

    conv_p, conv_s = last2(u_conv)
    ffn_p, ffn_s = last2(gp)
    c = qw
    return (y_p.reshape(1, seq, D_MODEL), y_s.reshape(n_seq, dec_seq, D_MODEL),
            rows_p(c), rows_s(c), rows_p(c + kvw), rows_s(c + kvw),
            rows_p(c + 2 * kvw), rows_s(c + 2 * kvw), rows_p(c + 3 * kvw), rows_s(c + 3 * kvw),
            win_p(c + 4 * kvw), kw_next.reshape(state_k_win.shape),
            win_p(c + 5 * kvw), vw_next.reshape(state_v_win.shape),
            conv_p, conv_s, ffn_p, ffn_s)
```

```python
import functools
import math

import jax
import jax.numpy as jnp
from jax import lax
from jax.experimental import pallas as pl
from jax.experimental.pallas import tpu as pltpu

F32 = jnp.float32
BF16 = jnp.bfloat16

D_MODEL = 4096
D_CONV = 2048
N_HEADS = 16
HEAD_DIM = 128
N_KV = 4
HPG = 4
CMP_STRIDE = 16
CMP_HID = 256
SEL_BLOCK = 64
N_SEL = 16
WINDOW = 512
FORCE_SCORE = 1e4
REL_BUCKETS = 32
REL_EXACT = 16
REL_MAX_DIST = 128
D_FF = 11008
PAGE = 128
ALPHA = 2.0 ** 0.25
LN_EPS = 1e-5
LOG2E = 1.0 / math.log(2.0)
Q_SCALE = HEAD_DIM ** -0.5 * LOG2E

KEY_TILE = 128
SEL_GROUP = 4
GCONV_PARTS = 2
PROMPT_TILES = 4
SEL_STATES = 3
PAGE_ROWS = PAGE * N_KV
NEG = -1e30
MIB = 1024 * 1024

IMG_DIAG, IMG_PREV, IMG_ZERO, IMG_EDGE, IMG_NEG = 0, 1, 2, 3, 4
N_IMG = 5


def _params(sem, vmem_mib):
    return pltpu.CompilerParams(dimension_semantics=sem, vmem_limit_bytes=vmem_mib * MIB)


def _dot(a, b):
    return jnp.dot(a, b, preferred_element_type=F32)


def _proj_kernel(x_ref, w_ref, of_ref, ob_ref):
    acc = _dot(x_ref[...], w_ref[...].astype(BF16))
    of_ref[...] = acc
    ob_ref[...] = acc.astype(BF16)


def _proj(x, w, col0, ncols, tm, tn):
    m_rows, k = x.shape
    c0 = col0 // tn
    return pl.pallas_call(
        _proj_kernel,
        grid=(m_rows // tm, ncols // tn),
        in_specs=[pl.BlockSpec((tm, k), lambda m, n: (m, 0)),
                  pl.BlockSpec((k, tn), lambda m, n: (0, c0 + n))],
        out_specs=[pl.BlockSpec((tm, tn), lambda m, n: (m, n)),
                   pl.BlockSpec((tm, tn), lambda m, n: (m, n))],
        out_shape=[jax.ShapeDtypeStruct((m_rows, ncols), F32),
                   jax.ShapeDtypeStruct((m_rows, ncols), BF16)],
        compiler_params=_params(("arbitrary", "arbitrary"), 56),
        name="proj_qkv",
    )(x, w)


def _gate_kernel(x_ref, w_ref, o_ref):
    o_ref[...] = jax.nn.sigmoid(_dot(x_ref[...], w_ref[...]))


def _gate_proj(x, w, tm):
    m_rows, k = x.shape
    n = w.shape[1]
    return pl.pallas_call(
        _gate_kernel,
        grid=(m_rows // tm,),
        in_specs=[pl.BlockSpec((tm, k), lambda m: (m, 0)),
                  pl.BlockSpec((k, n), lambda m: (0, 0))],
        out_specs=pl.BlockSpec((tm, n), lambda m: (m, 0)),
        out_shape=jax.ShapeDtypeStruct((m_rows, n), F32),
        compiler_params=_params(("arbitrary",), 40),
        name="proj_gates",
    )(x, w)


def _gconv_kernel(*refs, three, n_prompt_tiles):
    if three:
        x_ref, w0_ref, w1_ref, w2_ref, cw_ref, cb_ref, s1_ref, s2_ref, z_ref, u_ref, carry_ref = refs
    else:
        x_ref, w0_ref, w1_ref, cw_ref, cb_ref, s1_ref, s2_ref, z_ref, u_ref, carry_ref = refs
    m = pl.program_id(0)
    n = pl.program_id(1)
    is_sample = m >= n_prompt_tiles
    tm, tn = u_ref.shape
    hm = tm // GCONV_PARTS
    cw = cw_ref[...]
    cb = cb_ref[...]
    c = carry_ref[n]
    c = jnp.where(m == 0, jnp.zeros_like(c), c)
    prev_a, prev_b = c[6:7], c[7:8]
    row = lax.broadcasted_iota(jnp.int32, (hm, tn), 0)
    rmask = row & jnp.where(is_sample, 7, hm - 1)
    w0 = w0_ref[...].astype(BF16)
    w1 = w1_ref[...].astype(BF16)
    w2 = w2_ref[...].astype(BF16) if three else None

    def project(part):
        x = x_ref[part * hm:(part + 1) * hm, :]
        u = _dot(x, w1)
        if three:
            u = u * _dot(x, w2)
        p0 = _dot(x, w0)
        return p0, u

    def epilogue(part, p0, u, prev_a, prev_b):
        rows = slice(part * hm, (part + 1) * hm)
        u_ref[rows, :] = u
        fill1 = jnp.where(is_sample, s1_ref[rows, :], prev_b)
        fill2 = jnp.where(is_sample, s2_ref[rows, :], jnp.where(row == 0, prev_a, prev_b))
        prev1 = jnp.where(rmask == 0, fill1, pltpu.roll(u, 1, 0))
        prev2 = jnp.where(rmask < 2, fill2, pltpu.roll(u, 2, 0))
        conv = cb + prev2 * cw[0:1] + prev1 * cw[1:2] + u * cw[2:3]
        if three:
            z = p0 * conv
        else:
            z = jax.nn.silu(conv) * p0
        z_ref[rows, :] = z.astype(z_ref.dtype)
        return u[hm - 2:hm - 1], u[hm - 1:hm]

    done = project(0)
    for part in range(1, GCONV_PARTS):
        nxt = project(part)
        prev_a, prev_b = epilogue(part - 1, *done, prev_a, prev_b)
        done = nxt
    epilogue(GCONV_PARTS - 1, *done, prev_a, prev_b)
    carry_ref[n] = done[1][hm - 8:hm]


def _gconv(x, w, col_blocks, cw, cb, s1, s2, n_prompt_rows, tm, tn, vmem_mib, name):
    m_rows, k = x.shape
    ncols = cw.shape[1]
    three = len(col_blocks) == 3
    assert m_rows - n_prompt_rows == tm and n_prompt_rows % tm == 0
    nt = ncols // tn
    w_specs = [pl.BlockSpec((k, tn), functools.partial(lambda m, n, c: (0, c + n), c=c)) for c in col_blocks]
    col_spec = pl.BlockSpec((tm, tn), lambda m, n: (m, n))
    npt = n_prompt_rows // tm
    hist_spec = pl.BlockSpec((tm, tn), lambda m, n: (0, jnp.where(m < npt, 0, n)))
    return pl.pallas_call(
        functools.partial(_gconv_kernel, three=three, n_prompt_tiles=npt),
        grid=(m_rows // tm, nt),
        in_specs=[pl.BlockSpec((tm, k), lambda m, n: (m, 0))] + w_specs + [
            pl.BlockSpec((3, tn), lambda m, n: (0, n)),
            pl.BlockSpec((1, tn), lambda m, n: (0, n)),
            hist_spec, hist_spec],
        out_specs=[col_spec, col_spec],
        out_shape=[jax.ShapeDtypeStruct((m_rows, ncols), BF16),
                   jax.ShapeDtypeStruct((m_rows, ncols), F32)],
        scratch_shapes=[pltpu.VMEM((nt, 8, tn), F32)],
        compiler_params=_params(("arbitrary", "arbitrary"), vmem_mib),
        name=name,
    )(x, *([w] * len(col_blocks)), cw, cb, s1, s2)


def _merge_kernel(x_ref, z_ref, op_ref, os_ref, wc_ref, wn_ref, wga_ref, wgb_ref, out_ref, *, n_prompt_tiles):
    x = x_ref[...]
    o = jnp.where(pl.program_id(0) < n_prompt_tiles, op_ref[...], os_ref[...].astype(BF16))
    a = _dot(z_ref[...], wc_ref[...].astype(BF16))
    b = _dot(o, wn_ref[...].astype(BF16))
    ga = jax.nn.sigmoid(_dot(x, wga_ref[...]))
    gb = jax.nn.sigmoid(_dot(x, wgb_ref[...]))
    out_ref[...] = (ga * a + gb * b).astype(BF16)


def _merge(x, z, o_p, o_s, wc, wn, wg, tm, tn):
    m_rows, k = x.shape
    kc = z.shape[1]
    nt = D_MODEL // tn
    npt = o_p.shape[0] // tm
    assert npt * tm == o_p.shape[0] and o_s.shape[0] % tm == 0 and o_p.shape[0] + o_s.shape[0] == m_rows
    return pl.pallas_call(
        functools.partial(_merge_kernel, n_prompt_tiles=npt),
        grid=(m_rows // tm, nt),
        in_specs=[pl.BlockSpec((tm, k), lambda m, n: (m, 0)),
                  pl.BlockSpec((tm, kc), lambda m, n: (m, 0)),
                  pl.BlockSpec((tm, kc), lambda m, n: (jnp.minimum(m, npt - 1), 0)),
                  pl.BlockSpec((tm, kc), lambda m, n: (jnp.maximum(m - npt, 0), 0)),
                  pl.BlockSpec((kc, tn), lambda m, n: (0, n)),
                  pl.BlockSpec((kc, tn), lambda m, n: (0, n)),
                  pl.BlockSpec((k, tn), lambda m, n: (0, n)),
                  pl.BlockSpec((k, tn), lambda m, n: (0, nt + n))],
        out_specs=pl.BlockSpec((tm, tn), lambda m, n: (m, n)),
        out_shape=jax.ShapeDtypeStruct((m_rows, D_MODEL), BF16),
        compiler_params=_params(("arbitrary", "arbitrary"), 56),
        name="merge_branches",
    )(x, z, o_p, o_s, wc, wn, wg, wg)


def _resid_kernel(l_ref, w_ref, r_ref, o_ref):
    o_ref[...] = ALPHA * r_ref[...] + _dot(l_ref[...], w_ref[...])


def _resid2_kernel(l_ref, w_ref, rp_ref, rs_ref, o_ref, *, n_prompt_tiles):
    res = jnp.where(pl.program_id(0) < n_prompt_tiles, rp_ref[...], rs_ref[...])
    o_ref[...] = ALPHA * res + _dot(l_ref[...], w_ref[...].astype(BF16))


def _resid_mm_split(lhs, w, res_p, res_s, tm, tn, vmem_mib, name):
    m_rows, k = lhs.shape
    n_cols = w.shape[1]
    npt = res_p.shape[0] // tm
    assert res_s.shape[0] == tm and npt * tm + tm == m_rows
    return pl.pallas_call(
        functools.partial(_resid2_kernel, n_prompt_tiles=npt),
        grid=(m_rows // tm, n_cols // tn),
        in_specs=[pl.BlockSpec((tm, k), lambda m, n: (m, 0)),
                  pl.BlockSpec((k, tn), lambda m, n: (0, n)),
                  pl.BlockSpec((tm, tn), lambda m, n: (jnp.minimum(m, npt - 1), n)),
                  pl.BlockSpec((tm, tn), lambda m, n: (0, n))],
        out_specs=pl.BlockSpec((tm, tn), lambda m, n: (m, n)),
        out_shape=jax.ShapeDtypeStruct((m_rows, n_cols), F32),
        compiler_params=_params(("arbitrary", "arbitrary"), vmem_mib),
        name=name,
    )(lhs, w, res_p, res_s)


def _resid_mm(lhs, w, res, tm, tn, vmem_mib, name):
    m_rows, k = lhs.shape
    n_cols = w.shape[1]
    return pl.pallas_call(
        _resid_kernel,
        grid=(m_rows // tm, n_cols // tn),
        in_specs=[pl.BlockSpec((tm, k), lambda m, n: (m, 0)),
                  pl.BlockSpec((k, tn), lambda m, n: (0, n)),
                  pl.BlockSpec((tm, tn), lambda m, n: (m, n))],
        out_specs=pl.BlockSpec((tm, tn), lambda m, n: (m, n)),
        out_shape=jax.ShapeDtypeStruct((m_rows, n_cols), F32),
        compiler_params=_params(("arbitrary", "arbitrary"), vmem_mib),
        name=name,
    )(lhs, w, res)


def _ln_kernel(x_ref, g_ref, b_ref, of_ref, ob_ref):
    x = x_ref[...]
    mu = jnp.mean(x, axis=-1, keepdims=True)
    xc = x - mu
    var = jnp.mean(xc * xc, axis=-1, keepdims=True)
    y = xc * lax.rsqrt(var + LN_EPS) * g_ref[...] + b_ref[...]
    of_ref[...] = y
    ob_ref[...] = y.astype(BF16)


def _layer_norm(x, g, b, tr, name):
    m_rows, d = x.shape
    row_spec = pl.BlockSpec((tr, d), lambda m: (m, 0))
    vec_spec = pl.BlockSpec((1, d), lambda m: (0, 0))
    return pl.pallas_call(
        _ln_kernel,
        grid=(m_rows // tr,),
        in_specs=[row_spec, vec_spec, vec_spec],
        out_specs=[row_spec, row_spec],
        out_shape=[jax.ShapeDtypeStruct((m_rows, d), F32), jax.ShapeDtypeStruct((m_rows, d), BF16)],
        compiler_params=_params(("arbitrary",), 48),
        name=name,
    )(x, g, b)


def _ln_split_kernel(x_ref, g_ref, b_ref, yp_ref, ys_ref, *, n_prompt_tiles):
    x = x_ref[...]
    mu = jnp.mean(x, axis=-1, keepdims=True)
    xc = x - mu
    var = jnp.mean(xc * xc, axis=-1, keepdims=True)
    y = xc * lax.rsqrt(var + LN_EPS) * g_ref[...] + b_ref[...]
    m = pl.program_id(0)

    @pl.when(m < n_prompt_tiles)
    def _():
        yp_ref[...] = y

    @pl.when(m >= n_prompt_tiles)
    def _():
        ys_ref[...] = y


def _layer_norm_split(x, g, b, n_prompt_rows, tr, name):
    m_rows, d = x.shape
    npt = n_prompt_rows // tr
    vec_spec = pl.BlockSpec((1, d), lambda m: (0, 0))
    return pl.pallas_call(
        functools.partial(_ln_split_kernel, n_prompt_tiles=npt),
        grid=(m_rows // tr,),
        in_specs=[pl.BlockSpec((tr, d), lambda m: (m, 0)), vec_spec, vec_spec],
        out_specs=[pl.BlockSpec((tr, d), lambda m: (jnp.minimum(m, npt - 1), 0)),
                   pl.BlockSpec((tr, d), lambda m: (jnp.maximum(m - npt, 0), 0))],
        out_shape=[jax.ShapeDtypeStruct((n_prompt_rows, d), F32),
                   jax.ShapeDtypeStruct((m_rows - n_prompt_rows, d), F32)],
        compiler_params=_params(("arbitrary",), 48),
        name=name,
    )(x, g, b)


def _page_copies(pt_ref, seq, cache_ref, buf, slot, sem):
    return [pltpu.make_async_copy(
        cache_ref.at[pl.ds(pl.multiple_of(pt_ref[seq, j] * PAGE_ROWS, PAGE_ROWS), PAGE_ROWS)],
        buf.at[slot, pl.ds(j * PAGE_ROWS, PAGE_ROWS)], sem)
        for j in range(pt_ref.shape[1])]


def _prefetch_pages(pt_ref, b, n_seq, streams):
    slot = b % 2

    @pl.when(b == 0)
    def _():
        for cache_ref, buf, sem in streams:
            for cp in _page_copies(pt_ref, 0, cache_ref, buf, 0, sem.at[0]):
                cp.start()

    @pl.when(b + 1 < n_seq)
    def _():
        for cache_ref, buf, sem in streams:
            for cp in _page_copies(pt_ref, b + 1, cache_ref, buf, 1 - slot, sem.at[1 - slot]):
                cp.start()

    for cache_ref, buf, sem in streams:
        for cp in _page_copies(pt_ref, b, cache_ref, buf, slot, sem.at[slot]):
            cp.wait()
    return slot


def _compress_rows(xg, pe, w1, w2, next_chunk=1):
    return _compress_finish(_compress_first_layer(xg, pe, w1), w2, next_chunk)


def _compress_first_layer(xg, pe, w1):
    half = CMP_STRIDE * HEAD_DIM
    pa = _dot(jnp.broadcast_to(pe[0:1], (8, half)).astype(BF16), w1[:, :CMP_HID])[0:1]
    pb = _dot(jnp.broadcast_to(pe[1:2], (8, half)).astype(BF16), w1[:, CMP_HID:])[0:1]
    fs = _dot(xg, w1)
    return fs[:, :CMP_HID] + pa, fs[:, CMP_HID:] + pb


def _compress_finish(halves, w2, next_chunk):
    first, second = halves
    hid = jax.nn.gelu(first + pltpu.roll(second, second.shape[0] - next_chunk, 0))
    return _dot(hid.astype(BF16), w2)


def _compress_kernel(x_ref, pe_ref, w1_ref, w2_ref, o_ref):
    row_w = N_KV * HEAD_DIM
    for g in range(N_KV):
        xg = jnp.concatenate(
            [x_ref[:, r * row_w + g * HEAD_DIM: r * row_w + (g + 1) * HEAD_DIM] for r in range(CMP_STRIDE)],
            axis=1).astype(BF16)
        o_ref[:, g * HEAD_DIM:(g + 1) * HEAD_DIM] = _compress_rows(
            xg, pe_ref[...], w1_ref[...], w2_ref[...]).astype(BF16)


def _compress(x2, pe2, w1cat, w2, name):
    rows, width = x2.shape
    return pl.pallas_call(
        _compress_kernel,
        grid=(1,),
        in_specs=[pl.BlockSpec((rows, width), lambda t: (0, 0)),
                  pl.BlockSpec(pe2.shape, lambda t: (0, 0)),
                  pl.BlockSpec(w1cat.shape, lambda t: (0, 0)),
                  pl.BlockSpec(w2.shape, lambda t: (0, 0))],
        out_specs=pl.BlockSpec((rows, N_KV * HEAD_DIM), lambda t: (0, 0)),
        out_shape=jax.ShapeDtypeStruct((rows, N_KV * HEAD_DIM), BF16),
        compiler_params=_params(("arbitrary",), 48),
        name=name,
    )(x2, pe2, w1cat, w2)


def _compress_paged_kernel(pt_ref, kc_hbm, vc_hbm, pek_ref, w1k_ref, w2k_ref, pev_ref, w1v_ref, w2v_ref,
                           ok_ref, ov_ref, kbuf, vbuf, sem, *, chunks):
    b = pl.program_id(0)
    slot = _prefetch_pages(pt_ref, b, pl.num_programs(0), [(kc_hbm, kbuf, sem.at[0]), (vc_hbm, vbuf, sem.at[1])])

    def chunk_rows(buf):
        x3 = buf[slot].reshape(chunks, CMP_STRIDE * N_KV, HEAD_DIM)
        return jnp.concatenate(
            [x3[:, r * N_KV:(r + 1) * N_KV, :].reshape(chunks * N_KV, HEAD_DIM) for r in range(CMP_STRIDE)],
            axis=1).astype(BF16)

    hk = _compress_first_layer(chunk_rows(kbuf), pek_ref[...], w1k_ref[...])
    hv = _compress_first_layer(chunk_rows(vbuf), pev_ref[...], w1v_ref[...])
    ok_ref[...] = _compress_finish(hk, w2k_ref[...], N_KV).astype(BF16)
    ov_ref[...] = _compress_finish(hv, w2v_ref[...], N_KV).astype(BF16)


def _compress_paged(page_table, kc_cache, vc_cache, wk, wv):
    n_seq, n_pages = page_table.shape
    chunks = n_pages * PAGE // CMP_STRIDE
    w_specs = [pl.BlockSpec(w.shape, lambda b, pt: (0, 0)) for w in wk + wv]
    out_spec = pl.BlockSpec((chunks * N_KV, HEAD_DIM), lambda b, pt: (b, 0))
    out_shape = jax.ShapeDtypeStruct((n_seq * chunks * N_KV, HEAD_DIM), BF16)
    buf = pltpu.VMEM((2, n_pages * PAGE_ROWS, HEAD_DIM), F32)
    return pl.pallas_call(
        functools.partial(_compress_paged_kernel, chunks=chunks),
        grid_spec=pltpu.PrefetchScalarGridSpec(
            num_scalar_prefetch=1, grid=(n_seq,),
            in_specs=[pl.BlockSpec(memory_space=pl.ANY), pl.BlockSpec(memory_space=pl.ANY)] + w_specs,
            out_specs=[out_spec, out_spec],
            scratch_shapes=[buf, buf, pltpu.SemaphoreType.DMA((2, 2))]),
        out_shape=[out_shape, out_shape],
        compiler_params=_params(("arbitrary",), 48),
        name="compress_sample",
    )(page_table, kc_cache, vc_cache, *wk, *wv)


def _bias_kernel(rb_ref, img_ref, ct_ref, *, heads, qbp, rpk, off2):
    g = pl.program_id(0)
    r_l = heads * qbp
    lane = lax.broadcasted_iota(jnp.int32, (1, r_l), 1)
    n_l = lane // qbp
    tbl = []
    for k in range(REL_BUCKETS):
        row = jnp.zeros((1, r_l), F32)
        for n in range(heads):
            row = jnp.where(n_l == n, rb_ref[k, g * heads + n], row)
        tbl.append(row)

    def bias_of(dist):
        nn = jnp.maximum(dist, 0)
        nf = jnp.maximum(nn, 1).astype(F32)
        large = REL_EXACT + (jnp.log(nf / REL_EXACT) / math.log(REL_MAX_DIST / REL_EXACT)
                             * (REL_BUCKETS - REL_EXACT)).astype(jnp.int32)
        bucket = jnp.where(nn < REL_EXACT, nn, jnp.minimum(large, REL_BUCKETS - 1))
        val = jnp.zeros(dist.shape, F32)
        for k in range(REL_BUCKETS):
            val = jnp.where(bucket == k, tbl[k], val)
        return val

    shape = (KEY_TILE, r_l)
    kj = lax.broadcasted_iota(jnp.int32, shape, 0) // rpk
    qi = lax.broadcasted_iota(jnp.int32, shape, 1) % qbp
    far = tbl[REL_BUCKETS - 1]
    for r0 in range(0, rpk * KEY_TILE, KEY_TILE):
        rows = slice(r0, r0 + KEY_TILE)
        d0 = qi - (kj + r0 // rpk)
        img_ref[IMG_DIAG, rows] = jnp.where(d0 >= 0, (bias_of(d0) - far) * LOG2E, NEG)
        img_ref[IMG_PREV, rows] = (bias_of(d0 + KEY_TILE) - far) * LOG2E
        img_ref[IMG_ZERO, rows] = jnp.zeros(shape, F32)
        img_ref[IMG_EDGE, rows] = jnp.where(d0 < 0, 0.0, NEG)
        img_ref[IMG_NEG, rows] = jnp.full(shape, NEG, F32)
    for r0 in range(0, ct_ref.shape[0], KEY_TILE):
        dist = qi + CMP_STRIDE * (off2 - r0 // rpk - kj) - (2 * CMP_STRIDE - 1)
        ct_ref[r0:r0 + KEY_TILE, :] = jnp.where(dist >= 0, bias_of(dist) * LOG2E, NEG)


def _bias_images(rel_bias, heads, qbp, rpk, nct, off2, name):
    r_l = heads * qbp
    groups = N_HEADS // heads
    return pl.pallas_call(
        functools.partial(_bias_kernel, heads=heads, qbp=qbp, rpk=rpk, off2=off2),
        grid=(groups,),
        in_specs=[pl.BlockSpec(memory_space=pltpu.SMEM)],
        out_specs=[pl.BlockSpec((None, N_IMG, rpk * KEY_TILE, r_l), lambda g: (g, 0, 0, 0)),
                   pl.BlockSpec((None, rpk * nct, r_l), lambda g: (g, 0, 0))],
        out_shape=[jax.ShapeDtypeStruct((groups, N_IMG, rpk * KEY_TILE, r_l), F32),
                   jax.ShapeDtypeStruct((groups, rpk * nct, r_l), F32)],
        compiler_params=_params(("arbitrary",), 32),
        name=name,
    )(rel_bias)


def _pv(v, p):
    return lax.dot_general(v, p.astype(BF16), (((0,), (0,)), ((), ())), preferred_element_type=F32)


def _online_update(carry, st, v):
    m_run, l_run, acc = carry
    m_new = jnp.maximum(m_run, jnp.max(st, axis=0, keepdims=True))
    alpha = jnp.exp2(m_run - m_new)
    p = jnp.exp2(st - m_new)
    l_new = alpha * l_run + jnp.sum(p, axis=0, keepdims=True)
    return m_new, l_new, alpha * acc + _pv(v, p)


def _softmax_init(r_l):
    return jnp.full((1, r_l), NEG, F32), jnp.zeros((1, r_l), F32), jnp.zeros((HEAD_DIM, r_l), F32)


def _block_onehot(n_keys):
    kj = lax.broadcasted_iota(jnp.int32, (n_keys, KEY_TILE), 0)
    col = lax.broadcasted_iota(jnp.int32, (n_keys, KEY_TILE), 1)
    return jnp.where(col == kj // SEL_BLOCK, 1.0, 0.0).astype(BF16)


def _query_lanes(q):
    q = q * Q_SCALE
    qg = jnp.concatenate([q[:, n * HEAD_DIM:(n + 1) * HEAD_DIM] for n in range(HPG)], axis=0)
    return qg.T.astype(BF16)


def _cmp_and_select(q_t, kc, vc, cbias, t0, ps_ref, *, qbp, nc, nb, side_work=None):
    r_l = HPG * qbp
    s = _dot(kc, q_t) + cbias
    valid = cbias > 0.5 * NEG
    mx = jnp.max(s, axis=0, keepdims=True)
    mx = jnp.where(mx > 0.5 * NEG, mx, 0.0)
    e = jnp.where(valid, jnp.exp2(s - mx), 0.0)
    p_c = e / jnp.maximum(jnp.sum(e, axis=0, keepdims=True), 1e-30)
    o_c = _pv(vc, p_c)

    if qbp == KEY_TILE:
        psum = p_c[:, 0:qbp]
        for n in range(1, HPG):
            psum = psum + p_c[:, n * qbp:(n + 1) * qbp]
    else:
        psum = p_c
        for n in range(1, HPG):
            psum = psum + pltpu.roll(p_c, n * qbp, 1)
    ps_ref[...] = psum
    per_sel = SEL_BLOCK // CMP_STRIDE
    nbi = nc // per_sel
    imp = ps_ref[pl.ds(0, nbi, stride=per_sel), :]
    for c in range(1, per_sel):
        imp = imp + ps_ref[pl.ds(c, nbi, stride=per_sel), :]
    if nb > nbi:
        imp = jnp.concatenate([imp, jnp.zeros((nb - nbi, KEY_TILE), F32)], axis=0)
    jb = lax.broadcasted_iota(jnp.int32, (nb, KEY_TILE), 0)
    tq = t0 + lax.broadcasted_iota(jnp.int32, (nb, KEY_TILE), 1) % qbp
    cur = tq // SEL_BLOCK
    forced = (jb == 0) | (jb == cur) | (jb == cur - 1)
    score = jnp.where(jb <= cur, jnp.where(forced, FORCE_SCORE, imp), NEG)
    jbf = jb.astype(F32)
    side = side_work() if side_work is not None else None
    selm = jnp.full((nb, KEY_TILE), NEG, F32)
    for _ in range(N_SEL):
        best = jnp.max(score, axis=0, keepdims=True)
        first = jnp.min(jnp.where(score == best, jbf, 1e9), axis=0, keepdims=True)
        pick = jbf == first
        selm = jnp.where(pick, 0.0, selm)
        score = jnp.where(pick, 3.0 * NEG, score)
    if r_l > KEY_TILE:
        selm = jnp.concatenate([selm] * (r_l // KEY_TILE), axis=1)
    return o_c, selm, side


def _combine(gt, o_c, sel, win, qbp):
    (_, l_s, acc_s), (_, l_w, acc_w) = sel, win
    o_t = gt[0:1] * o_c + gt[1:2] * (acc_s / l_s) + gt[2:3] * (acc_w / l_w)
    return o_t.T


def _attn_prompt_kernel(q_ref, gt_ref, kc_ref, vc_ref, ks_ref, vs_ref, kw_ref, vw_ref, img_ref, ct_ref,
                        o_ref, selg_ref, ps_ref, *, nc, nb, off2):
    qbp = KEY_TILE
    r_l = HPG * qbp
    n_win = WINDOW // KEY_TILE
    grp_keys = SEL_GROUP * KEY_TILE
    grp_blocks = grp_keys // SEL_BLOCK
    n_grp, pad_rows = selg_ref.shape[1], selg_ref.shape[2] - grp_blocks
    onehot = _block_onehot(grp_keys)
    zpad = jnp.zeros((KEY_TILE - selg_ref.shape[2], r_l), BF16)

    def img_of(d, edge):
        idx = jnp.minimum(d, IMG_ZERO)
        if edge:
            idx = jnp.where(d == n_win, IMG_EDGE, idx)
        return img_ref[jnp.where(d < 0, IMG_NEG, idx)]

    def grp_rows(it):
        return pl.ds(pl.multiple_of(it * grp_keys, grp_keys), grp_keys)

    def before_selection(t):
        i_tile = PROMPT_TILES * pl.program_id(1) + t
        q_t = _query_lanes(q_ref[t * qbp:(t + 1) * qbp, :])

        def window():
            first_tile = jnp.maximum(i_tile - n_win, 0)
            rows = pl.ds(pl.multiple_of(first_tile * KEY_TILE, KEY_TILE), (n_win + 1) * KEY_TILE)
            add = jnp.concatenate([img_of(i_tile - (first_tile + w), True) for w in range(n_win + 1)], axis=0)
            return _online_update(_softmax_init(r_l), _dot(kw_ref[rows, :], q_t) + add, vw_ref[rows, :])

        start = pl.multiple_of(off2 - i_tile * (KEY_TILE // CMP_STRIDE), 8)
        o_c, selm, win = _cmp_and_select(q_t, kc_ref[...], vc_ref[...], ct_ref[pl.ds(start, nc), :],
                                         i_tile * KEY_TILE, ps_ref.at[t], qbp=qbp, nc=nc, nb=nb, side_work=window)
        selg_ref[t] = jnp.concatenate(
            [selm.reshape(n_grp, grp_blocks, r_l), jnp.zeros((n_grp, pad_rows, r_l), F32)], axis=1).astype(BF16)
        return i_tile, q_t, o_c, win

    def selected_branch(t, i_tile, q_t):
        n_it = (i_tile + SEL_GROUP) // SEL_GROUP
        n_far = jnp.maximum(i_tile - 1, 0) // SEL_GROUP

        def scores(it):
            k_aug = jnp.concatenate([ks_ref[grp_rows(it), :], onehot], axis=1)
            q_aug = jnp.concatenate([q_t, selg_ref[t, it], zpad], axis=0)
            return _dot(k_aug, q_aug)

        def multi_body(itn, states):
            sts = [scores(SEL_STATES * itn + k) for k in range(SEL_STATES)]
            return tuple(_online_update(states[k], sts[k], vs_ref[grp_rows(SEL_STATES * itn + k), :])
                         for k in range(SEL_STATES))

        def single_body(it, state):
            st = scores(it) + jnp.concatenate(
                [img_of(i_tile - (it * SEL_GROUP + w), False) for w in range(SEL_GROUP)], axis=0)
            return _online_update(state, st, vs_ref[grp_rows(it), :])

        n_multi = n_far // SEL_STATES
        states = lax.fori_loop(0, n_multi, multi_body, tuple(_softmax_init(r_l) for _ in range(SEL_STATES)))
        states = (lax.fori_loop(SEL_STATES * n_multi, n_it, single_body, states[0]),) + states[1:]
        m_all = states[0][0]
        for st_k in states[1:]:
            m_all = jnp.maximum(m_all, st_k[0])
        l_all, acc_all = None, None
        for m_k, l_k, acc_k in states:
            f_k = jnp.exp2(m_k - m_all)
            l_all = f_k * l_k if l_all is None else l_all + f_k * l_k
            acc_all = f_k * acc_k if acc_all is None else acc_all + f_k * acc_k
        return m_all, l_all, acc_all

    heads = [before_selection(t) for t in range(PROMPT_TILES)]
    for t, (i_tile, q_t, o_c, win) in enumerate(heads):
        sel = selected_branch(t, i_tile, q_t)
        o_r = _combine(gt_ref[t], o_c, sel, win, qbp)
        for n in range(HPG):
            o_ref[t * qbp:(t + 1) * qbp, n * HEAD_DIM:(n + 1) * HEAD_DIM] = \
                o_r[n * qbp:(n + 1) * qbp].astype(o_ref.dtype)


SAMPLE_GROUP_COL = 64
SAMPLE_FAR_PARTS = 6


def _softmax_segments(segs):
    m = jnp.max(segs[0][0], axis=0, keepdims=True)
    for st, _ in segs[1:]:
        m = jnp.maximum(m, jnp.max(st, axis=0, keepdims=True))
    l_sum, acc = None, None
    for st, v in segs:
        p = jnp.exp2(st - m)
        l_part, a_part = jnp.sum(p, axis=0, keepdims=True), _pv(v, p)
        l_sum = l_part if l_sum is None else l_sum + l_part
        acc = a_part if acc is None else acc + a_part
    return m, l_sum, acc


def _attn_sample_kernel(pt_ref, q_ref, gt_ref, kc_ref, vc_ref, kw_ref, vw_ref, kst_ref, vst_ref, kwt_ref, vwt_ref,
                        img_ref, ct_ref, feat_ref, ks_hbm, vs_hbm, o_ref, kwo_ref, vwo_ref, selm_ref, kbuf, vbuf,
                        sem, *, qb, nb):
    b = pl.program_id(0)
    slot = _prefetch_pages(pt_ref, b, pl.num_programs(0), [(ks_hbm, kbuf, sem.at[0]), (vs_hbm, vbuf, sem.at[1])])
    lanes = N_HEADS * qb
    past_rows = kbuf.shape[1]
    past = past_rows // N_KV
    tile_rows = KEY_TILE * N_KV
    tail_rows = kst_ref.shape[0]
    win_rows = kw_ref.shape[0]
    cmp_rows = kc_ref.shape[0]

    q = q_ref[...] * Q_SCALE
    q_t = jnp.concatenate([q[:, h * HEAD_DIM:(h + 1) * HEAD_DIM] for h in range(N_HEADS)], axis=0).T.astype(BF16)
    lane = lax.broadcasted_iota(jnp.int32, (N_KV, lanes), 1)
    grp_rows = jnp.where(lane // (HPG * qb) == lax.broadcasted_iota(jnp.int32, (N_KV, lanes), 0), 0.0, NEG)

    q_aug = jnp.concatenate(
        [q_t, jnp.concatenate([jnp.zeros((SAMPLE_GROUP_COL, lanes), F32), grp_rows,
                               jnp.zeros((KEY_TILE - SAMPLE_GROUP_COL - N_KV, lanes), F32)], axis=0).astype(BF16)],
        axis=0)

    def scores(k_rows, feat):
        return _dot(jnp.concatenate([k_rows.astype(BF16), feat], axis=1), q_aug)

    far = past_rows - tile_rows
    part = far // SAMPLE_FAR_PARTS
    half = win_rows // 2
    sel_rows = [slice(r0, r0 + part) for r0 in range(0, far, part)] + [slice(far, past_rows)]
    pending = [lambda rows=rows: scores(kbuf[slot, rows, :], feat_ref[rows, :]) for rows in sel_rows]
    pending += [lambda: scores(kst_ref[...], feat_ref[0:tail_rows, :]),
                lambda: scores(kw_ref[0:half, :], feat_ref[0:half, :]),
                lambda: scores(kw_ref[half:win_rows, :], feat_ref[half:win_rows, :]),
                lambda: scores(kwt_ref[...], feat_ref[0:tail_rows, :])]
    raw = []

    def issue(n=1):
        for _ in range(n):
            if len(raw) < len(pending):
                raw.append(pending[len(raw)]())

    s = scores(kc_ref[...], feat_ref[0:cmp_rows, :]) + ct_ref[...]
    issue()
    mx = jnp.max(s, axis=0, keepdims=True)
    mx = jnp.where(mx > 0.5 * NEG, mx, 0.0)
    e = jnp.where(s > 0.5 * NEG, jnp.exp2(s - mx), 0.0)
    p_c = e / jnp.maximum(jnp.sum(e, axis=0, keepdims=True), 1e-30)
    o_c = _pv(vc_ref[...], p_c)
    issue()

    l32 = lax.broadcasted_iota(jnp.int32, p_c.shape, 1) % (HPG * qb)
    psum = p_c
    for n in range(1, HPG):
        sh = n * qb
        psum = psum + jnp.where(l32 >= sh, pltpu.roll(p_c, sh, 1), pltpu.roll(p_c, sh + lanes - HPG * qb, 1))
    rows_per_block = SEL_BLOCK // CMP_STRIDE * N_KV
    nbi = cmp_rows // rows_per_block
    imp = jnp.sum(psum.reshape(nbi, rows_per_block, lanes), axis=1)
    imp = jnp.concatenate([imp, jnp.zeros((nb - nbi, lanes), F32)], axis=0)
    jb = lax.broadcasted_iota(jnp.int32, (nb, lanes), 0)
    cur = (past + lax.broadcasted_iota(jnp.int32, (nb, lanes), 1) % qb) // SEL_BLOCK
    forced = (jb == 0) | (jb == cur) | (jb == cur - 1)
    score = jnp.where(jb <= cur, jnp.where(forced, FORCE_SCORE, imp), NEG)
    jbf = jb.astype(F32)
    selm = jnp.full((nb, lanes), NEG, F32)
    for rnd in range(N_SEL):
        best = jnp.max(score, axis=0, keepdims=True)
        first = jnp.min(jnp.where(score == best, jbf, 1e9), axis=0, keepdims=True)
        pick = jbf == first
        selm = jnp.where(pick, 0.0, selm)
        score = jnp.where(pick, 3.0 * NEG, score)
        if rnd % 2 == 1:
            issue()
    issue(len(pending))
    sel_st = raw[:len(sel_rows)]
    sel_st[-1] = sel_st[-1] + img_ref[IMG_PREV]
    st_new, st_old, st_mid, st_wn = raw[len(sel_rows):]
    st_new = st_new + img_ref[IMG_DIAG, 0:tail_rows]
    st_old = jnp.concatenate([st_old[:tile_rows] + img_ref[IMG_EDGE], st_old[tile_rows:]], axis=0)
    st_mid = jnp.concatenate([st_mid[:half - tile_rows], st_mid[half - tile_rows:] + img_ref[IMG_PREV]], axis=0)
    st_wn = st_wn + img_ref[IMG_DIAG, 0:tail_rows]

    for j in range(nb):
        selm_ref[j] = jnp.broadcast_to(selm[j:j + 1, :], (8, lanes))
    block_rows = SEL_BLOCK * N_KV

    def select(st, row0):
        n_rows = st.shape[0]
        nblk = -(-n_rows // block_rows)
        per = n_rows // nblk // 8
        mask = selm_ref[row0 // block_rows:row0 // block_rows + nblk]
        return (st.reshape(nblk, per, 8, lanes) + mask[:, None]).reshape(n_rows, lanes)

    segs = [(select(st, rows.start), vbuf[slot, rows, :].astype(BF16)) for st, rows in zip(sel_st, sel_rows)]
    segs.append((select(st_new, past_rows), vst_ref[...].astype(BF16)))
    sel = _softmax_segments(segs)

    win = _softmax_segments([(st_old, vw_ref[0:half, :].astype(BF16)),
                             (st_mid, vw_ref[half:win_rows, :].astype(BF16)),
                             (st_wn, vwt_ref[...].astype(BF16))])

    o_r = _combine(gt_ref[...], o_c, sel, win, qb)
    for h in range(N_HEADS):
        o_ref[:, h * HEAD_DIM:(h + 1) * HEAD_DIM] = o_r[h * qb:(h + 1) * qb]

    for src, new, dst in ((kw_ref, kwt_ref, kwo_ref), (vw_ref, vwt_ref, vwo_ref)):
        dst[0:win_rows - tail_rows, :] = src[tail_rows:win_rows, :]
        dst[win_rows - tail_rows:win_rows, :] = new[...]


def _attn_prompt(qkv_f, qkv_b, gates_t, kcmp, vcmp, img, ct, seq, off2):
    qbp = KEY_TILE
    r_l = HPG * qbp
    n_tiles = seq // qbp
    nc = seq // CMP_STRIDE
    nb = seq // SEL_BLOCK
    q_cols = N_HEADS * HEAD_DIM // HEAD_DIM

    def kv_spec(which):
        return pl.BlockSpec((seq, HEAD_DIM), functools.partial(lambda g, i, c: (0, c + g), c=q_cols + which * N_KV))

    cmp_spec = pl.BlockSpec((nc, HEAD_DIM), lambda g, i: (0, g))
    return pl.pallas_call(
        functools.partial(_attn_prompt_kernel, nc=nc, nb=nb, off2=off2),
        grid=(N_KV, n_tiles // PROMPT_TILES),
        in_specs=[pl.BlockSpec((PROMPT_TILES * qbp, HPG * HEAD_DIM), lambda g, i: (i, g)),
                  pl.BlockSpec((PROMPT_TILES, None, 3, r_l), lambda g, i: (i, g, 0, 0)),
                  cmp_spec, cmp_spec,
                  kv_spec(2), kv_spec(3), kv_spec(4), kv_spec(5),
                  pl.BlockSpec((None, N_IMG, KEY_TILE, r_l), lambda g, i: (g, 0, 0, 0)),
                  pl.BlockSpec((None, ct.shape[1], r_l), lambda g, i: (g, 0, 0))],
        out_specs=pl.BlockSpec((PROMPT_TILES * qbp, HPG * HEAD_DIM), lambda g, i: (i, g)),
        out_shape=jax.ShapeDtypeStruct((seq, N_HEADS * HEAD_DIM), BF16),
        scratch_shapes=[pltpu.VMEM((PROMPT_TILES, nb * SEL_BLOCK // (SEL_GROUP * KEY_TILE), 16, r_l), BF16),
                        pltpu.VMEM((PROMPT_TILES, nc, KEY_TILE), F32)],
        compiler_params=_params(("arbitrary", "arbitrary"), 48),
        name="nsa_prompt",
    )(qkv_f, gates_t, kcmp, vcmp, qkv_b, qkv_b, qkv_b, qkv_b, img, ct)


def _key_features(n_rows):
    r = jnp.arange(n_rows, dtype=jnp.int32)[:, None]
    c = jnp.arange(KEY_TILE, dtype=jnp.int32)[None, :]
    hit = (c == r // (SEL_BLOCK * N_KV)) | (c == SAMPLE_GROUP_COL + r % N_KV)
    return hit.astype(BF16)


def _attn_sample(page_table, q_new, gates_t, kcmp, vcmp, kw, vw, ks_t, vs_t, kw_t, vw_t, img, ct,
                 ks_cache, vs_cache, qb):
    n_seq, n_pages = page_table.shape
    past = n_pages * PAGE
    lanes = N_HEADS * qb
    nb = 8 * (-(-(past // SEL_BLOCK + 2) // 8))
    assert nb <= SAMPLE_GROUP_COL and lanes == KEY_TILE
    qw = N_HEADS * HEAD_DIM
    tail_rows = ks_t.shape[1]
    cmp_rows = past // CMP_STRIDE * N_KV
    feat = _key_features(past * N_KV + tail_rows)
    tail_spec = pl.BlockSpec((None, tail_rows, HEAD_DIM), lambda b, pt: (b, 0, 0))
    win_spec = pl.BlockSpec((WINDOW * N_KV, HEAD_DIM), lambda b, pt: (b, 0))
    cmp_spec = pl.BlockSpec((cmp_rows, HEAD_DIM), lambda b, pt: (b, 0))
    any_spec = pl.BlockSpec(memory_space=pl.ANY)
    return pl.pallas_call(
        functools.partial(_attn_sample_kernel, qb=qb, nb=nb),
        grid_spec=pltpu.PrefetchScalarGridSpec(
            num_scalar_prefetch=1, grid=(n_seq,),
            in_specs=[pl.BlockSpec((qb, qw), lambda b, pt: (b, 0)),
                      pl.BlockSpec((None, 3, lanes), lambda b, pt: (b, 0, 0)),
                      cmp_spec, cmp_spec, win_spec, win_spec,
                      tail_spec, tail_spec, tail_spec, tail_spec,
                      pl.BlockSpec((None,) + img.shape[1:], lambda b, pt: (0, 0, 0, 0)),
                      pl.BlockSpec((None,) + ct.shape[1:], lambda b, pt: (0, 0, 0)),
                      pl.BlockSpec(feat.shape, lambda b, pt: (0, 0)),
                      any_spec, any_spec],
            out_specs=[pl.BlockSpec((qb, qw), lambda b, pt: (b, 0)), win_spec, win_spec],
            scratch_shapes=[pltpu.VMEM((nb, 8, lanes), F32),
                            pltpu.VMEM((2, n_pages * PAGE_ROWS, HEAD_DIM), F32),
                            pltpu.VMEM((2, n_pages * PAGE_ROWS, HEAD_DIM), F32),
                            pltpu.SemaphoreType.DMA((2, 2))]),
        out_shape=[jax.ShapeDtypeStruct((n_seq * qb, qw), F32),
                   jax.ShapeDtypeStruct(kw.shape, kw.dtype), jax.ShapeDtypeStruct(vw.shape, vw.dtype)],
        compiler_params=_params(("arbitrary",), 56),
        name="nsa_sample",
    )(page_table, q_new, gates_t, kcmp, vcmp, kw, vw, ks_t, vs_t, kw_t, vw_t, img, ct, feat, ks_cache, vs_cache)


def _gates_to_lanes(g_sig, n_tiles, q_rows, qbp):
    g5 = g_sig.reshape(n_tiles, q_rows, 3, N_KV, HPG)
    if qbp > q_rows:
        g5 = jnp.pad(g5, ((0, 0), (0, qbp - q_rows), (0, 0), (0, 0), (0, 0)))
    return g5.transpose(0, 3, 2, 4, 1).reshape(n_tiles, N_KV, 3, HPG * qbp)


def _history_rows(state, dec_seq):
    n_seq, _, c = state.shape
    z = jnp.zeros((n_seq, dec_seq - 2, c), state.dtype)
    s1 = jnp.concatenate([state[:, 1:2], jnp.zeros((n_seq, 1, c), state.dtype), z], axis=1)
    s2 = jnp.concatenate([state, z], axis=1)
    return s1.reshape(n_seq * dec_seq, c), s2.reshape(n_seq * dec_seq, c)


def _cmp_weights(pe, w1, w2):
    half = CMP_STRIDE * HEAD_DIM
    w1cat = jnp.concatenate([w1[:half], w1[half:]], axis=1).astype(BF16)
    return pe.reshape(2, half), w1cat, w2.astype(BF16)


def _token_group_rows(x):
    return x.reshape(-1, HEAD_DIM)


def kernel(x_prompt, x_sample, cache_k_cmp, cache_v_cmp, cache_k_sel, cache_v_sel, state_k_win, state_v_win,
           state_conv, state_ffn_conv, page_table, rel_bias, w_in, conv_w, conv_b, w_br_conv, w_br_nsa, w_out,
           pe_cmp_k, w_cmp_k1, w_cmp_k2, pe_cmp_v, w_cmp_v1, w_cmp_v2, ln1_g, ln1_b, w_ffn_in, ffn_conv_w,
           ffn_conv_b, w_ffn_out, ln2_g, ln2_b):
    seq = x_prompt.shape[1]
    n_seq, dec_seq = x_sample.shape[0], x_sample.shape[1]
    n_s = n_seq * dec_seq
    past = page_table.shape[1] * PAGE
    kvw = N_KV * HEAD_DIM
    qw = N_HEADS * HEAD_DIM
    tm = 1024

    x_s2 = x_sample.reshape(n_s, D_MODEL)
    x_bf = jnp.concatenate([x_prompt[0].astype(BF16), x_s2.astype(BF16)], axis=0)
    c_qkv = 3 * D_CONV
    c_gate = c_qkv + qw + 6 * kvw
    c_mix = c_gate + 3 * N_HEADS
    w_main = w_in[0].astype(BF16)
    w_gate = jnp.pad(w_main[:, c_gate:c_mix], ((0, 0), (0, 128 - 3 * N_HEADS)))
    w_mix = w_main[:, c_mix:]

    s1, s2 = _history_rows(state_conv[0], dec_seq)
    tn_c = 256
    z_conv, u_conv = _gconv(x_bf, w_main, (0, D_CONV // tn_c, 2 * D_CONV // tn_c), conv_w[0], conv_b, s1, s2,
                            seq, tm, tn_c, 48, "proj_conv")
    qkv_f, qkv_b = _proj(x_bf, w_main, c_qkv, qw + 6 * kvw, tm, 512)
    g_sig = _gate_proj(x_bf, w_gate, tm)[:, :3 * N_HEADS]

    pe_k, w1_k, w2_k = _cmp_weights(pe_cmp_k[0], w_cmp_k1[0], w_cmp_k2[0])
    pe_v, w1_v, w2_v = _cmp_weights(pe_cmp_v[0], w_cmp_v1[0], w_cmp_v2[0])
    chunk_w = CMP_STRIDE * kvw
    kc_p = qkv_b[:seq, qw:qw + kvw].reshape(seq // CMP_STRIDE, chunk_w)
    vc_p = qkv_b[:seq, qw + kvw:qw + 2 * kvw].reshape(seq // CMP_STRIDE, chunk_w)
    kcmp_p = _compress(kc_p, pe_k, w1_k, w2_k, "compress_k_prompt")
    vcmp_p = _compress(vc_p, pe_v, w1_v, w2_v, "compress_v_prompt")
    kcmp_s, vcmp_s = _compress_paged(page_table, _token_group_rows(cache_k_cmp), _token_group_rows(cache_v_cmp),
                                     (pe_k, w1_k, w2_k), (pe_v, w1_v, w2_v))

    off2_p = (seq - KEY_TILE) // CMP_STRIDE
    nct_p = KEY_TILE * (-(-(off2_p + seq // CMP_STRIDE) // KEY_TILE))
    img_p, ct_p = _bias_images(rel_bias, HPG, KEY_TILE, 1, nct_p, off2_p, "bias_prompt")
    gates_p = _gates_to_lanes(g_sig[:seq], seq // KEY_TILE, KEY_TILE, KEY_TILE)
    o_p = _attn_prompt(qkv_f, qkv_b, gates_p, kcmp_p, vcmp_p, img_p, ct_p, seq, off2_p)

    chunks_s = past // CMP_STRIDE
    img_s, ct_s = _bias_images(rel_bias, N_HEADS, dec_seq, N_KV, chunks_s, chunks_s, "bias_sample")
    gates_s = g_sig[seq:].reshape(n_seq, dec_seq, 3, N_HEADS).transpose(0, 2, 3, 1).reshape(n_seq, 3, N_HEADS * dec_seq)
    new_f = qkv_f[seq:].reshape(n_seq, dec_seq, qw + 6 * kvw)

    def tail(col):
        return new_f[:, :, col:col + kvw].reshape(n_seq, dec_seq * N_KV, HEAD_DIM)

    o_s, kw_next, vw_next = _attn_sample(page_table, qkv_f[seq:, :qw], gates_s,
                       kcmp_s.reshape(-1, HEAD_DIM), vcmp_s.reshape(-1, HEAD_DIM),
                       _token_group_rows(state_k_win), _token_group_rows(state_v_win),
                       tail(qw + 2 * kvw), tail(qw + 3 * kvw), tail(qw + 4 * kvw), tail(qw + 5 * kvw),
                       img_s, ct_s, _token_group_rows(cache_k_sel), _token_group_rows(cache_v_sel), dec_seq)

    mix = _merge(x_bf, z_conv, o_p, o_s, w_br_conv[0], w_br_nsa[0], w_mix, 512, 256)
    r1 = _resid_mm_split(mix, w_out[0].astype(BF16), x_prompt[0], x_s2, tm, 512, 48, "out_proj")
    h_f, h_b = _layer_norm(r1, ln1_g, ln1_b, 256, "ln1")
    f1, f2 = _history_rows(state_ffn_conv[0], dec_seq)
    tn_f = 256
    act, gp = _gconv(h_b, w_ffn_in[0], (D_FF // tn_f, 0), ffn_conv_w[0], ffn_conv_b, f1, f2,
                     seq, tm, tn_f, 56, "ffn_in")
    r2 = _resid_mm(act, w_ffn_out[0].astype(BF16), h_f, 512, 256, 56, "ffn_out")
    y_p, y_s = _layer_norm_split(r2, ln2_g, ln2_b, seq, 256, "ln2")

    def rows_p(col):
        return qkv_f[:seq, col:col + kvw].reshape(1, 1, seq, N_KV, HEAD_DIM)

    def rows_s(col):
        return new_f[:, :, col:col + kvw].reshape(1, n_seq, dec_seq, N_KV, HEAD_DIM)

    def win_p(col):
        return qkv_f[seq - WINDOW:seq, col:col + kvw].reshape(1, 1, WINDOW, N_KV, HEAD_DIM)


    def last2(u):
        c = u.shape[1]
        return (u[seq - 2:seq].reshape(1, 1, 2, c),
                u[seq:].reshape(n_seq, dec_seq, c)[:, dec_seq - 2:].reshape(1, n_seq, 2, c))

    conv_p, conv_s = last2(u_conv)
    ffn_p, ffn_s = last2(gp)
    c = qw
    return (y_p.reshape(1, seq, D_MODEL), y_s.reshape(n_seq, dec_seq, D_MODEL),
            rows_p(c), rows_s(c), rows_p(c + kvw), rows_s(c + kvw),
            rows_p(c + 2 * kvw), rows_s(c + 2 * kvw), rows_p(c + 3 * kvw), rows_s(c + 3 * kvw),
            win_p(c + 4 * kvw), kw_next.reshape(state_k_win.shape),
            win_p(c + 5 * kvw), vw_next.reshape(state_v_win.shape),
            conv_p, conv_s, ffn_p, ffn_s)
```

```python
import functools
import math

import jax
import jax.numpy as jnp
from jax import lax
from jax.experimental import pallas as pl
from jax.experimental.pallas import tpu as pltpu

F32 = jnp.float32
BF16 = jnp.bfloat16

D_MODEL = 4096
D_CONV = 2048
N_HEADS = 16
HEAD_DIM = 128
N_KV = 4
HPG = 4
CMP_STRIDE = 16
CMP_HID = 256
SEL_BLOCK = 64
N_SEL = 16
WINDOW = 512
FORCE_SCORE = 1e4
REL_BUCKETS = 32
REL_EXACT = 16
REL_MAX_DIST = 128
D_FF = 11008
PAGE = 128
ALPHA = 2.0 ** 0.25
LN_EPS = 1e-5
LOG2E = 1.0 / math.log(2.0)
Q_SCALE = HEAD_DIM ** -0.5 * LOG2E

KEY_TILE = 128
SEL_GROUP = 4
GCONV_PARTS = 2
PROMPT_TILES = 4
SEL_STATES = 3
PAGE_ROWS = PAGE * N_KV
NEG = -1e30
MIB = 1024 * 1024

IMG_DIAG, IMG_PREV, IMG_ZERO, IMG_EDGE, IMG_NEG = 0, 1, 2, 3, 4
N_IMG = 5


def _params(sem, vmem_mib):
    return pltpu.CompilerParams(dimension_semantics=sem, vmem_limit_bytes=vmem_mib * MIB)


def _dot(a, b):
    return jnp.dot(a, b, preferred_element_type=F32)


def _proj_kernel(x_ref, w_ref, of_ref, ob_ref):
    acc = _dot(x_ref[...], w_ref[...].astype(BF16))
    of_ref[...] = acc
    ob_ref[...] = acc.astype(BF16)


def _proj(x, w, col0, ncols, tm, tn):
    m_rows, k = x.shape
    c0 = col0 // tn
    return pl.pallas_call(
        _proj_kernel,
        grid=(m_rows // tm, ncols // tn),
        in_specs=[pl.BlockSpec((tm, k), lambda m, n: (m, 0)),
                  pl.BlockSpec((k, tn), lambda m, n: (0, c0 + n))],
        out_specs=[pl.BlockSpec((tm, tn), lambda m, n: (m, n)),
                   pl.BlockSpec((tm, tn), lambda m, n: (m, n))],
        out_shape=[jax.ShapeDtypeStruct((m_rows, ncols), F32),
                   jax.ShapeDtypeStruct((m_rows, ncols), BF16)],
        compiler_params=_params(("arbitrary", "arbitrary"), 56),
        name="proj_qkv",
    )(x, w)


def _gate_kernel(x_ref, w_ref, o_ref):
    o_ref[...] = jax.nn.sigmoid(_dot(x_ref[...], w_ref[...]))


def _gate_proj(x, w, tm):
    m_rows, k = x.shape
    n = w.shape[1]
    return pl.pallas_call(
        _gate_kernel,
        grid=(m_rows // tm,),
        in_specs=[pl.BlockSpec((tm, k), lambda m: (m, 0)),
                  pl.BlockSpec((k, n), lambda m: (0, 0))],
        out_specs=pl.BlockSpec((tm, n), lambda m: (m, 0)),
        out_shape=jax.ShapeDtypeStruct((m_rows, n), F32),
        compiler_params=_params(("arbitrary",), 40),
        name="proj_gates",
    )(x, w)


def _gconv_kernel(*refs, three, n_prompt_tiles):
    if three:
        x_ref, w0_ref, w1_ref, w2_ref, cw_ref, cb_ref, s1_ref, s2_ref, z_ref, u_ref, carry_ref = refs
    else:
        x_ref, w0_ref, w1_ref, cw_ref, cb_ref, s1_ref, s2_ref, z_ref, u_ref, carry_ref = refs
    m = pl.program_id(0)
    n = pl.program_id(1)
    is_sample = m >= n_prompt_tiles
    tm, tn = u_ref.shape
    hm = tm // GCONV_PARTS
    cw = cw_ref[...]
    cb = cb_ref[...]
    c = carry_ref[n]
    c = jnp.where(m == 0, jnp.zeros_like(c), c)
    prev_a, prev_b = c[6:7], c[7:8]
    row = lax.broadcasted_iota(jnp.int32, (hm, tn), 0)
    rmask = row & jnp.where(is_sample, 7, hm - 1)
    w0 = w0_ref[...].astype(BF16)
    w1 = w1_ref[...].astype(BF16)
    w2 = w2_ref[...].astype(BF16) if three else None

    def project(part):
        x = x_ref[part * hm:(part + 1) * hm, :]
        u = _dot(x, w1)
        if three:
            u = u * _dot(x, w2)
        p0 = _dot(x, w0)
        return p0, u

    def epilogue(part, p0, u, prev_a, prev_b):
        rows = slice(part * hm, (part + 1) * hm)
        u_ref[rows, :] = u
        fill1 = jnp.where(is_sample, s1_ref[rows, :], prev_b)
        fill2 = jnp.where(is_sample, s2_ref[rows, :], jnp.where(row == 0, prev_a, prev_b))
        prev1 = jnp.where(rmask == 0, fill1, pltpu.roll(u, 1, 0))
        prev2 = jnp.where(rmask < 2, fill2, pltpu.roll(u, 2, 0))
        conv = cb + prev2 * cw[0:1] + prev1 * cw[1:2] + u * cw[2:3]
        if three:
            z = p0 * conv
        else:
            z = jax.nn.silu(conv) * p0
        z_ref[rows, :] = z.astype(z_ref.dtype)
        return u[hm - 2:hm - 1], u[hm - 1:hm]

    done = project(0)
    for part in range(1, GCONV_PARTS):
        nxt = project(part)
        prev_a, prev_b = epilogue(part - 1, *done, prev_a, prev_b)
        done = nxt
    epilogue(GCONV_PARTS - 1, *done, prev_a, prev_b)
    carry_ref[n] = done[1][hm - 8:hm]


def _gconv(x, w, col_blocks, cw, cb, s1, s2, n_prompt_rows, tm, tn, vmem_mib, name):
    m_rows, k = x.shape
    ncols = cw.shape[1]
    three = len(col_blocks) == 3
    assert m_rows - n_prompt_rows == tm and n_prompt_rows % tm == 0
    nt = ncols // tn
    w_specs = [pl.BlockSpec((k, tn), functools.partial(lambda m, n, c: (0, c + n), c=c)) for c in col_blocks]
    col_spec = pl.BlockSpec((tm, tn), lambda m, n: (m, n))
    npt = n_prompt_rows // tm
    hist_spec = pl.BlockSpec((tm, tn), lambda m, n: (0, jnp.where(m < npt, 0, n)))
    return pl.pallas_call(
        functools.partial(_gconv_kernel, three=three, n_prompt_tiles=npt),
        grid=(m_rows // tm, nt),
        in_specs=[pl.BlockSpec((tm, k), lambda m, n: (m, 0))] + w_specs + [
            pl.BlockSpec((3, tn), lambda m, n: (0, n)),
            pl.BlockSpec((1, tn), lambda m, n: (0, n)),
            hist_spec, hist_spec],
        out_specs=[col_spec, col_spec],
        out_shape=[jax.ShapeDtypeStruct((m_rows, ncols), BF16),
                   jax.ShapeDtypeStruct((m_rows, ncols), F32)],
        scratch_shapes=[pltpu.VMEM((nt, 8, tn), F32)],
        compiler_params=_params(("arbitrary", "arbitrary"), vmem_mib),
        name=name,
    )(x, *([w] * len(col_blocks)), cw, cb, s1, s2)


def _merge_kernel(x_ref, z_ref, op_ref, os_ref, wc_ref, wn_ref, wga_ref, wgb_ref, out_ref, *, n_prompt_tiles):
    x = x_ref[...]
    o = jnp.where(pl.program_id(0) < n_prompt_tiles, op_ref[...], os_ref[...].astype(BF16))
    ga = jax.nn.sigmoid(_dot(x, wga_ref[...]))
    gb = jax.nn.sigmoid(_dot(x, wgb_ref[...]))
    a = _dot(z_ref[...], wc_ref[...].astype(BF16))
    b = _dot(o, wn_ref[...].astype(BF16))
    out_ref[...] = (ga * a + gb * b).astype(BF16)


def _merge(x, z, o_p, o_s, wc, wn, wg, tm, tn):
    m_rows, k = x.shape
    kc = z.shape[1]
    nt = D_MODEL // tn
    npt = o_p.shape[0] // tm
    assert npt * tm == o_p.shape[0] and o_s.shape[0] % tm == 0 and o_p.shape[0] + o_s.shape[0] == m_rows
    return pl.pallas_call(
        functools.partial(_merge_kernel, n_prompt_tiles=npt),
        grid=(m_rows // tm, nt),
        in_specs=[pl.BlockSpec((tm, k), lambda m, n: (m, 0)),
                  pl.BlockSpec((tm, kc), lambda m, n: (m, 0)),
                  pl.BlockSpec((tm, kc), lambda m, n: (jnp.minimum(m, npt - 1), 0)),
                  pl.BlockSpec((tm, kc), lambda m, n: (jnp.maximum(m - npt, 0), 0)),
                  pl.BlockSpec((kc, tn), lambda m, n: (0, n)),
                  pl.BlockSpec((kc, tn), lambda m, n: (0, n)),
                  pl.BlockSpec((k, tn), lambda m, n: (0, n)),
                  pl.BlockSpec((k, tn), lambda m, n: (0, nt + n))],
        out_specs=pl.BlockSpec((tm, tn), lambda m, n: (m, n)),
        out_shape=jax.ShapeDtypeStruct((m_rows, D_MODEL), BF16),
        compiler_params=_params(("arbitrary", "arbitrary"), 56),
        name="merge_branches",
    )(x, z, o_p, o_s, wc, wn, wg, wg)


def _resid_kernel(l_ref, w_ref, r_ref, o_ref):
    o_ref[...] = ALPHA * r_ref[...] + _dot(l_ref[...], w_ref[...])


def _resid2_kernel(l_ref, w_ref, rp_ref, rs_ref, o_ref, *, n_prompt_tiles):
    res = jnp.where(pl.program_id(0) < n_prompt_tiles, rp_ref[...], rs_ref[...])
    o_ref[...] = ALPHA * res + _dot(l_ref[...], w_ref[...].astype(BF16))


def _resid_mm_split(lhs, w, res_p, res_s, tm, tn, vmem_mib, name):
    m_rows, k = lhs.shape
    n_cols = w.shape[1]
    npt = res_p.shape[0] // tm
    assert res_s.shape[0] == tm and npt * tm + tm == m_rows
    return pl.pallas_call(
        functools.partial(_resid2_kernel, n_prompt_tiles=npt),
        grid=(m_rows // tm, n_cols // tn),
        in_specs=[pl.BlockSpec((tm, k), lambda m, n: (m, 0)),
                  pl.BlockSpec((k, tn), lambda m, n: (0, n)),
                  pl.BlockSpec((tm, tn), lambda m, n: (jnp.minimum(m, npt - 1), n)),
                  pl.BlockSpec((tm, tn), lambda m, n: (0, n))],
        out_specs=pl.BlockSpec((tm, tn), lambda m, n: (m, n)),
        out_shape=jax.ShapeDtypeStruct((m_rows, n_cols), F32),
        compiler_params=_params(("arbitrary", "arbitrary"), vmem_mib),
        name=name,
    )(lhs, w, res_p, res_s)


def _resid_mm(lhs, w, res, tm, tn, vmem_mib, name):
    m_rows, k = lhs.shape
    n_cols = w.shape[1]
    return pl.pallas_call(
        _resid_kernel,
        grid=(m_rows // tm, n_cols // tn),
        in_specs=[pl.BlockSpec((tm, k), lambda m, n: (m, 0)),
                  pl.BlockSpec((k, tn), lambda m, n: (0, n)),
                  pl.BlockSpec((tm, tn), lambda m, n: (m, n))],
        out_specs=pl.BlockSpec((tm, tn), lambda m, n: (m, n)),
        out_shape=jax.ShapeDtypeStruct((m_rows, n_cols), F32),
        compiler_params=_params(("arbitrary", "arbitrary"), vmem_mib),
        name=name,
    )(lhs, w, res)


def _ln_kernel(x_ref, g_ref, b_ref, of_ref, ob_ref):
    x = x_ref[...]
    mu = jnp.mean(x, axis=-1, keepdims=True)
    xc = x - mu
    var = jnp.mean(xc * xc, axis=-1, keepdims=True)
    y = xc * lax.rsqrt(var + LN_EPS) * g_ref[...] + b_ref[...]
    of_ref[...] = y
    ob_ref[...] = y.astype(BF16)


def _layer_norm(x, g, b, tr, name):
    m_rows, d = x.shape
    row_spec = pl.BlockSpec((tr, d), lambda m: (m, 0))
    vec_spec = pl.BlockSpec((1, d), lambda m: (0, 0))
    return pl.pallas_call(
        _ln_kernel,
        grid=(m_rows // tr,),
        in_specs=[row_spec, vec_spec, vec_spec],
        out_specs=[row_spec, row_spec],
        out_shape=[jax.ShapeDtypeStruct((m_rows, d), F32), jax.ShapeDtypeStruct((m_rows, d), BF16)],
        compiler_params=_params(("arbitrary",), 48),
        name=name,
    )(x, g, b)


def _ln_split_kernel(x_ref, g_ref, b_ref, yp_ref, ys_ref, *, n_prompt_tiles):
    x = x_ref[...]
    mu = jnp.mean(x, axis=-1, keepdims=True)
    xc = x - mu
    var = jnp.mean(xc * xc, axis=-1, keepdims=True)
    y = xc * lax.rsqrt(var + LN_EPS) * g_ref[...] + b_ref[...]
    m = pl.program_id(0)

    @pl.when(m < n_prompt_tiles)
    def _():
        yp_ref[...] = y

    @pl.when(m >= n_prompt_tiles)
    def _():
        ys_ref[...] = y


def _layer_norm_split(x, g, b, n_prompt_rows, tr, name):
    m_rows, d = x.shape
    npt = n_prompt_rows // tr
    vec_spec = pl.BlockSpec((1, d), lambda m: (0, 0))
    return pl.pallas_call(
        functools.partial(_ln_split_kernel, n_prompt_tiles=npt),
        grid=(m_rows // tr,),
        in_specs=[pl.BlockSpec((tr, d), lambda m: (m, 0)), vec_spec, vec_spec],
        out_specs=[pl.BlockSpec((tr, d), lambda m: (jnp.minimum(m, npt - 1), 0)),
                   pl.BlockSpec((tr, d), lambda m: (jnp.maximum(m - npt, 0), 0))],
        out_shape=[jax.ShapeDtypeStruct((n_prompt_rows, d), F32),
                   jax.ShapeDtypeStruct((m_rows - n_prompt_rows, d), F32)],
        compiler_params=_params(("arbitrary",), 48),
        name=name,
    )(x, g, b)


def _page_copies(pt_ref, seq, cache_ref, buf, slot, sem):
    return [pltpu.make_async_copy(
        cache_ref.at[pl.ds(pl.multiple_of(pt_ref[seq, j] * PAGE_ROWS, PAGE_ROWS), PAGE_ROWS)],
        buf.at[slot, pl.ds(j * PAGE_ROWS, PAGE_ROWS)], sem)
        for j in range(pt_ref.shape[1])]


def _prefetch_pages(pt_ref, b, n_seq, streams):
    slot = b % 2

    @pl.when(b == 0)
    def _():
        for cache_ref, buf, sem in streams:
            for cp in _page_copies(pt_ref, 0, cache_ref, buf, 0, sem.at[0]):
                cp.start()

    @pl.when(b + 1 < n_seq)
    def _():
        for cache_ref, buf, sem in streams:
            for cp in _page_copies(pt_ref, b + 1, cache_ref, buf, 1 - slot, sem.at[1 - slot]):
                cp.start()

    for cache_ref, buf, sem in streams:
        for cp in _page_copies(pt_ref, b, cache_ref, buf, slot, sem.at[slot]):
            cp.wait()
    return slot


def _compress_rows(xg, pe, w1, w2, next_chunk=1):
    return _compress_finish(_compress_first_layer(xg, pe, w1), w2, next_chunk)


def _compress_first_layer(xg, pe, w1):
    half = CMP_STRIDE * HEAD_DIM
    pa = _dot(jnp.broadcast_to(pe[0:1], (8, half)).astype(BF16), w1[:, :CMP_HID])[0:1]
    pb = _dot(jnp.broadcast_to(pe[1:2], (8, half)).astype(BF16), w1[:, CMP_HID:])[0:1]
    fs = _dot(xg, w1)
    return fs[:, :CMP_HID] + pa, fs[:, CMP_HID:] + pb


def _compress_finish(halves, w2, next_chunk):
    first, second = halves
    hid = jax.nn.gelu(first + pltpu.roll(second, second.shape[0] - next_chunk, 0))
    return _dot(hid.astype(BF16), w2)


def _compress_kernel(x_ref, pe_ref, w1_ref, w2_ref, o_ref):
    row_w = N_KV * HEAD_DIM
    for g in range(N_KV):
        xg = jnp.concatenate(
            [x_ref[:, r * row_w + g * HEAD_DIM: r * row_w + (g + 1) * HEAD_DIM] for r in range(CMP_STRIDE)],
            axis=1).astype(BF16)
        o_ref[:, g * HEAD_DIM:(g + 1) * HEAD_DIM] = _compress_rows(
            xg, pe_ref[...], w1_ref[...], w2_ref[...]).astype(BF16)


def _compress(x2, pe2, w1cat, w2, name):
    rows, width = x2.shape
    return pl.pallas_call(
        _compress_kernel,
        grid=(1,),
        in_specs=[pl.BlockSpec((rows, width), lambda t: (0, 0)),
                  pl.BlockSpec(pe2.shape, lambda t: (0, 0)),
                  pl.BlockSpec(w1cat.shape, lambda t: (0, 0)),
                  pl.BlockSpec(w2.shape, lambda t: (0, 0))],
        out_specs=pl.BlockSpec((rows, N_KV * HEAD_DIM), lambda t: (0, 0)),
        out_shape=jax.ShapeDtypeStruct((rows, N_KV * HEAD_DIM), BF16),
        compiler_params=_params(("arbitrary",), 48),
        name=name,
    )(x2, pe2, w1cat, w2)


def _compress_paged_kernel(pt_ref, kc_hbm, vc_hbm, pek_ref, w1k_ref, w2k_ref, pev_ref, w1v_ref, w2v_ref,
                           ok_ref, ov_ref, kbuf, vbuf, sem, *, chunks):
    b = pl.program_id(0)
    slot = _prefetch_pages(pt_ref, b, pl.num_programs(0), [(kc_hbm, kbuf, sem.at[0]), (vc_hbm, vbuf, sem.at[1])])

    def chunk_rows(buf):
        x3 = buf[slot].reshape(chunks, CMP_STRIDE * N_KV, HEAD_DIM)
        return jnp.concatenate(
            [x3[:, r * N_KV:(r + 1) * N_KV, :].reshape(chunks * N_KV, HEAD_DIM) for r in range(CMP_STRIDE)],
            axis=1).astype(BF16)

    hk = _compress_first_layer(chunk_rows(kbuf), pek_ref[...], w1k_ref[...])
    hv = _compress_first_layer(chunk_rows(vbuf), pev_ref[...], w1v_ref[...])
    ok_ref[...] = _compress_finish(hk, w2k_ref[...], N_KV).astype(BF16)
    ov_ref[...] = _compress_finish(hv, w2v_ref[...], N_KV).astype(BF16)


def _compress_paged(page_table, kc_cache, vc_cache, wk, wv):
    n_seq, n_pages = page_table.shape
    chunks = n_pages * PAGE // CMP_STRIDE
    w_specs = [pl.BlockSpec(w.shape, lambda b, pt: (0, 0)) for w in wk + wv]
    out_spec = pl.BlockSpec((chunks * N_KV, HEAD_DIM), lambda b, pt: (b, 0))
    out_shape = jax.ShapeDtypeStruct((n_seq * chunks * N_KV, HEAD_DIM), BF16)
    buf = pltpu.VMEM((2, n_pages * PAGE_ROWS, HEAD_DIM), F32)
    return pl.pallas_call(
        functools.partial(_compress_paged_kernel, chunks=chunks),
        grid_spec=pltpu.PrefetchScalarGridSpec(
            num_scalar_prefetch=1, grid=(n_seq,),
            in_specs=[pl.BlockSpec(memory_space=pl.ANY), pl.BlockSpec(memory_space=pl.ANY)] + w_specs,
            out_specs=[out_spec, out_spec],
            scratch_shapes=[buf, buf, pltpu.SemaphoreType.DMA((2, 2))]),
        out_shape=[out_shape, out_shape],
        compiler_params=_params(("arbitrary",), 48),
        name="compress_sample",
    )(page_table, kc_cache, vc_cache, *wk, *wv)


def _bias_kernel(rb_ref, img_ref, ct_ref, *, heads, qbp, rpk, off2):
    g = pl.program_id(0)
    r_l = heads * qbp
    lane = lax.broadcasted_iota(jnp.int32, (1, r_l), 1)
    n_l = lane // qbp
    tbl = []
    for k in range(REL_BUCKETS):
        row = jnp.zeros((1, r_l), F32)
        for n in range(heads):
            row = jnp.where(n_l == n, rb_ref[k, g * heads + n], row)
        tbl.append(row)

    def bias_of(dist):
        nn = jnp.maximum(dist, 0)
        nf = jnp.maximum(nn, 1).astype(F32)
        large = REL_EXACT + (jnp.log(nf / REL_EXACT) / math.log(REL_MAX_DIST / REL_EXACT)
                             * (REL_BUCKETS - REL_EXACT)).astype(jnp.int32)
        bucket = jnp.where(nn < REL_EXACT, nn, jnp.minimum(large, REL_BUCKETS - 1))
        val = jnp.zeros(dist.shape, F32)
        for k in range(REL_BUCKETS):
            val = jnp.where(bucket == k, tbl[k], val)
        return val

    shape = (KEY_TILE, r_l)
    kj = lax.broadcasted_iota(jnp.int32, shape, 0) // rpk
    qi = lax.broadcasted_iota(jnp.int32, shape, 1) % qbp
    far = tbl[REL_BUCKETS - 1]
    for r0 in range(0, rpk * KEY_TILE, KEY_TILE):
        rows = slice(r0, r0 + KEY_TILE)
        d0 = qi - (kj + r0 // rpk)
        img_ref[IMG_DIAG, rows] = jnp.where(d0 >= 0, (bias_of(d0) - far) * LOG2E, NEG)
        img_ref[IMG_PREV, rows] = (bias_of(d0 + KEY_TILE) - far) * LOG2E
        img_ref[IMG_ZERO, rows] = jnp.zeros(shape, F32)
        img_ref[IMG_EDGE, rows] = jnp.where(d0 < 0, 0.0, NEG)
        img_ref[IMG_NEG, rows] = jnp.full(shape, NEG, F32)
    for r0 in range(0, ct_ref.shape[0], KEY_TILE):
        dist = qi + CMP_STRIDE * (off2 - r0 // rpk - kj) - (2 * CMP_STRIDE - 1)
        ct_ref[r0:r0 + KEY_TILE, :] = jnp.where(dist >= 0, bias_of(dist) * LOG2E, NEG)


def _bias_images(rel_bias, heads, qbp, rpk, nct, off2, name):
    r_l = heads * qbp
    groups = N_HEADS // heads
    return pl.pallas_call(
        functools.partial(_bias_kernel, heads=heads, qbp=qbp, rpk=rpk, off2=off2),
        grid=(groups,),
        in_specs=[pl.BlockSpec(memory_space=pltpu.SMEM)],
        out_specs=[pl.BlockSpec((None, N_IMG, rpk * KEY_TILE, r_l), lambda g: (g, 0, 0, 0)),
                   pl.BlockSpec((None, rpk * nct, r_l), lambda g: (g, 0, 0))],
        out_shape=[jax.ShapeDtypeStruct((groups, N_IMG, rpk * KEY_TILE, r_l), F32),
                   jax.ShapeDtypeStruct((groups, rpk * nct, r_l), F32)],
        compiler_params=_params(("arbitrary",), 32),
        name=name,
    )(rel_bias)


def _pv(v, p):
    return lax.dot_general(v, p.astype(BF16), (((0,), (0,)), ((), ())), preferred_element_type=F32)


def _online_update(carry, st, v):
    m_run, l_run, acc = carry
    m_new = jnp.maximum(m_run, jnp.max(st, axis=0, keepdims=True))
    alpha = jnp.exp2(m_run - m_new)
    p = jnp.exp2(st - m_new)
    l_new = alpha * l_run + jnp.sum(p, axis=0, keepdims=True)
    return m_new, l_new, alpha * acc + _pv(v, p)


def _softmax_init(r_l):
    return jnp.full((1, r_l), NEG, F32), jnp.zeros((1, r_l), F32), jnp.zeros((HEAD_DIM, r_l), F32)


def _block_onehot(n_keys):
    kj = lax.broadcasted_iota(jnp.int32, (n_keys, KEY_TILE), 0)
    col = lax.broadcasted_iota(jnp.int32, (n_keys, KEY_TILE), 1)
    return jnp.where(col == kj // SEL_BLOCK, 1.0, 0.0).astype(BF16)


def _query_lanes(q):
    q = q * Q_SCALE
    qg = jnp.concatenate([q[:, n * HEAD_DIM:(n + 1) * HEAD_DIM] for n in range(HPG)], axis=0)
    return qg.T.astype(BF16)


def _cmp_and_select(q_t, kc, vc, cbias, t0, ps_ref, *, qbp, nc, nb, side_work=None):
    r_l = HPG * qbp
    s = _dot(kc, q_t) + cbias
    valid = cbias > 0.5 * NEG
    mx = jnp.max(s, axis=0, keepdims=True)
    mx = jnp.where(mx > 0.5 * NEG, mx, 0.0)
    e = jnp.where(valid, jnp.exp2(s - mx), 0.0)
    p_c = e / jnp.maximum(jnp.sum(e, axis=0, keepdims=True), 1e-30)
    o_c = _pv(vc, p_c)

    if qbp == KEY_TILE:
        psum = p_c[:, 0:qbp]
        for n in range(1, HPG):
            psum = psum + p_c[:, n * qbp:(n + 1) * qbp]
    else:
        psum = p_c
        for n in range(1, HPG):
            psum = psum + pltpu.roll(p_c, n * qbp, 1)
    ps_ref[...] = psum
    per_sel = SEL_BLOCK // CMP_STRIDE
    nbi = nc // per_sel
    imp = ps_ref[pl.ds(0, nbi, stride=per_sel), :]
    for c in range(1, per_sel):
        imp = imp + ps_ref[pl.ds(c, nbi, stride=per_sel), :]
    if nb > nbi:
        imp = jnp.concatenate([imp, jnp.zeros((nb - nbi, KEY_TILE), F32)], axis=0)
    jb = lax.broadcasted_iota(jnp.int32, (nb, KEY_TILE), 0)
    tq = t0 + lax.broadcasted_iota(jnp.int32, (nb, KEY_TILE), 1) % qbp
    cur = tq // SEL_BLOCK
    forced = (jb == 0) | (jb == cur) | (jb == cur - 1)
    score = jnp.where(jb <= cur, jnp.where(forced, FORCE_SCORE, imp), NEG)
    jbf = jb.astype(F32)
    side = side_work() if side_work is not None else None
    selm = jnp.full((nb, KEY_TILE), NEG, F32)
    for _ in range(N_SEL):
        best = jnp.max(score, axis=0, keepdims=True)
        first = jnp.min(jnp.where(score == best, jbf, 1e9), axis=0, keepdims=True)
        pick = jbf == first
        selm = jnp.where(pick, 0.0, selm)
        score = jnp.where(pick, 3.0 * NEG, score)
    if r_l > KEY_TILE:
        selm = jnp.concatenate([selm] * (r_l // KEY_TILE), axis=1)
    return o_c, selm, side


def _combine(gt, o_c, sel, win, qbp):
    (_, l_s, acc_s), (_, l_w, acc_w) = sel, win
    o_t = gt[0:1] * o_c + gt[1:2] * (acc_s / l_s) + gt[2:3] * (acc_w / l_w)
    return o_t.T


def _attn_prompt_kernel(q_ref, gt_ref, kc_ref, vc_ref, ks_ref, vs_ref, kw_ref, vw_ref, img_ref, ct_ref,
                        o_ref, selg_ref, ps_ref, *, nc, nb, off2):
    qbp = KEY_TILE
    r_l = HPG * qbp
    n_win = WINDOW // KEY_TILE
    grp_keys = SEL_GROUP * KEY_TILE
    grp_blocks = grp_keys // SEL_BLOCK
    n_grp, pad_rows = selg_ref.shape[1], selg_ref.shape[2] - grp_blocks
    onehot = _block_onehot(grp_keys)
    zpad = jnp.zeros((KEY_TILE - selg_ref.shape[2], r_l), BF16)

    def img_of(d, edge):
        idx = jnp.minimum(d, IMG_ZERO)
        if edge:
            idx = jnp.where(d == n_win, IMG_EDGE, idx)
        return img_ref[jnp.where(d < 0, IMG_NEG, idx)]

    def grp_rows(it):
        return pl.ds(pl.multiple_of(it * grp_keys, grp_keys), grp_keys)

    def before_selection(t):
        i_tile = PROMPT_TILES * pl.program_id(1) + t
        q_t = _query_lanes(q_ref[t * qbp:(t + 1) * qbp, :])

        def window():
            first_tile = jnp.maximum(i_tile - n_win, 0)
            rows = pl.ds(pl.multiple_of(first_tile * KEY_TILE, KEY_TILE), (n_win + 1) * KEY_TILE)
            add = jnp.concatenate([img_of(i_tile - (first_tile + w), True) for w in range(n_win + 1)], axis=0)
            return _online_update(_softmax_init(r_l), _dot(kw_ref[rows, :], q_t) + add, vw_ref[rows, :])

        start = pl.multiple_of(off2 - i_tile * (KEY_TILE // CMP_STRIDE), 8)
        o_c, selm, win = _cmp_and_select(q_t, kc_ref[...], vc_ref[...], ct_ref[pl.ds(start, nc), :],
                                         i_tile * KEY_TILE, ps_ref.at[t], qbp=qbp, nc=nc, nb=nb, side_work=window)
        selg_ref[t] = jnp.concatenate(
            [selm.reshape(n_grp, grp_blocks, r_l), jnp.zeros((n_grp, pad_rows, r_l), F32)], axis=1).astype(BF16)
        return i_tile, q_t, o_c, win

    def selected_branch(t, i_tile, q_t):
        n_it = (i_tile + SEL_GROUP) // SEL_GROUP
        n_far = jnp.maximum(i_tile - 1, 0) // SEL_GROUP

        def scores(it):
            k_aug = jnp.concatenate([ks_ref[grp_rows(it), :], onehot], axis=1)
            q_aug = jnp.concatenate([q_t, selg_ref[t, it], zpad], axis=0)
            return _dot(k_aug, q_aug)

        def multi_body(itn, states):
            sts = [scores(SEL_STATES * itn + k) for k in range(SEL_STATES)]
            return tuple(_online_update(states[k], sts[k], vs_ref[grp_rows(SEL_STATES * itn + k), :])
                         for k in range(SEL_STATES))

        def single_body(it, state):
            st = scores(it) + jnp.concatenate(
                [img_of(i_tile - (it * SEL_GROUP + w), False) for w in range(SEL_GROUP)], axis=0)
            return _online_update(state, st, vs_ref[grp_rows(it), :])

        n_multi = n_far // SEL_STATES
        states = lax.fori_loop(0, n_multi, multi_body, tuple(_softmax_init(r_l) for _ in range(SEL_STATES)))
        states = (lax.fori_loop(SEL_STATES * n_multi, n_it, single_body, states[0]),) + states[1:]
        m_all = states[0][0]
        for st_k in states[1:]:
            m_all = jnp.maximum(m_all, st_k[0])
        l_all, acc_all = None, None
        for m_k, l_k, acc_k in states:
            f_k = jnp.exp2(m_k - m_all)
            l_all = f_k * l_k if l_all is None else l_all + f_k * l_k
            acc_all = f_k * acc_k if acc_all is None else acc_all + f_k * acc_k
        return m_all, l_all, acc_all

    heads = [before_selection(t) for t in range(PROMPT_TILES)]
    for t, (i_tile, q_t, o_c, win) in enumerate(heads):
        sel = selected_branch(t, i_tile, q_t)
        o_r = _combine(gt_ref[t], o_c, sel, win, qbp)
        for n in range(HPG):
            o_ref[t * qbp:(t + 1) * qbp, n * HEAD_DIM:(n + 1) * HEAD_DIM] = \
                o_r[n * qbp:(n + 1) * qbp].astype(o_ref.dtype)


SAMPLE_GROUP_COL = 64
SAMPLE_FAR_PARTS = 6


def _softmax_segments(segs):
    m = jnp.max(segs[0][0], axis=0, keepdims=True)
    for st, _ in segs[1:]:
        m = jnp.maximum(m, jnp.max(st, axis=0, keepdims=True))
    l_sum, acc = None, None
    for st, v in segs:
        p = jnp.exp2(st - m)
        l_part, a_part = jnp.sum(p, axis=0, keepdims=True), _pv(v, p)
        l_sum = l_part if l_sum is None else l_sum + l_part
        acc = a_part if acc is None else acc + a_part
    return m, l_sum, acc


def _attn_sample_kernel(pt_ref, q_ref, gt_ref, kc_ref, vc_ref, kw_ref, vw_ref, kst_ref, vst_ref, kwt_ref, vwt_ref,
                        img_ref, ct_ref, feat_ref, ks_hbm, vs_hbm, o_ref, kwo_ref, vwo_ref, selm_ref, kbuf, vbuf,
                        sem, *, qb, nb):
    b = pl.program_id(0)
    slot = _prefetch_pages(pt_ref, b, pl.num_programs(0), [(ks_hbm, kbuf, sem.at[0]), (vs_hbm, vbuf, sem.at[1])])
    lanes = N_HEADS * qb
    past_rows = kbuf.shape[1]
    past = past_rows // N_KV
    tile_rows = KEY_TILE * N_KV
    tail_rows = kst_ref.shape[0]
    win_rows = kw_ref.shape[0]
    cmp_rows = kc_ref.shape[0]

    q = q_ref[...] * Q_SCALE
    q_t = jnp.concatenate([q[:, h * HEAD_DIM:(h + 1) * HEAD_DIM] for h in range(N_HEADS)], axis=0).T.astype(BF16)
    lane = lax.broadcasted_iota(jnp.int32, (N_KV, lanes), 1)
    grp_rows = jnp.where(lane // (HPG * qb) == lax.broadcasted_iota(jnp.int32, (N_KV, lanes), 0), 0.0, NEG)

    q_aug = jnp.concatenate(
        [q_t, jnp.concatenate([jnp.zeros((SAMPLE_GROUP_COL, lanes), F32), grp_rows,
                               jnp.zeros((KEY_TILE - SAMPLE_GROUP_COL - N_KV, lanes), F32)], axis=0).astype(BF16)],
        axis=0)

    def scores(k_rows, feat):
        return _dot(jnp.concatenate([k_rows.astype(BF16), feat], axis=1), q_aug)

    far = past_rows - tile_rows
    part = far // SAMPLE_FAR_PARTS
    half = win_rows // 2
    sel_rows = [slice(r0, r0 + part) for r0 in range(0, far, part)] + [slice(far, past_rows)]
    pending = [lambda rows=rows: scores(kbuf[slot, rows, :], feat_ref[rows, :]) for rows in sel_rows]
    pending += [lambda: scores(kst_ref[...], feat_ref[0:tail_rows, :]),
                lambda: scores(kw_ref[0:half, :], feat_ref[0:half, :]),
                lambda: scores(kw_ref[half:win_rows, :], feat_ref[half:win_rows, :]),
                lambda: scores(kwt_ref[...], feat_ref[0:tail_rows, :])]
    raw = []

    def issue(n=1):
        for _ in range(n):
            if len(raw) < len(pending):
                raw.append(pending[len(raw)]())

    s = scores(kc_ref[...], feat_ref[0:cmp_rows, :]) + ct_ref[...]
    issue()
    mx = jnp.max(s, axis=0, keepdims=True)
    mx = jnp.where(mx > 0.5 * NEG, mx, 0.0)
    e = jnp.where(s > 0.5 * NEG, jnp.exp2(s - mx), 0.0)
    p_c = e / jnp.maximum(jnp.sum(e, axis=0, keepdims=True), 1e-30)
    o_c = _pv(vc_ref[...], p_c)
    issue()

    l32 = lax.broadcasted_iota(jnp.int32, p_c.shape, 1) % (HPG * qb)
    psum = p_c
    for n in range(1, HPG):
        sh = n * qb
        psum = psum + jnp.where(l32 >= sh, pltpu.roll(p_c, sh, 1), pltpu.roll(p_c, sh + lanes - HPG * qb, 1))
    rows_per_block = SEL_BLOCK // CMP_STRIDE * N_KV
    nbi = cmp_rows // rows_per_block
    imp = jnp.sum(psum.reshape(nbi, rows_per_block, lanes), axis=1)
    imp = jnp.concatenate([imp, jnp.zeros((nb - nbi, lanes), F32)], axis=0)
    jb = lax.broadcasted_iota(jnp.int32, (nb, lanes), 0)
    cur = (past + lax.broadcasted_iota(jnp.int32, (nb, lanes), 1) % qb) // SEL_BLOCK
    forced = (jb == 0) | (jb == cur) | (jb == cur - 1)
    score = jnp.where(jb <= cur, jnp.where(forced, FORCE_SCORE, imp), NEG)
    jbf = jb.astype(F32)
    selm = jnp.full((nb, lanes), NEG, F32)
    for rnd in range(N_SEL):
        best = jnp.max(score, axis=0, keepdims=True)
        first = jnp.min(jnp.where(score == best, jbf, 1e9), axis=0, keepdims=True)
        pick = jbf == first
        selm = jnp.where(pick, 0.0, selm)
        score = jnp.where(pick, 3.0 * NEG, score)
        if rnd % 2 == 1:
            issue()
    issue(len(pending))
    sel_st = raw[:len(sel_rows)]
    sel_st[-1] = sel_st[-1] + img_ref[IMG_PREV]
    st_new, st_old, st_mid, st_wn = raw[len(sel_rows):]
    st_new = st_new + img_ref[IMG_DIAG, 0:tail_rows]
    st_old = jnp.concatenate([st_old[:tile_rows] + img_ref[IMG_EDGE], st_old[tile_rows:]], axis=0)
    st_mid = jnp.concatenate([st_mid[:half - tile_rows], st_mid[half - tile_rows:] + img_ref[IMG_PREV]], axis=0)
    st_wn = st_wn + img_ref[IMG_DIAG, 0:tail_rows]

    for j in range(nb):
        selm_ref[j] = jnp.broadcast_to(selm[j:j + 1, :], (8, lanes))
    block_rows = SEL_BLOCK * N_KV

    def select(st, row0):
        n_rows = st.shape[0]
        nblk = -(-n_rows // block_rows)
        per = n_rows // nblk // 8
        mask = selm_ref[row0 // block_rows:row0 // block_rows + nblk]
        return (st.reshape(nblk, per, 8, lanes) + mask[:, None]).reshape(n_rows, lanes)

    segs = [(select(st, rows.start), vbuf[slot, rows, :].astype(BF16)) for st, rows in zip(sel_st, sel_rows)]
    segs.append((select(st_new, past_rows), vst_ref[...].astype(BF16)))
    sel = _softmax_segments(segs)

    win = _softmax_segments([(st_old, vw_ref[0:half, :].astype(BF16)),
                             (st_mid, vw_ref[half:win_rows, :].astype(BF16)),
                             (st_wn, vwt_ref[...].astype(BF16))])

    o_r = _combine(gt_ref[...], o_c, sel, win, qb)
    for h in range(N_HEADS):
        o_ref[:, h * HEAD_DIM:(h + 1) * HEAD_DIM] = o_r[h * qb:(h + 1) * qb]

    for src, new, dst in ((kw_ref, kwt_ref, kwo_ref), (vw_ref, vwt_ref, vwo_ref)):
        dst[0:win_rows - tail_rows, :] = src[tail_rows:win_rows, :]
        dst[win_rows - tail_rows:win_rows, :] = new[...]


def _attn_prompt(qkv_f, qkv_b, gates_t, kcmp, vcmp, img, ct, seq, off2):
    qbp = KEY_TILE
    r_l = HPG * qbp
    n_tiles = seq // qbp
    nc = seq // CMP_STRIDE
    nb = seq // SEL_BLOCK
    q_cols = N_HEADS * HEAD_DIM // HEAD_DIM

    def kv_spec(which):
        return pl.BlockSpec((seq, HEAD_DIM), functools.partial(lambda g, i, c: (0, c + g), c=q_cols + which * N_KV))

    cmp_spec = pl.BlockSpec((nc, HEAD_DIM), lambda g, i: (0, g))
    return pl.pallas_call(
        functools.partial(_attn_prompt_kernel, nc=nc, nb=nb, off2=off2),
        grid=(N_KV, n_tiles // PROMPT_TILES),
        in_specs=[pl.BlockSpec((PROMPT_TILES * qbp, HPG * HEAD_DIM), lambda g, i: (i, g)),
                  pl.BlockSpec((PROMPT_TILES, None, 3, r_l), lambda g, i: (i, g, 0, 0)),
                  cmp_spec, cmp_spec,
                  kv_spec(2), kv_spec(3), kv_spec(4), kv_spec(5),
                  pl.BlockSpec((None, N_IMG, KEY_TILE, r_l), lambda g, i: (g, 0, 0, 0)),
                  pl.BlockSpec((None, ct.shape[1], r_l), lambda g, i: (g, 0, 0))],
        out_specs=pl.BlockSpec((PROMPT_TILES * qbp, HPG * HEAD_DIM), lambda g, i: (i, g)),
        out_shape=jax.ShapeDtypeStruct((seq, N_HEADS * HEAD_DIM), BF16),
        scratch_shapes=[pltpu.VMEM((PROMPT_TILES, nb * SEL_BLOCK // (SEL_GROUP * KEY_TILE), 16, r_l), BF16),
                        pltpu.VMEM((PROMPT_TILES, nc, KEY_TILE), F32)],
        compiler_params=_params(("arbitrary", "arbitrary"), 48),
        name="nsa_prompt",
    )(qkv_f, gates_t, kcmp, vcmp, qkv_b, qkv_b, qkv_b, qkv_b, img, ct)


def _key_features(n_rows):
    r = jnp.arange(n_rows, dtype=jnp.int32)[:, None]
    c = jnp.arange(KEY_TILE, dtype=jnp.int32)[None, :]
    hit = (c == r // (SEL_BLOCK * N_KV)) | (c == SAMPLE_GROUP_COL + r % N_KV)
    return hit.astype(BF16)


def _attn_sample(page_table, q_new, gates_t, kcmp, vcmp, kw, vw, ks_t, vs_t, kw_t, vw_t, img, ct,
                 ks_cache, vs_cache, qb):
    n_seq, n_pages = page_table.shape
    past = n_pages * PAGE
    lanes = N_HEADS * qb
    nb = 8 * (-(-(past // SEL_BLOCK + 2) // 8))
    assert nb <= SAMPLE_GROUP_COL and lanes == KEY_TILE
    qw = N_HEADS * HEAD_DIM
    tail_rows = ks_t.shape[1]
    cmp_rows = past // CMP_STRIDE * N_KV
    feat = _key_features(past * N_KV + tail_rows)
    tail_spec = pl.BlockSpec((None, tail_rows, HEAD_DIM), lambda b, pt: (b, 0, 0))
    win_spec = pl.BlockSpec((WINDOW * N_KV, HEAD_DIM), lambda b, pt: (b, 0))
    cmp_spec = pl.BlockSpec((cmp_rows, HEAD_DIM), lambda b, pt: (b, 0))
    any_spec = pl.BlockSpec(memory_space=pl.ANY)
    return pl.pallas_call(
        functools.partial(_attn_sample_kernel, qb=qb, nb=nb),
        grid_spec=pltpu.PrefetchScalarGridSpec(
            num_scalar_prefetch=1, grid=(n_seq,),
            in_specs=[pl.BlockSpec((qb, qw), lambda b, pt: (b, 0)),
                      pl.BlockSpec((None, 3, lanes), lambda b, pt: (b, 0, 0)),
                      cmp_spec, cmp_spec, win_spec, win_spec,
                      tail_spec, tail_spec, tail_spec, tail_spec,
                      pl.BlockSpec((None,) + img.shape[1:], lambda b, pt: (0, 0, 0, 0)),
                      pl.BlockSpec((None,) + ct.shape[1:], lambda b, pt: (0, 0, 0)),
                      pl.BlockSpec(feat.shape, lambda b, pt: (0, 0)),
                      any_spec, any_spec],
            out_specs=[pl.BlockSpec((qb, qw), lambda b, pt: (b, 0)), win_spec, win_spec],
            scratch_shapes=[pltpu.VMEM((nb, 8, lanes), F32),
                            pltpu.VMEM((2, n_pages * PAGE_ROWS, HEAD_DIM), F32),
                            pltpu.VMEM((2, n_pages * PAGE_ROWS, HEAD_DIM), F32),
                            pltpu.SemaphoreType.DMA((2, 2))]),
        out_shape=[jax.ShapeDtypeStruct((n_seq * qb, qw), F32),
                   jax.ShapeDtypeStruct(kw.shape, kw.dtype), jax.ShapeDtypeStruct(vw.shape, vw.dtype)],
        compiler_params=_params(("arbitrary",), 56),
        name="nsa_sample",
    )(page_table, q_new, gates_t, kcmp, vcmp, kw, vw, ks_t, vs_t, kw_t, vw_t, img, ct, feat, ks_cache, vs_cache)


def _gates_to_lanes(g_sig, n_tiles, q_rows, qbp):
    g5 = g_sig.reshape(n_tiles, q_rows, 3, N_KV, HPG)
    if qbp > q_rows:
        g5 = jnp.pad(g5, ((0, 0), (0, qbp - q_rows), (0, 0), (0, 0), (0, 0)))
    return g5.transpose(0, 3, 2, 4, 1).reshape(n_tiles, N_KV, 3, HPG * qbp)


def _history_rows(state, dec_seq):
    n_seq, _, c = state.shape
    z = jnp.zeros((n_seq, dec_seq - 2, c), state.dtype)
    s1 = jnp.concatenate([state[:, 1:2], jnp.zeros((n_seq, 1, c), state.dtype), z], axis=1)
    s2 = jnp.concatenate([state, z], axis=1)
    return s1.reshape(n_seq * dec_seq, c), s2.reshape(n_seq * dec_seq, c)


def _cmp_weights(pe, w1, w2):
    half = CMP_STRIDE * HEAD_DIM
    w1cat = jnp.concatenate([w1[:half], w1[half:]], axis=1).astype(BF16)
    return pe.reshape(2, half), w1cat, w2.astype(BF16)


def _token_group_rows(x):
    return x.reshape(-1, HEAD_DIM)


def kernel(x_prompt, x_sample, cache_k_cmp, cache_v_cmp, cache_k_sel, cache_v_sel, state_k_win, state_v_win,
           state_conv, state_ffn_conv, page_table, rel_bias, w_in, conv_w, conv_b, w_br_conv, w_br_nsa, w_out,
           pe_cmp_k, w_cmp_k1, w_cmp_k2, pe_cmp_v, w_cmp_v1, w_cmp_v2, ln1_g, ln1_b, w_ffn_in, ffn_conv_w,
           ffn_conv_b, w_ffn_out, ln2_g, ln2_b):
    seq = x_prompt.shape[1]
    n_seq, dec_seq = x_sample.shape[0], x_sample.shape[1]
    n_s = n_seq * dec_seq
    past = page_table.shape[1] * PAGE
    kvw = N_KV * HEAD_DIM
    qw = N_HEADS * HEAD_DIM
    tm = 1024

    x_s2 = x_sample.reshape(n_s, D_MODEL)
    x_bf = jnp.concatenate([x_prompt[0].astype(BF16), x_s2.astype(BF16)], axis=0)
    c_qkv = 3 * D_CONV
    c_gate = c_qkv + qw + 6 * kvw
    c_mix = c_gate + 3 * N_HEADS
    w_main = w_in[0].astype(BF16)
    w_gate = jnp.pad(w_main[:, c_gate:c_mix], ((0, 0), (0, 128 - 3 * N_HEADS)))
    w_mix = w_main[:, c_mix:]

    s1, s2 = _history_rows(state_conv[0], dec_seq)
    tn_c = 256
    z_conv, u_conv = _gconv(x_bf, w_main, (0, D_CONV // tn_c, 2 * D_CONV // tn_c), conv_w[0], conv_b, s1, s2,
                            seq, tm, tn_c, 48, "proj_conv")
    qkv_f, qkv_b = _proj(x_bf, w_main, c_qkv, qw + 6 * kvw, tm, 512)
    g_sig = _gate_proj(x_bf, w_gate, tm)[:, :3 * N_HEADS]

    pe_k, w1_k, w2_k = _cmp_weights(pe_cmp_k[0], w_cmp_k1[0], w_cmp_k2[0])
    pe_v, w1_v, w2_v = _cmp_weights(pe_cmp_v[0], w_cmp_v1[0], w_cmp_v2[0])
    chunk_w = CMP_STRIDE * kvw
    kc_p = qkv_b[:seq, qw:qw + kvw].reshape(seq // CMP_STRIDE, chunk_w)
    vc_p = qkv_b[:seq, qw + kvw:qw + 2 * kvw].reshape(seq // CMP_STRIDE, chunk_w)
    kcmp_p = _compress(kc_p, pe_k, w1_k, w2_k, "compress_k_prompt")
    vcmp_p = _compress(vc_p, pe_v, w1_v, w2_v, "compress_v_prompt")
    kcmp_s, vcmp_s = _compress_paged(page_table, _token_group_rows(cache_k_cmp), _token_group_rows(cache_v_cmp),
                                     (pe_k, w1_k, w2_k), (pe_v, w1_v, w2_v))

    off2_p = (seq - KEY_TILE) // CMP_STRIDE
    nct_p = KEY_TILE * (-(-(off2_p + seq // CMP_STRIDE) // KEY_TILE))
    img_p, ct_p = _bias_images(rel_bias, HPG, KEY_TILE, 1, nct_p, off2_p, "bias_prompt")
    gates_p = _gates_to_lanes(g_sig[:seq], seq // KEY_TILE, KEY_TILE, KEY_TILE)
    o_p = _attn_prompt(qkv_f, qkv_b, gates_p, kcmp_p, vcmp_p, img_p, ct_p, seq, off2_p)

    chunks_s = past // CMP_STRIDE
    img_s, ct_s = _bias_images(rel_bias, N_HEADS, dec_seq, N_KV, chunks_s, chunks_s, "bias_sample")
    gates_s = g_sig[seq:].reshape(n_seq, dec_seq, 3, N_HEADS).transpose(0, 2, 3, 1).reshape(n_seq, 3, N_HEADS * dec_seq)
    new_f = qkv_f[seq:].reshape(n_seq, dec_seq, qw + 6 * kvw)

    def tail(col):
        return new_f[:, :, col:col + kvw].reshape(n_seq, dec_seq * N_KV, HEAD_DIM)

    o_s, kw_next, vw_next = _attn_sample(page_table, qkv_f[seq:, :qw], gates_s,
                       kcmp_s.reshape(-1, HEAD_DIM), vcmp_s.reshape(-1, HEAD_DIM),
                       _token_group_rows(state_k_win), _token_group_rows(state_v_win),
                       tail(qw + 2 * kvw), tail(qw + 3 * kvw), tail(qw + 4 * kvw), tail(qw + 5 * kvw),
                       img_s, ct_s, _token_group_rows(cache_k_sel), _token_group_rows(cache_v_sel), dec_seq)

    mix = _merge(x_bf, z_conv, o_p, o_s, w_br_conv[0], w_br_nsa[0], w_mix, 512, 256)
    r1 = _resid_mm_split(mix, w_out[0].astype(BF16), x_prompt[0], x_s2, tm, 512, 48, "out_proj")
    h_f, h_b = _layer_norm(r1, ln1_g, ln1_b, 256, "ln1")
    f1, f2 = _history_rows(state_ffn_conv[0], dec_seq)
    tn_f = 256
    act, gp = _gconv(h_b, w_ffn_in[0], (D_FF // tn_f, 0), ffn_conv_w[0], ffn_conv_b, f1, f2,
                     seq, tm, tn_f, 56, "ffn_in")
    r2 = _resid_mm(act, w_ffn_out[0].astype(BF16), h_f, 512, 256, 56, "ffn_out")
    y_p, y_s = _layer_norm_split(r2, ln2_g, ln2_b, seq, 256, "ln2")

    def rows_p(col):
        return qkv_f[:seq, col:col + kvw].reshape(1, 1, seq, N_KV, HEAD_DIM)

    def rows_s(col):
        return new_f[:, :, col:col + kvw].reshape(1, n_seq, dec_seq, N_KV, HEAD_DIM)

    def win_p(col):
        return qkv_f[seq - WINDOW:seq, col:col + kvw].reshape(1, 1, WINDOW, N_KV, HEAD_DIM)


    def last2(u):
        c = u.shape[1]
        return (u[seq - 2:seq].reshape(1, 1, 2, c),
                u[seq:].reshape(n_seq, dec_seq, c)[:, dec_seq - 2:].reshape(1, n_seq, 2, c))

    conv_p, conv_s = last2(u_conv)
    ffn_p, ffn_s = last2(gp)
    c = qw
    return (y_p.reshape(1, seq, D_MODEL), y_s.reshape(n_seq, dec_seq, D_MODEL),
            rows_p(c), rows_s(c), rows_p(c + kvw), rows_s(c + kvw),
            rows_p(c + 2 * kvw), rows_s(c + 2 * kvw), rows_p(c + 3 * kvw), rows_s(c + 3 * kvw),
            win_p(c + 4 * kvw), kw_next.reshape(state_k_win.shape),
            win_p(c + 5 * kvw), vw_next.reshape(state_v_win.shape),
            conv_p, conv_s, ffn_p, ffn_s)
```
